```python
import jax, jax.numpy as jnp
from jax import lax
import numpy as np

D_MODEL = 2048
BATCH = 2
SEQ = 4096
DEPTH = 1
DEC_BATCH = 8
DEC_SEQ = 1
PAST_LEN = 16384
PAGE_SIZE = 128

WINDOWS = (128, 512, 2048)
DILATIONS = (1, 4, 16)
N_GROUPS = 3
A_HEADS = 8
A_HEAD_DIM = 128
A_QKV = N_GROUPS * A_HEADS * A_HEAD_DIM
A_WIDTH = A_HEADS * A_HEAD_DIM
R_HEADS = 8
R_KEY_DIM = 256
R_VAL_DIM = 256
R_QK = R_HEADS * R_KEY_DIM
R_WIDTH = R_HEADS * R_VAL_DIM
R_CHUNK = 128
PLE_DIM = 256
EPS = 1e-6
SPLITS = (A_QKV, A_QKV, A_QKV, A_WIDTH, R_QK, R_QK, R_WIDTH, R_WIDTH, D_MODEL, D_MODEL)
N_IN = 3 * A_QKV + A_WIDTH + 2 * R_QK + 2 * R_WIDTH + 2 * D_MODEL

kernel_name = 'hybrid_dilated_attn_retention_step'


def rms_norm(x, g):
    xf = x.astype(jnp.float32)
    y = xf * lax.rsqrt(jnp.mean(xf * xf, axis=-1, keepdims=True) + EPS)
    return (y * g.astype(jnp.float32)).astype(x.dtype)


def alibi_slopes():
    n = N_GROUPS * A_HEADS
    return jnp.exp2(-8.0 * (jnp.arange(n, dtype=jnp.float32) + 1.0) / n).reshape(N_GROUPS, A_HEADS)


def retention_log_decay():
    return jnp.log1p(-jnp.exp2(-5.0 - jnp.arange(R_HEADS, dtype=jnp.float32)))


def dilated_band(q, k, v, dil, window, slope):
    B, T, H, hd = q.shape
    L = window // dil
    n = T // dil
    n_pad = -(-n // L) * L
    nb = n_pad // L
    Bd = B * dil

    def to_sub(x):
        x = x.reshape(B, n, dil, H, hd).transpose(0, 2, 1, 3, 4).reshape(Bd, n, H, hd)
        x = jnp.pad(x, ((0, 0), (0, n_pad - n), (0, 0), (0, 0)))
        return x.reshape(Bd, nb, L, H, hd)

    def with_prev(x):
        prev = jnp.pad(x[:, :-1], ((0, 0), (1, 0), (0, 0), (0, 0), (0, 0)))
        return jnp.concatenate([prev, x], axis=2)

    qb = to_sub(q)
    kc = with_prev(to_sub(k))
    vc = with_prev(to_sub(v))
    s = jnp.einsum('bnqhd,bnkhd->bnhqk', qb, kc).astype(jnp.float32)
    qi = jnp.arange(L)
    kj = jnp.arange(2 * L)
    dist = qi[:, None] + L - kj[None, :]
    valid = (dist >= 0) & (dist <= L)
    valid = valid[None] & ~((jnp.arange(nb)[:, None, None] == 0) & (kj[None, None, :] < L))
    bias = -slope[:, None, None] * (dist * dil).astype(jnp.float32)[None]
    s = jnp.where(valid[None, :, None], s + bias[None, None], -jnp.inf)
    m = jnp.max(s, axis=-1, keepdims=True)
    e = jnp.exp(s - m)
    den = jnp.sum(e, axis=-1)
    o = jnp.einsum('bnhqk,bnkhd->bnqhd', e, vc.astype(jnp.float32))
    o = o / jnp.swapaxes(den, 2, 3)[..., None]
    lse = jnp.swapaxes(m[..., 0] + jnp.log(den), 2, 3)

    def from_sub(x):
        x = x.reshape((Bd, n_pad) + x.shape[3:])[:, :n]
        x = x.reshape((B, dil, n) + x.shape[2:])
        x = jnp.swapaxes(x, 1, 2)
        return x.reshape((B, T) + x.shape[3:])

    return from_sub(o), from_sub(lse)


def dilated_gather(q, k_all, v_all, past, dil, window, slope):
    S = q.shape[1]
    offs = jnp.arange(window // dil + 1) * dil
    idx = past + jnp.arange(S)[:, None] - offs[None, :]
    valid = idx >= 0
    idx = jnp.maximum(idx, 0)
    kg = k_all[:, idx]
    vg = v_all[:, idx]
    s = jnp.einsum('bshd,bskhd->bshk', q, kg).astype(jnp.float32)
    s = s - slope[:, None] * offs.astype(jnp.float32)[None, :]
    s = jnp.where(valid[None, :, None, :], s, -jnp.inf)
    m = jnp.max(s, axis=-1, keepdims=True)
    e = jnp.exp(s - m)
    den = jnp.sum(e, axis=-1)
    o = jnp.einsum('bshk,bskhd->bshd', e, vg.astype(jnp.float32)) / den[..., None]
    return o, m[..., 0] + jnp.log(den)


def combine_groups(outs, lses):
    o = jnp.stack(outs, axis=0)
    w = jax.nn.softmax(jnp.stack(lses, axis=0), axis=0)
    return jnp.sum(w[..., None] * o, axis=0)


def retention_chunkwise(q, k, v, state0, log_g):
    B, n, H, dk = q.shape
    dv = v.shape[-1]
    L = R_CHUNK if n % R_CHUNK == 0 else n
    nc = n // L

    def chunks(x):
        return x.astype(jnp.float32).reshape(B, nc, L, H, x.shape[-1]).transpose(1, 0, 2, 3, 4)

    qc, kc, vc = chunks(q), chunks(k), chunks(v)
    pos = jnp.arange(L, dtype=jnp.float32)
    rel = pos[:, None] - pos[None, :]
    intra = jnp.where(rel[None] >= 0, jnp.exp(jnp.maximum(rel, 0.0)[None] * log_g[:, None, None]), 0.0)
    q_decay = jnp.exp((pos[:, None] + 1.0) * log_g[None, :])
    k_decay = jnp.exp((L - 1.0 - pos)[:, None] * log_g[None, :])
    chunk_decay = jnp.exp(L * log_g)

    def step(R, xs):
        qi, ki, vi = xs
        scores = jnp.einsum('bihd,bjhd->bhij', qi, ki) * intra[None]
        inner = jnp.einsum('bhij,bjhe->bihe', scores, vi)
        cross = jnp.einsum('bihd,bhde->bihe', qi, R) * q_decay[None, :, :, None]
        R = R * chunk_decay[None, :, None, None] + jnp.einsum('bjhd,bjhe->bhde', ki * k_decay[None, :, :, None], vi)
        return R, inner + cross

    R, out = lax.scan(step, state0.astype(jnp.float32), (qc, kc, vc))
    out = out.transpose(1, 0, 2, 3, 4).reshape(B, n, H, dv)
    return out, R


def layer(h, p, win_caches, ret_state, g_pre, w_in, g_ret, w_a_out, w_r_out, w_o, g_post, w_ple_gate, w_ple_proj):
    B, S, _ = h.shape
    u = rms_norm(h, g_pre)
    z = jnp.einsum('bsd,de->bse', u, w_in)
    split_points = np.cumsum(np.array(SPLITS))[:-1].tolist()
    qa, ka, va, za, qr, kr, vr, zr, ga, gb = jnp.split(z, split_points, axis=-1)
    qa = qa.reshape(B, S, N_GROUPS, A_HEADS, A_HEAD_DIM) * (A_HEAD_DIM ** -0.5)
    ka = ka.reshape(B, S, N_GROUPS, A_HEADS, A_HEAD_DIM)
    va = va.reshape(B, S, N_GROUPS, A_HEADS, A_HEAD_DIM)
    qr = qr.reshape(B, S, R_HEADS, R_KEY_DIM)
    kr = kr.reshape(B, S, R_HEADS, R_KEY_DIM) * (R_KEY_DIM ** -0.5)
    vr = vr.reshape(B, S, R_HEADS, R_VAL_DIM)

    slopes = alibi_slopes()
    outs, lses, new_win = [], [], []
    for g in range(N_GROUPS):
        q_g, k_g, v_g = qa[:, :, g], ka[:, :, g], va[:, :, g]
        if win_caches is None:
            o, l = dilated_band(q_g, k_g, v_g, DILATIONS[g], WINDOWS[g], slopes[g])
            keep = min(WINDOWS[g], S)
            new_win.append(jnp.stack([k_g[:, S - keep:], v_g[:, S - keep:]], axis=2))
        else:
            buf = win_caches[g]
            past = buf.shape[1]
            full = jnp.concatenate([buf, jnp.stack([k_g, v_g], axis=2).astype(buf.dtype)], axis=1)
            o, l = dilated_gather(q_g, full[:, :, 0], full[:, :, 1], past, DILATIONS[g], WINDOWS[g], slopes[g])
            new_win.append(full[:, S:])
        outs.append(o)
        lses.append(l)
    ya = combine_groups(outs, lses).reshape(B, S, A_WIDTH).astype(h.dtype) * jax.nn.silu(za)

    if ret_state is None:
        ret_state = jnp.zeros((B, R_HEADS, R_KEY_DIM, R_VAL_DIM), jnp.float32)
    yr, r_new = retention_chunkwise(qr, kr, vr, ret_state, retention_log_decay())
    yr = yr * lax.rsqrt(jnp.mean(yr * yr, axis=-1, keepdims=True) + EPS)
    yr = (yr.reshape(B, S, R_WIDTH) * g_ret.astype(jnp.float32)).astype(h.dtype) * jax.nn.silu(zr)

    merged = (jax.nn.sigmoid(ga) * jnp.einsum('bse,ed->bsd', ya, w_a_out)
              + jax.nn.sigmoid(gb) * jnp.einsum('bse,ed->bsd', yr, w_r_out))
    y = jnp.einsum('bsd,de->bse', merged, w_o)
    h = h + rms_norm(y, g_post)
    h = h + jax.nn.sigmoid(jnp.einsum('bsd,de->bse', h, w_ple_gate)) * jnp.einsum('bsp,pd->bsd', p, w_ple_proj)
    return h, new_win, r_new


def setup_inputs(seed: int = 0) -> dict:
    key = jax.random.key(seed)
    ks = jax.random.split(key, 20)
    f32 = jnp.float32

    def nrm(k, shape, scale):
        return scale * jax.random.normal(k, shape, f32)

    def win_shape(w):
        return (DEPTH, DEC_BATCH, min(w, PAST_LEN), 2, A_HEADS, A_HEAD_DIM)

    return {
        'x_prompt': nrm(ks[0], (BATCH, SEQ, D_MODEL), 1.0),
        'x_sample': nrm(ks[1], (DEC_BATCH, DEC_SEQ, D_MODEL), 1.0),
        'cache_win0': nrm(ks[2], win_shape(WINDOWS[0]), 1.0),
        'cache_win1': nrm(ks[3], win_shape(WINDOWS[1]), 1.0),
        'cache_win2': nrm(ks[4], win_shape(WINDOWS[2]), 1.0),
        'state_ret': nrm(ks[5], (DEPTH, DEC_BATCH, R_HEADS, R_KEY_DIM, R_VAL_DIM), 0.25),
        'p_prompt': nrm(ks[6], (DEPTH, BATCH, SEQ, PLE_DIM), 1.0),
        'p_sample': nrm(ks[7], (DEPTH, DEC_BATCH, DEC_SEQ, PLE_DIM), 1.0),
        'g_pre': 1.0 + nrm(ks[8], (DEPTH, D_MODEL), 0.02),
        'w_in': nrm(ks[9], (DEPTH, D_MODEL, N_IN), D_MODEL ** -0.5),
        'g_ret': 1.0 + nrm(ks[10], (DEPTH, R_WIDTH), 0.02),
        'w_a_out': nrm(ks[11], (DEPTH, A_WIDTH, D_MODEL), A_WIDTH ** -0.5),
        'w_r_out': nrm(ks[12], (DEPTH, R_WIDTH, D_MODEL), R_WIDTH ** -0.5),
        'w_o': nrm(ks[13], (DEPTH, D_MODEL, D_MODEL), D_MODEL ** -0.5),
        'g_post': 1.0 + nrm(ks[14], (DEPTH, D_MODEL), 0.02),
        'w_ple_gate': nrm(ks[15], (DEPTH, D_MODEL, D_MODEL), D_MODEL ** -0.5),
        'w_ple_proj': nrm(ks[16], (DEPTH, PLE_DIM, D_MODEL), PLE_DIM ** -0.5),
    }


def reference(x_prompt, x_sample, cache_win0, cache_win1, cache_win2, state_ret, p_prompt, p_sample,
              g_pre, w_in, g_ret, w_a_out, w_r_out, w_o, g_post, w_ple_gate, w_ple_proj):
    hp = x_prompt
    hs = x_sample
    wp = [[] for _ in range(N_GROUPS)]
    ws = [[] for _ in range(N_GROUPS)]
    rp, rs = [], []
    for i in range(DEPTH):
        weights = (g_pre[i], w_in[i], g_ret[i], w_a_out[i], w_r_out[i], w_o[i], g_post[i], w_ple_gate[i], w_ple_proj[i])
        hp, nwp, nrp = layer(hp, p_prompt[i], None, None, *weights)
        hs, nws, nrs = layer(hs, p_sample[i], (cache_win0[i], cache_win1[i], cache_win2[i]), state_ret[i], *weights)
        for g in range(N_GROUPS):
            wp[g].append(nwp[g])
            ws[g].append(nws[g])
        rp.append(nrp)
        rs.append(nrs)
    y_prompt = hp
    y_sample = hs
    win0_prompt = jnp.stack(wp[0])
    win1_prompt = jnp.stack(wp[1])
    win2_prompt = jnp.stack(wp[2])
    ret_prompt = jnp.stack(rp)
    win0_sample = jnp.stack(ws[0])
    win1_sample = jnp.stack(ws[1])
    win2_sample = jnp.stack(ws[2])
    ret_sample = jnp.stack(rs)
    return (y_prompt, y_sample, win0_prompt, win1_prompt, win2_prompt, ret_prompt,
            win0_sample, win1_sample, win2_sample, ret_sample)
```

```python
import functools

import jax
import jax.numpy as jnp
from jax import lax
from jax.experimental import pallas as pl
from jax.experimental.pallas import tpu as pltpu

F32 = jnp.float32
BF16 = jnp.bfloat16

D_MODEL = 2048
N_GROUPS = 3
DILATIONS = (1, 4, 16)
WINDOWS = (128, 512, 2048)
BAND = 128
A_HEADS = 8
A_HEAD_DIM = 128
A_QKV = N_GROUPS * A_HEADS * A_HEAD_DIM
A_WIDTH = A_HEADS * A_HEAD_DIM
R_HEADS = 8
R_DIM = 256
R_WIDTH = R_HEADS * R_DIM
R_CHUNK = 128
PLE_DIM = 256
EPS = 1e-6
N_IN = 3 * A_QKV + A_WIDTH + 4 * R_WIDTH + 2 * D_MODEL

COL_QA, COL_KA, COL_VA = 0, A_QKV, 2 * A_QKV
COL_ZA = 3 * A_QKV
COL_QR = COL_ZA + A_WIDTH
COL_KR = COL_QR + R_WIDTH
COL_VR = COL_KR + R_WIDTH
COL_ZR = COL_VR + R_WIDTH
COL_GA = COL_ZR + R_WIDTH
COL_GB = COL_GA + D_MODEL

NEG = -1e30
ATT_TILE = BAND * max(DILATIONS)
HEAD_COLS = COL_QR
REST_COLS = N_IN - HEAD_COLS

VMEM_LIMIT = 56 * 1024 * 1024


def _params(semantics, vmem=VMEM_LIMIT):
    return pltpu.CompilerParams(dimension_semantics=semantics, vmem_limit_bytes=vmem)


def _silu(x):
    return x * jax.nn.sigmoid(x)


def _rmsnorm_kernel(x_ref, g_ref, o_ref):
    x = x_ref[...]
    y = x * lax.rsqrt(jnp.mean(x * x, axis=-1, keepdims=True) + EPS)
    o_ref[...] = (y * g_ref[...]).astype(o_ref.dtype)


def _rmsnorm(x, g, tm):
    m, d = x.shape
    return pl.pallas_call(
        _rmsnorm_kernel,
        grid=(m // tm,),
        in_specs=[pl.BlockSpec((tm, d), lambda i: (i, 0)),
                  pl.BlockSpec((1, d), lambda i: (0, 0))],
        out_specs=pl.BlockSpec((tm, d), lambda i: (i, 0)),
        out_shape=jax.ShapeDtypeStruct((m, d), BF16),
        compiler_params=_params(("parallel",)),
        name="rmsnorm",
    )(x, g.reshape(1, d))


def _matmul_kernel(u_ref, w_ref, o_ref):
    o_ref[...] = jnp.dot(u_ref[...], w_ref[...], preferred_element_type=F32).astype(o_ref.dtype)


def _matmul_headmajor_kernel(u_ref, w_ref, o_ref):
    acc = jnp.dot(u_ref[...], w_ref[...], preferred_element_type=F32)
    for c in range(o_ref.shape[0]):
        o_ref[c] = acc[:, c * A_HEAD_DIM:(c + 1) * A_HEAD_DIM].astype(o_ref.dtype)


def _in_proj(u, w, col0, ncols, out_dtype, tm, tn, head_major):
    m, k = u.shape
    j0 = col0 // tn
    in_specs = [pl.BlockSpec((tm, k), lambda i, j: (i, 0)),
                pl.BlockSpec((k, tn), lambda i, j: (0, j0 + j))]
    if head_major:
        out_specs = pl.BlockSpec((tn // A_HEAD_DIM, tm, A_HEAD_DIM), lambda i, j: (j, i, 0))
        out_shape = jax.ShapeDtypeStruct((ncols // A_HEAD_DIM, m, A_HEAD_DIM), out_dtype)
        body = _matmul_headmajor_kernel
    else:
        out_specs = pl.BlockSpec((tm, tn), lambda i, j: (i, j))
        out_shape = jax.ShapeDtypeStruct((m, ncols), out_dtype)
        body = _matmul_kernel
    return pl.pallas_call(
        body, grid=(m // tm, ncols // tn), in_specs=in_specs, out_specs=out_specs,
        out_shape=out_shape, compiler_params=_params(("parallel", "arbitrary")),
        name="in_proj_hm" if head_major else "in_proj",
    )(u, w)


def _attn_kernel(slope_ref, q0, q1, q2, k0, k1, k2, v0, v1, v2, za_ref, o_ref,
                 qbuf, kbuf, vbuf, ores, lres, onat, lnat):
    q_refs, k_refs, v_refs = (q0, q1, q2), (k0, k1, k2), (v0, v1, v2)
    h = pl.program_id(1)
    t = pl.program_id(2)
    slot = lax.rem(t, 2)
    nblk_tile = ATT_TILE // BAND

    @pl.when(t == 0)
    def _():
        kbuf[...] = jnp.zeros_like(kbuf)
        vbuf[...] = jnp.zeros_like(vbuf)

    qi = lax.broadcasted_iota(jnp.int32, (BAND, BAND), 0)
    kj = lax.broadcasted_iota(jnp.int32, (BAND, BAND), 1)
    dist_c = (qi - kj).astype(F32)
    scale = A_HEAD_DIM ** -0.5

    for g in range(N_GROUPS):
        dil = DILATIONS[g]
        n = ATT_TILE // dil
        nblk = n // BAND
        base = pl.multiple_of(slot * ATT_TILE, ATT_TILE)
        for r in range(dil):
            rows = pl.ds(r, n, stride=dil) if dil > 1 else pl.ds(0, n)
            qbuf[g, r * n:(r + 1) * n, :] = q_refs[g][rows, :].astype(BF16)
            kbuf[g, pl.ds(base + r * n, n), :] = k_refs[g][rows, :].astype(BF16)
            vbuf[g, pl.ds(base + r * n, n), :] = v_refs[g][rows, :].astype(BF16)

        slope = slope_ref[g, h] * float(dil)
        bias_c = jnp.where(kj <= qi, -slope * dist_c, NEG)
        bias_p = jnp.where(kj >= qi, -slope * (dist_c + BAND), NEG)

        def unit(u, carry, g=g, nblk=nblk, bias_c=bias_c, bias_p=bias_p):
            i = lax.rem(u, nblk)
            has_prev = jnp.logical_or(i > 0, t > 0)
            cur = slot * nblk_tile + u
            prev = jnp.where(i == 0, (1 - slot) * nblk_tile + u + nblk - 1, cur - 1)
            rq = pl.ds(pl.multiple_of(u * BAND, BAND), BAND)
            rc = pl.ds(pl.multiple_of(cur * BAND, BAND), BAND)
            rp = pl.ds(pl.multiple_of(prev * BAND, BAND), BAND)
            q = qbuf[g, rq, :]
            dn = (((1,), (1,)), ((), ()))
            s_c = lax.dot_general(q, kbuf[g, rc, :], dn, preferred_element_type=F32) * scale + bias_c
            s_p = lax.dot_general(q, kbuf[g, rp, :], dn, preferred_element_type=F32) * scale + bias_p
            s_p = jnp.where(has_prev, s_p, NEG)
            m = jnp.maximum(jnp.max(s_c, axis=-1, keepdims=True), jnp.max(s_p, axis=-1, keepdims=True))
            e_c = jnp.exp(s_c - m)
            e_p = jnp.exp(s_p - m)
            den = jnp.sum(e_c, axis=-1, keepdims=True) + jnp.sum(e_p, axis=-1, keepdims=True)
            acc = (jnp.dot(e_c.astype(BF16), vbuf[g, rc, :], preferred_element_type=F32)
                   + jnp.dot(e_p.astype(BF16), vbuf[g, rp, :], preferred_element_type=F32))
            ores[rq, :] = acc / den
            lres[rq, :] = jnp.broadcast_to(m + jnp.log(den), (BAND, A_HEAD_DIM))
            return carry

        lax.fori_loop(0, nblk_tile, unit, 0)

        for r in range(dil):
            rows = pl.ds(r, n, stride=dil) if dil > 1 else pl.ds(0, n)
            onat[g, rows, :] = ores[r * n:(r + 1) * n, :]
            lnat[g, rows, :] = lres[r * n:(r + 1) * n, :]

    l0, l1, l2 = lnat[0], lnat[1], lnat[2]
    mx = jnp.maximum(jnp.maximum(l0, l1), l2)
    w0, w1, w2 = jnp.exp(l0 - mx), jnp.exp(l1 - mx), jnp.exp(l2 - mx)
    mixed = (w0 * onat[0] + w1 * onat[1] + w2 * onat[2]) / (w0 + w1 + w2)
    o_ref[...] = (mixed * _silu(za_ref[...])).astype(o_ref.dtype)


def _attention(zh, slopes, batch, seq):
    nt = seq // ATT_TILE
    m = batch * seq

    def head_spec(which, g):
        c0 = which * N_GROUPS * A_HEADS + g * A_HEADS
        return pl.BlockSpec((None, ATT_TILE, A_HEAD_DIM), lambda b, h, t, s: (c0 + h, b * nt + t, 0))

    in_specs = [head_spec(w, g) for w in range(3) for g in range(N_GROUPS)]
    in_specs.append(pl.BlockSpec((None, ATT_TILE, A_HEAD_DIM),
                                 lambda b, h, t, s: (3 * N_GROUPS * A_HEADS + h, b * nt + t, 0)))
    grid_spec = pltpu.PrefetchScalarGridSpec(
        num_scalar_prefetch=1,
        grid=(batch, A_HEADS, nt),
        in_specs=in_specs,
        out_specs=pl.BlockSpec((ATT_TILE, A_HEAD_DIM), lambda b, h, t, s: (b * nt + t, h)),
        scratch_shapes=[
            pltpu.VMEM((N_GROUPS, ATT_TILE, A_HEAD_DIM), BF16),
            pltpu.VMEM((N_GROUPS, 2 * ATT_TILE, A_HEAD_DIM), BF16),
            pltpu.VMEM((N_GROUPS, 2 * ATT_TILE, A_HEAD_DIM), BF16),
            pltpu.VMEM((ATT_TILE, A_HEAD_DIM), F32),
            pltpu.VMEM((ATT_TILE, A_HEAD_DIM), F32),
            pltpu.VMEM((N_GROUPS, ATT_TILE, A_HEAD_DIM), F32),
            pltpu.VMEM((N_GROUPS, ATT_TILE, A_HEAD_DIM), F32),
        ],
    )
    return pl.pallas_call(
        _attn_kernel, grid_spec=grid_spec,
        out_shape=jax.ShapeDtypeStruct((m, A_WIDTH), BF16),
        compiler_params=_params(("parallel", "parallel", "arbitrary")),
        name="dilated_attn",
    )(slopes, *([zh] * 10))


def _retention_kernel(lg_ref, q_ref, k_ref, v_ref, zr_ref, gret_ref, y_ref, rout_ref, r_scr):
    h = pl.program_id(1)
    s = pl.program_id(2)
    log_g = lg_ref[h]

    @pl.when(s == 0)
    def _():
        r_scr[...] = jnp.zeros_like(r_scr)

    L = R_CHUNK
    pi = lax.broadcasted_iota(jnp.int32, (L, L), 0)
    pj = lax.broadcasted_iota(jnp.int32, (L, L), 1)
    rel = (pi - pj).astype(F32)
    intra = jnp.where(rel >= 0, jnp.exp(jnp.maximum(rel, 0.0) * log_g), 0.0)
    pos = lax.broadcasted_iota(jnp.int32, (L, 1), 0).astype(F32)
    q_decay = jnp.exp((pos + 1.0) * log_g)
    k_decay = jnp.exp((L - 1.0 - pos) * log_g)
    chunk_decay = jnp.exp(jnp.full((1, 1), float(L), F32) * log_g)
    gret = gret_ref[...]

    def chunk(c, carry):
        rows = pl.ds(pl.multiple_of(c * L, L), L)
        q = q_ref[rows, :]
        k = k_ref[rows, :] * (R_DIM ** -0.5)
        v = v_ref[rows, :]
        r_prev = r_scr[...]
        scores = lax.dot_general(q, k, (((1,), (1,)), ((), ())), preferred_element_type=F32) * intra
        inner = jnp.dot(scores.astype(BF16), v, preferred_element_type=F32)
        cross = jnp.dot(q, r_prev.astype(BF16), preferred_element_type=F32) * q_decay
        kd = (k.astype(F32) * k_decay).astype(BF16)
        r_scr[...] = r_prev * chunk_decay + lax.dot_general(
            kd, v, (((0,), (0,)), ((), ())), preferred_element_type=F32)
        y = inner + cross
        y = y * lax.rsqrt(jnp.mean(y * y, axis=-1, keepdims=True) + EPS)
        y = (y * gret) * _silu(zr_ref[rows, :].astype(F32))
        y_ref[rows, :] = y.astype(y_ref.dtype)
        return carry

    lax.fori_loop(0, q_ref.shape[0] // L, chunk, 0)

    @pl.when(s == pl.num_programs(2) - 1)
    def _():
        rout_ref[...] = r_scr[...]


def _retention(zr_all, log_g, g_ret, batch, seq, rb):
    ns = seq // rb
    m = batch * seq

    def col_spec(col):
        c0 = (col - HEAD_COLS) // R_DIM
        return pl.BlockSpec((rb, R_DIM), lambda b, h, s, lg: (b * ns + s, c0 + h))

    grid_spec = pltpu.PrefetchScalarGridSpec(
        num_scalar_prefetch=1,
        grid=(batch, R_HEADS, ns),
        in_specs=[col_spec(COL_QR), col_spec(COL_KR), col_spec(COL_VR), col_spec(COL_ZR),
                  pl.BlockSpec((1, R_DIM), lambda b, h, s, lg: (0, h))],
        out_specs=[pl.BlockSpec((rb, R_DIM), lambda b, h, s, lg: (b * ns + s, h)),
                   pl.BlockSpec((None, None, R_DIM, R_DIM), lambda b, h, s, lg: (b, h, 0, 0))],
        scratch_shapes=[pltpu.VMEM((R_DIM, R_DIM), F32)],
    )
    return pl.pallas_call(
        _retention_kernel, grid_spec=grid_spec,
        out_shape=[jax.ShapeDtypeStruct((m, R_WIDTH), BF16),
                   jax.ShapeDtypeStruct((batch, R_HEADS, R_DIM, R_DIM), F32)],
        compiler_params=_params(("parallel", "parallel", "arbitrary")),
        name="retention",
    )(log_g, zr_all, zr_all, zr_all, zr_all, g_ret.reshape(1, R_WIDTH))


def _out_kernel(ya_ref, yr_ref, ga_ref, gb_ref, x_ref, p_ref,
                wa_ref, wr_ref, wo_ref, gpost_ref, wg_ref, wp_ref, o_ref):
    a = jnp.dot(ya_ref[...].astype(BF16), wa_ref[...], preferred_element_type=F32)
    b = jnp.dot(yr_ref[...].astype(BF16), wr_ref[...], preferred_element_type=F32)
    merged = (jax.nn.sigmoid(ga_ref[...].astype(F32)) * a
              + jax.nn.sigmoid(gb_ref[...].astype(F32)) * b)
    y = jnp.dot(merged.astype(BF16), wo_ref[...], preferred_element_type=F32)
    y = y * lax.rsqrt(jnp.mean(y * y, axis=-1, keepdims=True) + EPS)
    hres = x_ref[...] + y * gpost_ref[...]
    gate = jax.nn.sigmoid(jnp.dot(hres.astype(BF16), wg_ref[...], preferred_element_type=F32))
    emb = jnp.dot(p_ref[...].astype(BF16), wp_ref[...], preferred_element_type=F32)
    o_ref[...] = hres + gate * emb


def _out_proj(ya, yr, gates, ga_blk, gb_blk, x, p, wa, wr, wo, g_post, wg, wp, tm):
    m = x.shape[0]

    def rows(width, cb=0):
        return pl.BlockSpec((tm, width), lambda i: (i, cb))

    def whole(arr):
        return pl.BlockSpec(arr.shape, lambda i: (0, 0), pipeline_mode=pl.Buffered(1))

    gp = g_post.reshape(1, D_MODEL)
    return pl.pallas_call(
        _out_kernel,
        grid=(m // tm,),
        in_specs=[rows(A_WIDTH), rows(R_WIDTH), rows(D_MODEL, ga_blk), rows(D_MODEL, gb_blk),
                  rows(D_MODEL), rows(PLE_DIM),
                  whole(wa), whole(wr), whole(wo), whole(gp), whole(wg), whole(wp)],
        out_specs=rows(D_MODEL),
        out_shape=jax.ShapeDtypeStruct((m, D_MODEL), F32),
        compiler_params=_params(("parallel",)),
        name="out_proj",
    )(ya, yr, gates, gates, x, p, wa, wr, wo, gp, wg, wp)


def _attn_step_kernel(slope_ref, z_ref, c0_ref, c1_ref, c2_ref, o_ref):
    c_refs = (c0_ref, c1_ref, c2_ref)
    scale = A_HEAD_DIM ** -0.5
    j = lax.broadcasted_iota(jnp.int32, (BAND, 1), 0).astype(F32)
    for h in range(A_HEADS):
        outs, lses = [], []
        for g in range(N_GROUPS):
            lo = g * A_WIDTH + h * A_HEAD_DIM
            q = z_ref[:, COL_QA + lo:COL_QA + lo + A_HEAD_DIM] * scale
            kn = z_ref[:, COL_KA + lo:COL_KA + lo + A_HEAD_DIM]
            vn = z_ref[:, COL_VA + lo:COL_VA + lo + A_HEAD_DIM]
            kc = c_refs[g][:, h * A_HEAD_DIM:(h + 1) * A_HEAD_DIM]
            vc = c_refs[g][:, A_WIDTH + h * A_HEAD_DIM:A_WIDTH + (h + 1) * A_HEAD_DIM]
            slope = slope_ref[g, h] * float(DILATIONS[g])
            s = jnp.sum(kc * q, axis=-1, keepdims=True) - slope * (float(BAND) - j)
            s_new = jnp.sum(kn * q, axis=-1, keepdims=True)
            m = jnp.maximum(jnp.max(s, axis=0, keepdims=True), s_new)
            e = jnp.exp(s - m)
            e_new = jnp.exp(s_new - m)
            den = jnp.sum(e, axis=0, keepdims=True) + e_new
            outs.append((jnp.sum(e * vc, axis=0, keepdims=True) + e_new * vn) / den)
            lses.append(m + jnp.log(den))
        mx = jnp.maximum(jnp.maximum(lses[0], lses[1]), lses[2])
        ws = [jnp.exp(l - mx) for l in lses]
        mixed = (ws[0] * outs[0] + ws[1] * outs[1] + ws[2] * outs[2]) / (ws[0] + ws[1] + ws[2])
        za = z_ref[:, COL_ZA + h * A_HEAD_DIM:COL_ZA + (h + 1) * A_HEAD_DIM]
        o_ref[:, h * A_HEAD_DIM:(h + 1) * A_HEAD_DIM] = mixed * _silu(za)


def _attention_step(z3, caches, slopes):
    b = z3.shape[0]
    row = 2 * A_WIDTH
    views = [c.reshape(b, BAND, DILATIONS[g] * row) for g, c in enumerate(caches)]
    grid_spec = pltpu.PrefetchScalarGridSpec(
        num_scalar_prefetch=1,
        grid=(b,),
        in_specs=[pl.BlockSpec((None, 1, N_IN), lambda i, s: (i, 0, 0))]
                 + [pl.BlockSpec((None, BAND, row), lambda i, s: (i, 0, 0)) for _ in views],
        out_specs=pl.BlockSpec((None, 1, A_WIDTH), lambda i, s: (i, 0, 0)),
    )
    return pl.pallas_call(
        _attn_step_kernel, grid_spec=grid_spec,
        out_shape=jax.ShapeDtypeStruct((b, 1, A_WIDTH), F32),
        compiler_params=_params(("parallel",)),
        name="dilated_attn_step",
    )(slopes, z3, *views)


def _retention_step_kernel(lg_ref, q_ref, k_ref, v_ref, zr_ref, gret_ref, r_ref, y_ref, rout_ref):
    h = pl.program_id(1)
    gamma = jnp.exp(jnp.full((1, 1), 1.0, F32) * lg_ref[h])
    q = q_ref[...]
    k = k_ref[...] * (R_DIM ** -0.5)
    v = v_ref[...]
    r_prev = r_ref[...]
    qb = jnp.broadcast_to(q, (8, R_DIM)).astype(BF16)
    cross = jnp.dot(qb, r_prev.astype(BF16), preferred_element_type=F32)[0:1] * gamma
    inner = jnp.sum(q * k, axis=-1, keepdims=True) * v
    k_col = jnp.broadcast_to(k, (R_DIM, R_DIM)).T
    rout_ref[...] = r_prev * gamma + k_col * v
    y = inner + cross
    y = y * lax.rsqrt(jnp.mean(y * y, axis=-1, keepdims=True) + EPS)
    y_ref[...] = (y * gret_ref[...]) * _silu(zr_ref[...])


def _retention_step(z3, state, log_g, g_ret):
    b = z3.shape[0]

    def col_spec(col):
        c0 = col // R_DIM
        return pl.BlockSpec((None, 1, R_DIM), lambda i, h, lg: (i, 0, c0 + h))

    state_spec = pl.BlockSpec((None, None, R_DIM, R_DIM), lambda i, h, lg: (i, h, 0, 0))
    grid_spec = pltpu.PrefetchScalarGridSpec(
        num_scalar_prefetch=1,
        grid=(b, R_HEADS),
        in_specs=[col_spec(COL_QR), col_spec(COL_KR), col_spec(COL_VR), col_spec(COL_ZR),
                  pl.BlockSpec((1, R_DIM), lambda i, h, lg: (0, h)), state_spec],
        out_specs=[pl.BlockSpec((None, 1, R_DIM), lambda i, h, lg: (i, 0, h)), state_spec],
    )
    return pl.pallas_call(
        _retention_step_kernel, grid_spec=grid_spec,
        out_shape=[jax.ShapeDtypeStruct((b, 1, R_WIDTH), F32),
                   jax.ShapeDtypeStruct(state.shape, F32)],
        compiler_params=_params(("parallel", "parallel")),
        name="retention_step",
    )(log_g, z3, z3, z3, z3, g_ret.reshape(1, R_WIDTH), state)


def _alibi_slopes():
    n = N_GROUPS * A_HEADS
    return jnp.exp2(-8.0 * (jnp.arange(n, dtype=F32) + 1.0) / n).reshape(N_GROUPS, A_HEADS)


def _retention_log_decay():
    return jnp.log1p(-jnp.exp2(-5.0 - jnp.arange(R_HEADS, dtype=F32)))


def kernel(x_prompt, x_sample, cache_win0, cache_win1, cache_win2, state_ret, p_prompt, p_sample,
           g_pre, w_in, g_ret, w_a_out, w_r_out, w_o, g_post, w_ple_gate, w_ple_proj):
    batch, seq, _ = x_prompt.shape
    dec_batch = x_sample.shape[0]
    assert g_pre.shape[0] == 1 and x_sample.shape[1] == 1
    assert seq % ATT_TILE == 0
    m = batch * seq
    slopes = _alibi_slopes()
    log_g = _retention_log_decay()

    w_in_b = w_in[0].astype(BF16)
    wa, wr, wo = w_a_out[0].astype(BF16), w_r_out[0].astype(BF16), w_o[0].astype(BF16)
    wg, wp = w_ple_gate[0].astype(BF16), w_ple_proj[0].astype(BF16)

    xp = x_prompt.reshape(m, D_MODEL)
    u = _rmsnorm(xp, g_pre[0], 512)
    zh = _in_proj(u, w_in_b, 0, HEAD_COLS, F32, 1024, 1024, head_major=True)
    zr = _in_proj(u, w_in_b, HEAD_COLS, REST_COLS, BF16, 1024, 1024, head_major=False)
    ya = _attention(zh, slopes, batch, seq)
    yr, ret_prompt = _retention(zr, log_g, g_ret[0], batch, seq, 1024)
    ga_blk = (COL_GA - HEAD_COLS) // D_MODEL
    y_prompt = _out_proj(ya, yr, zr, ga_blk, ga_blk + 1, xp, p_prompt[0].reshape(m, PLE_DIM),
                         wa, wr, wo, g_post[0], wg, wp, 256).reshape(batch, seq, D_MODEL)

    win_prompt = []
    nh = N_GROUPS * A_HEADS
    for g in range(N_GROUPS):
        keep = min(WINDOWS[g], seq)
        kv = []
        for which in (1, 2):
            c0 = which * nh + g * A_HEADS
            part = zh[c0:c0 + A_HEADS].reshape(A_HEADS, batch, seq, A_HEAD_DIM)[:, :, seq - keep:]
            kv.append(jnp.transpose(part, (1, 2, 0, 3)))
        win_prompt.append(jnp.stack(kv, axis=2)[None])

    xs = x_sample.reshape(dec_batch, D_MODEL)
    us = _rmsnorm(xs, g_pre[0], dec_batch)
    zs = _in_proj(us, w_in_b, 0, N_IN, F32, dec_batch, 1024, head_major=False)
    z3 = zs.reshape(dec_batch, 1, N_IN)
    caches = (cache_win0[0], cache_win1[0], cache_win2[0])
    ya_s = _attention_step(z3, caches, slopes).reshape(dec_batch, A_WIDTH)
    yr_s, ret_sample = _retention_step(z3, state_ret[0], log_g, g_ret[0])
    y_sample = _out_proj(ya_s, yr_s.reshape(dec_batch, R_WIDTH), zs, COL_GA // D_MODEL,
                         COL_GB // D_MODEL, xs, p_sample[0].reshape(dec_batch, PLE_DIM),
                         wa, wr, wo, g_post[0], wg, wp, dec_batch).reshape(dec_batch, 1, D_MODEL)

    win_sample = []
    for g in range(N_GROUPS):
        k_new = zs[:, COL_KA + g * A_WIDTH:COL_KA + (g + 1) * A_WIDTH]
        v_new = zs[:, COL_VA + g * A_WIDTH:COL_VA + (g + 1) * A_WIDTH]
        new = jnp.stack([k_new, v_new], axis=1).reshape(dec_batch, 1, 2, A_HEADS, A_HEAD_DIM)
        win_sample.append(jnp.concatenate([caches[g][:, 1:], new], axis=1)[None])

    return (y_prompt, y_sample, win_prompt[0], win_prompt[1], win_prompt[2], ret_prompt[None],
            win_sample[0], win_sample[1], win_sample[2], ret_sample[None])
```

```python
import functools

import jax
import jax.numpy as jnp
from jax import lax
from jax.experimental import pallas as pl
from jax.experimental.pallas import tpu as pltpu

F32 = jnp.float32
BF16 = jnp.bfloat16

D_MODEL = 2048
N_GROUPS = 3
DILATIONS = (1, 4, 16)
WINDOWS = (128, 512, 2048)
BAND = 128
A_HEADS = 8
A_HEAD_DIM = 128
A_QKV = N_GROUPS * A_HEADS * A_HEAD_DIM
A_WIDTH = A_HEADS * A_HEAD_DIM
R_HEADS = 8
R_DIM = 256
R_WIDTH = R_HEADS * R_DIM
R_CHUNK = 128
PLE_DIM = 256
EPS = 1e-6
N_IN = 3 * A_QKV + A_WIDTH + 4 * R_WIDTH + 2 * D_MODEL

COL_QA, COL_KA, COL_VA = 0, A_QKV, 2 * A_QKV
COL_ZA = 3 * A_QKV
COL_QR = COL_ZA + A_WIDTH
COL_KR = COL_QR + R_WIDTH
COL_VR = COL_KR + R_WIDTH
COL_ZR = COL_VR + R_WIDTH
COL_GA = COL_ZR + R_WIDTH
COL_GB = COL_GA + D_MODEL

HEAD_COLS = COL_QR
REST_COLS = N_IN - HEAD_COLS
CH_Q, CH_K, CH_V, CH_ZA = (c // A_HEAD_DIM for c in (COL_QA, COL_KA, COL_VA, COL_ZA))
N_HEAD_CHUNKS = HEAD_COLS // A_HEAD_DIM

NEG = -1e30
ATT_TILE = BAND * max(DILATIONS)
ATT_UNROLL = 2

VMEM_LIMIT = 56 * 1024 * 1024


def _params(semantics, vmem=VMEM_LIMIT):
    return pltpu.CompilerParams(dimension_semantics=semantics, vmem_limit_bytes=vmem)


def _silu(x):
    return x * jax.nn.sigmoid(x)


def _rmsnorm_kernel(x_ref, g_ref, o_ref):
    x = x_ref[...]
    y = x * lax.rsqrt(jnp.mean(x * x, axis=-1, keepdims=True) + EPS)
    o_ref[...] = (y * g_ref[...]).astype(o_ref.dtype)


def _rmsnorm(x, g, tm):
    m, d = x.shape
    return pl.pallas_call(
        _rmsnorm_kernel,
        grid=(m // tm,),
        in_specs=[pl.BlockSpec((tm, d), lambda i: (i, 0)),
                  pl.BlockSpec((1, d), lambda i: (0, 0))],
        out_specs=pl.BlockSpec((tm, d), lambda i: (i, 0)),
        out_shape=jax.ShapeDtypeStruct((m, d), BF16),
        compiler_params=_params(("parallel",)),
        name="rmsnorm",
    )(x, g.reshape(1, d))


def _in_proj_kernel(u_ref, us_ref, w_ref, o_ref, os_ref, wb_ref, *, head_major):
    @pl.when(pl.program_id(1) == 0)
    def _():
        wb_ref[...] = w_ref[...].astype(BF16)
        os_ref[...] = jnp.dot(us_ref[...], wb_ref[...], preferred_element_type=F32)

    acc = jnp.dot(u_ref[...], wb_ref[...], preferred_element_type=F32)
    if head_major:
        for c in range(o_ref.shape[0]):
            o_ref[c] = acc[:, c * A_HEAD_DIM:(c + 1) * A_HEAD_DIM].astype(o_ref.dtype)
    else:
        o_ref[...] = acc.astype(o_ref.dtype)


def _in_proj(u, us, w, col0, ncols, out_dtype, tm, tn, head_major):
    m, k = u.shape
    ms = us.shape[0]
    j0 = col0 // tn
    in_specs = [pl.BlockSpec((tm, k), lambda j, i: (i, 0)),
                pl.BlockSpec((ms, k), lambda j, i: (0, 0)),
                pl.BlockSpec((k, tn), lambda j, i: (0, j0 + j))]
    if head_major:
        o_spec = pl.BlockSpec((tn // A_HEAD_DIM, tm, A_HEAD_DIM), lambda j, i: (j, i, 0))
        o_shape = jax.ShapeDtypeStruct((ncols // A_HEAD_DIM, m, A_HEAD_DIM), out_dtype)
    else:
        o_spec = pl.BlockSpec((tm, tn), lambda j, i: (i, j))
        o_shape = jax.ShapeDtypeStruct((m, ncols), out_dtype)
    return pl.pallas_call(
        functools.partial(_in_proj_kernel, head_major=head_major),
        grid=(ncols // tn, m // tm),
        in_specs=in_specs,
        out_specs=[o_spec, pl.BlockSpec((ms, tn), lambda j, i: (0, j))],
        out_shape=[o_shape, jax.ShapeDtypeStruct((ms, ncols), F32)],
        scratch_shapes=[pltpu.VMEM((k, tn), BF16)],
        compiler_params=_params(("parallel", "arbitrary")),
        name="in_proj_hm" if head_major else "in_proj",
    )(u, us, w)


def _attn_kernel(slope_ref, q0, q1, q2, k0, k1, k2, v0, v1, v2, za_ref, o_ref,
                 qbuf, kbuf, vbuf, onat, lnat, *, seq):
    q_refs, k_refs, v_refs = (q0, q1, q2), (k0, k1, k2), (v0, v1, v2)
    h = pl.program_id(1)
    t = pl.program_id(2)
    nblk_tile = ATT_TILE // BAND

    @pl.when(t == 0)
    def _():
        kbuf[...] = jnp.zeros_like(kbuf)
        vbuf[...] = jnp.zeros_like(vbuf)

    qi = lax.broadcasted_iota(jnp.int32, (BAND, BAND), 0)
    kj = lax.broadcasted_iota(jnp.int32, (BAND, BAND), 1)
    own = kj < qi
    diag = kj == qi
    dist = jnp.where(own, qi - kj, qi - kj + BAND).astype(F32)
    scale = A_HEAD_DIM ** -0.5

    for g in range(N_GROUPS):
        dil = DILATIONS[g]
        n = ATT_TILE // dil
        nblk = n // BAND
        run = seq // dil + BAND
        for r in range(dil):
            rows = pl.ds(r, n, stride=dil) if dil > 1 else pl.ds(0, n)
            dst = pl.ds(pl.multiple_of(r * run + BAND + t * n, BAND), n)
            qbuf[g, r * n:(r + 1) * n, :] = q_refs[g][rows, :].astype(BF16)
            kbuf[g, dst, :] = k_refs[g][rows, :].astype(BF16)
            vbuf[g, dst, :] = v_refs[g][rows, :].astype(BF16)

        bias = -(slope_ref[g, h] * float(dil)) * dist

        def unit(u, carry, g=g, dil=dil, n=n, nblk=nblk, run=run, bias=bias):
            r = lax.shift_right_logical(u, nblk.bit_length() - 1)
            i = lax.bitwise_and(u, nblk - 1)
            kv_rows = pl.ds(pl.multiple_of(r * run + t * n + i * BAND, BAND), 2 * BAND)
            q = qbuf[g, pl.ds(pl.multiple_of(u * BAND, BAND), BAND), :]
            kk = kbuf[g, kv_rows, :]
            vv = vbuf[g, kv_rows, :]
            s = lax.dot_general(q, kk, (((1,), (1,)), ((), ())), preferred_element_type=F32)
            s_prev, s_own = s[:, :BAND], s[:, BAND:]
            sm = jnp.where(own, s_own, s_prev) * scale + bias
            no_prev = jnp.logical_and(i == 0, t == 0)
            sm = jnp.where(jnp.logical_and(no_prev, jnp.logical_not(own)), NEG, sm)
            s_self = jnp.sum(jnp.where(diag, s_own, 0.0), axis=-1, keepdims=True) * scale
            m = jnp.maximum(jnp.max(sm, axis=-1, keepdims=True), s_self)
            e = jnp.exp(sm - m)
            e_self = jnp.exp(s_self - m)
            den = jnp.sum(e, axis=-1, keepdims=True) + e_self
            p = jnp.concatenate([jnp.where(own, 0.0, e), jnp.where(own, e, 0.0)], axis=1)
            acc = jnp.dot(p.astype(BF16), vv, preferred_element_type=F32)
            acc = acc + e_self * vv[BAND:, :].astype(F32)
            out_rows = pl.ds(r + i * (BAND * dil), BAND, stride=dil) if dil > 1 else (
                pl.ds(pl.multiple_of(u * BAND, BAND), BAND))
            onat[g, out_rows, :] = acc / den
            lnat[g, out_rows, :] = jnp.broadcast_to(m + jnp.log(den), (BAND, A_HEAD_DIM))
            return carry

        lax.fori_loop(0, nblk_tile, unit, 0, unroll=ATT_UNROLL)

    l0, l1, l2 = lnat[0], lnat[1], lnat[2]
    mx = jnp.maximum(jnp.maximum(l0, l1), l2)
    w0, w1, w2 = jnp.exp(l0 - mx), jnp.exp(l1 - mx), jnp.exp(l2 - mx)
    mixed = (w0 * onat[0] + w1 * onat[1] + w2 * onat[2]) / (w0 + w1 + w2)
    o_ref[...] = (mixed * _silu(za_ref[...])).astype(o_ref.dtype)


def _attention(zh, slopes, batch, seq):
    nt = seq // ATT_TILE
    m = batch * seq

    def head_spec(c0):
        return pl.BlockSpec((None, ATT_TILE, A_HEAD_DIM), lambda b, h, t, s: (c0 + h, b * nt + t, 0))

    in_specs = [head_spec(c + g * A_HEADS) for c in (CH_Q, CH_K, CH_V) for g in range(N_GROUPS)]
    in_specs.append(head_spec(CH_ZA))
    buf_rows = seq + BAND * max(DILATIONS)
    grid_spec = pltpu.PrefetchScalarGridSpec(
        num_scalar_prefetch=1,
        grid=(batch, A_HEADS, nt),
        in_specs=in_specs,
        out_specs=pl.BlockSpec((ATT_TILE, A_HEAD_DIM), lambda b, h, t, s: (b * nt + t, h)),
        scratch_shapes=[
            pltpu.VMEM((N_GROUPS, ATT_TILE, A_HEAD_DIM), BF16),
            pltpu.VMEM((N_GROUPS, buf_rows, A_HEAD_DIM), BF16),
            pltpu.VMEM((N_GROUPS, buf_rows, A_HEAD_DIM), BF16),
            pltpu.VMEM((N_GROUPS, ATT_TILE, A_HEAD_DIM), F32),
            pltpu.VMEM((N_GROUPS, ATT_TILE, A_HEAD_DIM), F32),
        ],
    )
    return pl.pallas_call(
        functools.partial(_attn_kernel, seq=seq), grid_spec=grid_spec,
        out_shape=jax.ShapeDtypeStruct((m, A_WIDTH), BF16),
        compiler_params=_params(("parallel", "parallel", "arbitrary")),
        name="dilated_attn",
    )(slopes, *([zh] * 10))


def _retention_kernel(lg_ref, q_ref, k_ref, v_ref, zr_ref, gret_ref, y_ref, rout_ref, r_scr):
    h = pl.program_id(1)
    s = pl.program_id(2)
    log_g = lg_ref[h]

    @pl.when(s == 0)
    def _():
        r_scr[...] = jnp.zeros_like(r_scr)

    L = R_CHUNK
    pi = lax.broadcasted_iota(jnp.int32, (L, L), 0)
    pj = lax.broadcasted_iota(jnp.int32, (L, L), 1)
    rel = (pi - pj).astype(F32)
    intra = jnp.where(rel >= 0, jnp.exp(jnp.maximum(rel, 0.0) * log_g), 0.0)
    pos = lax.broadcasted_iota(jnp.int32, (L, 1), 0).astype(F32)
    q_decay = jnp.exp((pos + 1.0) * log_g)
    k_decay = jnp.exp((L - 1.0 - pos) * log_g)
    chunk_decay = jnp.exp(jnp.full((1, 1), float(L), F32) * log_g)
    gret = gret_ref[...]

    def chunk(c, carry):
        rows = pl.ds(pl.multiple_of(c * L, L), L)
        q = q_ref[rows, :]
        k = k_ref[rows, :] * (R_DIM ** -0.5)
        v = v_ref[rows, :]
        r_prev = r_scr[...]
        scores = lax.dot_general(q, k, (((1,), (1,)), ((), ())), preferred_element_type=F32) * intra
        inner = jnp.dot(scores.astype(BF16), v, preferred_element_type=F32)
        cross = jnp.dot(q, r_prev.astype(BF16), preferred_element_type=F32) * q_decay
        kd = (k.astype(F32) * k_decay).astype(BF16)
        r_scr[...] = r_prev * chunk_decay + lax.dot_general(
            kd, v, (((0,), (0,)), ((), ())), preferred_element_type=F32)
        y = inner + cross
        y = y * lax.rsqrt(jnp.mean(y * y, axis=-1, keepdims=True) + EPS)
        y = (y * gret) * _silu(zr_ref[rows, :].astype(F32))
        y_ref[rows, :] = y.astype(y_ref.dtype)
        return carry

    lax.fori_loop(0, q_ref.shape[0] // L, chunk, 0)

    @pl.when(s == pl.num_programs(2) - 1)
    def _():
        rout_ref[...] = r_scr[...]


def _retention(zr_all, log_g, g_ret, batch, seq, rb):
    ns = seq // rb
    m = batch * seq

    def col_spec(col):
        c0 = (col - HEAD_COLS) // R_DIM
        return pl.BlockSpec((rb, R_DIM), lambda b, h, s, lg: (b * ns + s, c0 + h))

    grid_spec = pltpu.PrefetchScalarGridSpec(
        num_scalar_prefetch=1,
        grid=(batch, R_HEADS, ns),
        in_specs=[col_spec(COL_QR), col_spec(COL_KR), col_spec(COL_VR), col_spec(COL_ZR),
                  pl.BlockSpec((1, R_DIM), lambda b, h, s, lg: (0, h))],
        out_specs=[pl.BlockSpec((rb, R_DIM), lambda b, h, s, lg: (b * ns + s, h)),
                   pl.BlockSpec((None, None, R_DIM, R_DIM), lambda b, h, s, lg: (b, h, 0, 0))],
        scratch_shapes=[pltpu.VMEM((R_DIM, R_DIM), F32)],
    )
    return pl.pallas_call(
        _retention_kernel, grid_spec=grid_spec,
        out_shape=[jax.ShapeDtypeStruct((m, R_WIDTH), BF16),
                   jax.ShapeDtypeStruct((batch, R_HEADS, R_DIM, R_DIM), F32)],
        compiler_params=_params(("parallel", "parallel", "arbitrary")),
        name="retention",
    )(log_g, zr_all, zr_all, zr_all, zr_all, g_ret.reshape(1, R_WIDTH))


def _out_kernel(ya_ref, yr_ref, ga_ref, gb_ref, x_ref, p_ref,
                wa_ref, wr_ref, wo_ref, gpost_ref, wg_ref, wp_ref, o_ref):
    a = jnp.dot(ya_ref[...].astype(BF16), wa_ref[...], preferred_element_type=F32)
    b = jnp.dot(yr_ref[...].astype(BF16), wr_ref[...], preferred_element_type=F32)
    merged = (jax.nn.sigmoid(ga_ref[...].astype(F32)) * a
              + jax.nn.sigmoid(gb_ref[...].astype(F32)) * b)
    y = jnp.dot(merged.astype(BF16), wo_ref[...], preferred_element_type=F32)
    y = y * lax.rsqrt(jnp.mean(y * y, axis=-1, keepdims=True) + EPS)
    hres = x_ref[...] + y * gpost_ref[...]
    gate = jax.nn.sigmoid(jnp.dot(hres.astype(BF16), wg_ref[...], preferred_element_type=F32))
    emb = jnp.dot(p_ref[...].astype(BF16), wp_ref[...], preferred_element_type=F32)
    o_ref[...] = hres + gate * emb


def _out_proj(ya, yr, gates, ga_blk, gb_blk, x, p, wa, wr, wo, g_post, wg, wp, tm):
    m = x.shape[0]

    def rows(width, cb=0):
        return pl.BlockSpec((tm, width), lambda i: (i, cb))

    def whole(arr):
        return pl.BlockSpec(arr.shape, lambda i: (0, 0), pipeline_mode=pl.Buffered(1))

    gp = g_post.reshape(1, D_MODEL)
    return pl.pallas_call(
        _out_kernel,
        grid=(m // tm,),
        in_specs=[rows(A_WIDTH), rows(R_WIDTH), rows(D_MODEL, ga_blk), rows(D_MODEL, gb_blk),
                  rows(D_MODEL), rows(PLE_DIM),
                  whole(wa), whole(wr), whole(wo), whole(gp), whole(wg), whole(wp)],
        out_specs=rows(D_MODEL),
        out_shape=jax.ShapeDtypeStruct((m, D_MODEL), F32),
        compiler_params=_params(("parallel",)),
        name="out_proj",
    )(ya, yr, gates, gates, x, p, wa, wr, wo, gp, wg, wp)


def _attn_step_kernel(slope_ref, z_ref, c0_ref, c1_ref, c2_ref, o_ref):
    c_refs = (c0_ref, c1_ref, c2_ref)
    scale = A_HEAD_DIM ** -0.5
    steps = float(BAND) - lax.broadcasted_iota(jnp.int32, (BAND, 1, 1), 0).astype(F32)
    outs, lses = [], []
    for g in range(N_GROUPS):
        q = z_ref[CH_Q + g * A_HEADS:CH_Q + (g + 1) * A_HEADS, :] * scale
        kn = z_ref[CH_K + g * A_HEADS:CH_K + (g + 1) * A_HEADS, :]
        vn = z_ref[CH_V + g * A_HEADS:CH_V + (g + 1) * A_HEADS, :]
        kc = c_refs[g][:, 0]
        vc = c_refs[g][:, 1]
        slope = slope_ref[g][None]
        s = jnp.sum(kc * q[None], axis=-1, keepdims=True) - (slope * float(DILATIONS[g])) * steps
        s_new = jnp.sum(kn * q, axis=-1, keepdims=True)
        m = jnp.maximum(jnp.max(s, axis=0), s_new)
        e = jnp.exp(s - m[None])
        e_new = jnp.exp(s_new - m)
        den = jnp.sum(e, axis=0) + e_new
        outs.append((jnp.sum(e * vc, axis=0) + e_new * vn) / den)
        lses.append(m + jnp.log(den))
    mx = jnp.maximum(jnp.maximum(lses[0], lses[1]), lses[2])
    ws = [jnp.exp(l - mx) for l in lses]
    mixed = (ws[0] * outs[0] + ws[1] * outs[1] + ws[2] * outs[2]) / (ws[0] + ws[1] + ws[2])
    o_ref[...] = mixed * _silu(z_ref[CH_ZA:CH_ZA + A_HEADS, :])


def _attention_step(zs_h, caches, slopes):
    b = zs_h.shape[0]
    z3 = zs_h.reshape(b, N_HEAD_CHUNKS, A_HEAD_DIM)
    views = [c.reshape(b, BAND, DILATIONS[g], 2, A_HEADS, A_HEAD_DIM) for g, c in enumerate(caches)]
    cache_spec = pl.BlockSpec((None, BAND, None, 2, A_HEADS, A_HEAD_DIM),
                              lambda i: (i, 0, 0, 0, 0, 0))
    out = pl.pallas_call(
        _attn_step_kernel,
        grid=(b,),
        in_specs=[pl.BlockSpec((N_GROUPS, A_HEADS, 1), lambda i: (0, 0, 0)),
                  pl.BlockSpec((None, N_HEAD_CHUNKS, A_HEAD_DIM), lambda i: (i, 0, 0)),
                  cache_spec, cache_spec, cache_spec],
        out_specs=pl.BlockSpec((None, A_HEADS, A_HEAD_DIM), lambda i: (i, 0, 0)),
        out_shape=jax.ShapeDtypeStruct((b, A_HEADS, A_HEAD_DIM), F32),
        compiler_params=_params(("parallel",)),
        name="dilated_attn_step",
    )(slopes.reshape(N_GROUPS, A_HEADS, 1), z3, *views)
    return out.reshape(b, A_WIDTH)


def _retention_step_kernel(lg_ref, q_ref, k_ref, v_ref, zr_ref, gret_ref, r_ref, y_ref, rout_ref):
    h = pl.program_id(1)
    gamma = jnp.exp(jnp.full((1, 1), 1.0, F32) * lg_ref[h])
    q = q_ref[...]
    k = k_ref[...] * (R_DIM ** -0.5)
    v = v_ref[...]
    r_prev = r_ref[...]
    qb = jnp.broadcast_to(q, (8, R_DIM)).astype(BF16)
    cross = jnp.dot(qb, r_prev.astype(BF16), preferred_element_type=F32)[0:1] * gamma
    inner = jnp.sum(q * k, axis=-1, keepdims=True) * v
    k_col = jnp.broadcast_to(k, (R_DIM, R_DIM)).T
    rout_ref[...] = r_prev * gamma + k_col * v
    y = inner + cross
    y = y * lax.rsqrt(jnp.mean(y * y, axis=-1, keepdims=True) + EPS)
    y_ref[...] = (y * gret_ref[...]) * _silu(zr_ref[...])


def _retention_step(zs_r, state, log_g, g_ret):
    b = zs_r.shape[0]
    z3 = zs_r.reshape(b, 1, REST_COLS)

    def col_spec(col):
        c0 = (col - HEAD_COLS) // R_DIM
        return pl.BlockSpec((None, 1, R_DIM), lambda i, h, lg: (i, 0, c0 + h))

    state_spec = pl.BlockSpec((None, None, R_DIM, R_DIM), lambda i, h, lg: (i, h, 0, 0))
    grid_spec = pltpu.PrefetchScalarGridSpec(
        num_scalar_prefetch=1,
        grid=(b, R_HEADS),
        in_specs=[col_spec(COL_QR), col_spec(COL_KR), col_spec(COL_VR), col_spec(COL_ZR),
                  pl.BlockSpec((1, R_DIM), lambda i, h, lg: (0, h)), state_spec],
        out_specs=[pl.BlockSpec((None, 1, R_DIM), lambda i, h, lg: (i, 0, h)), state_spec],
    )
    y, new_state = pl.pallas_call(
        _retention_step_kernel, grid_spec=grid_spec,
        out_shape=[jax.ShapeDtypeStruct((b, 1, R_WIDTH), F32),
                   jax.ShapeDtypeStruct(state.shape, F32)],
        compiler_params=_params(("parallel", "parallel")),
        name="retention_step",
    )(log_g, z3, z3, z3, z3, g_ret.reshape(1, R_WIDTH), state)
    return y.reshape(b, R_WIDTH), new_state


def _window_copies(zh_ref, zs_ref, c_refs, wp_refs, ws_refs, sems, batch, seq):
    copies = []
    for g in range(N_GROUPS):
        keep = wp_refs[g].shape[1]
        w = c_refs[g].shape[1]
        for kv, ch in enumerate((CH_K, CH_V)):
            c0 = ch + g * A_HEADS
            copies.append(pltpu.make_async_copy(
                zs_ref.at[:, pl.ds(c0, A_HEADS), :], ws_refs[g].at[:, w - 1, kv], sems.at[g]))
            for b in range(batch):
                for hh in range(A_HEADS):
                    copies.append(pltpu.make_async_copy(
                        zh_ref.at[c0 + hh, pl.ds(b * seq + seq - keep, keep), :],
                        wp_refs[g].at[b, :, kv, hh, :], sems.at[N_GROUPS + g]))
        copies.append(pltpu.make_async_copy(
            c_refs[g].at[:, pl.ds(1, w - 1)], ws_refs[g].at[:, pl.ds(0, w - 1)], sems.at[g]))
    return copies


def _window_kernel(zh_ref, zs_ref, c0, c1, c2, wp0, wp1, wp2, ws0, ws1, ws2, sems, *, batch, seq):
    copies = _window_copies(zh_ref, zs_ref, (c0, c1, c2), (wp0, wp1, wp2), (ws0, ws1, ws2),
                            sems, batch, seq)
    for c in copies:
        c.start()
    for c in copies:
        c.wait()


def _window_outputs(zh, zs_h, caches, batch, seq):
    b = zs_h.shape[0]
    zs3 = zs_h.reshape(b, N_HEAD_CHUNKS, A_HEAD_DIM)
    any_spec = pl.BlockSpec(memory_space=pl.ANY)
    out_shape = ([jax.ShapeDtypeStruct((batch, min(w, seq), 2, A_HEADS, A_HEAD_DIM), F32) for w in WINDOWS]
                 + [jax.ShapeDtypeStruct(c.shape, F32) for c in caches])
    return pl.pallas_call(
        functools.partial(_window_kernel, batch=batch, seq=seq),
        in_specs=[any_spec] * 5,
        out_specs=[any_spec] * 6,
        out_shape=out_shape,
        scratch_shapes=[pltpu.SemaphoreType.DMA((2 * N_GROUPS,))],
        name="window_outputs",
    )(zh, zs3, *caches)


def _alibi_slopes():
    n = N_GROUPS * A_HEADS
    return jnp.exp2(-8.0 * (jnp.arange(n, dtype=F32) + 1.0) / n).reshape(N_GROUPS, A_HEADS)


def _retention_log_decay():
    return jnp.log1p(-jnp.exp2(-5.0 - jnp.arange(R_HEADS, dtype=F32)))


def kernel(x_prompt, x_sample, cache_win0, cache_win1, cache_win2, state_ret, p_prompt, p_sample,
           g_pre, w_in, g_ret, w_a_out, w_r_out, w_o, g_post, w_ple_gate, w_ple_proj):
    batch, seq, _ = x_prompt.shape
    dec_batch = x_sample.shape[0]
    assert g_pre.shape[0] == 1 and x_sample.shape[1] == 1
    assert seq % ATT_TILE == 0
    m = batch * seq
    slopes = _alibi_slopes()
    log_g = _retention_log_decay()

    wa, wr, wo = w_a_out[0].astype(BF16), w_r_out[0].astype(BF16), w_o[0].astype(BF16)
    wg, wp = w_ple_gate[0].astype(BF16), w_ple_proj[0].astype(BF16)

    xp = x_prompt.reshape(m, D_MODEL)
    xs = x_sample.reshape(dec_batch, D_MODEL)
    u = _rmsnorm(xp, g_pre[0], 512)
    us = _rmsnorm(xs, g_pre[0], dec_batch)
    zh, zs_h = _in_proj(u, us, w_in[0], 0, HEAD_COLS, F32, 1024, 1024, head_major=True)
    zr, zs_r = _in_proj(u, us, w_in[0], HEAD_COLS, REST_COLS, BF16, 1024, 1024, head_major=False)
    ga_blk = (COL_GA - HEAD_COLS) // D_MODEL

    ya = _attention(zh, slopes, batch, seq)
    yr, ret_prompt = _retention(zr, log_g, g_ret[0], batch, seq, 1024)
    y_prompt = _out_proj(ya, yr, zr, ga_blk, ga_blk + 1, xp, p_prompt[0].reshape(m, PLE_DIM),
                         wa, wr, wo, g_post[0], wg, wp, 256).reshape(batch, seq, D_MODEL)

    caches = (cache_win0[0], cache_win1[0], cache_win2[0])
    ya_s = _attention_step(zs_h, caches, slopes)
    yr_s, ret_sample = _retention_step(zs_r, state_ret[0], log_g, g_ret[0])
    y_sample = _out_proj(ya_s, yr_s, zs_r, ga_blk, ga_blk + 1, xs, p_sample[0].reshape(dec_batch, PLE_DIM),
                         wa, wr, wo, g_post[0], wg, wp, dec_batch).reshape(dec_batch, 1, D_MODEL)

    wins = _window_outputs(zh, zs_h, caches, batch, seq)
    return (y_prompt, y_sample, wins[0][None], wins[1][None], wins[2][None], ret_prompt[None],
            wins[3][None], wins[4][None], wins[5][None], ret_sample[None])
```

```python
import functools

import jax
import jax.numpy as jnp
from jax import lax
from jax.experimental import pallas as pl
from jax.experimental.pallas import tpu as pltpu

F32 = jnp.float32
BF16 = jnp.bfloat16

D_MODEL = 2048
N_GROUPS = 3
DILATIONS = (1, 4, 16)
WINDOWS = (128, 512, 2048)
BAND = 128
A_HEADS = 8
A_HEAD_DIM = 128
A_QKV = N_GROUPS * A_HEADS * A_HEAD_DIM
A_WIDTH = A_HEADS * A_HEAD_DIM
R_HEADS = 8
R_DIM = 256
R_WIDTH = R_HEADS * R_DIM
R_CHUNK = 128
PLE_DIM = 256
EPS = 1e-6
N_IN = 3 * A_QKV + A_WIDTH + 4 * R_WIDTH + 2 * D_MODEL

COL_QA, COL_KA, COL_VA = 0, A_QKV, 2 * A_QKV
COL_ZA = 3 * A_QKV
COL_QR = COL_ZA + A_WIDTH
COL_KR = COL_QR + R_WIDTH
COL_VR = COL_KR + R_WIDTH
COL_ZR = COL_VR + R_WIDTH
COL_GA = COL_ZR + R_WIDTH
COL_GB = COL_GA + D_MODEL

HEAD_COLS = COL_QR
REST_COLS = N_IN - HEAD_COLS
CH_Q, CH_K, CH_V, CH_ZA = (c // A_HEAD_DIM for c in (COL_QA, COL_KA, COL_VA, COL_ZA))
N_HEAD_CHUNKS = HEAD_COLS // A_HEAD_DIM

NEG = -1e30
ATT_TILE = BAND * max(DILATIONS)
ATT_UNROLL = 16

VMEM_LIMIT = 56 * 1024 * 1024


def _params(semantics, vmem=VMEM_LIMIT):
    return pltpu.CompilerParams(dimension_semantics=semantics, vmem_limit_bytes=vmem)


def _silu(x):
    return x * jax.nn.sigmoid(x)


def _rmsnorm_kernel(x_ref, g_ref, o_ref):
    x = x_ref[...]
    y = x * lax.rsqrt(jnp.mean(x * x, axis=-1, keepdims=True) + EPS)
    o_ref[...] = (y * g_ref[...]).astype(o_ref.dtype)


def _rmsnorm(x, g, tm):
    m, d = x.shape
    return pl.pallas_call(
        _rmsnorm_kernel,
        grid=(m // tm,),
        in_specs=[pl.BlockSpec((tm, d), lambda i: (i, 0)),
                  pl.BlockSpec((1, d), lambda i: (0, 0))],
        out_specs=pl.BlockSpec((tm, d), lambda i: (i, 0)),
        out_shape=jax.ShapeDtypeStruct((m, d), BF16),
        compiler_params=_params(("parallel",)),
        name="rmsnorm",
    )(x, g.reshape(1, d))


def _in_proj_kernel(u_ref, us_ref, w_ref, o_ref, os_ref, wb_ref, *, head_major):
    @pl.when(pl.program_id(1) == 0)
    def _():
        wb_ref[...] = w_ref[...].astype(BF16)
        os_ref[...] = jnp.dot(us_ref[...], wb_ref[...], preferred_element_type=F32)

    acc = jnp.dot(u_ref[...], wb_ref[...], preferred_element_type=F32)
    if head_major:
        for c in range(o_ref.shape[0]):
            o_ref[c] = acc[:, c * A_HEAD_DIM:(c + 1) * A_HEAD_DIM].astype(o_ref.dtype)
    else:
        o_ref[...] = acc.astype(o_ref.dtype)


def _in_proj(u, us, w, col0, ncols, out_dtype, tm, tn, head_major):
    m, k = u.shape
    ms = us.shape[0]
    j0 = col0 // tn
    in_specs = [pl.BlockSpec((tm, k), lambda j, i: (i, 0)),
                pl.BlockSpec((ms, k), lambda j, i: (0, 0)),
                pl.BlockSpec((k, tn), lambda j, i: (0, j0 + j))]
    if head_major:
        o_spec = pl.BlockSpec((tn // A_HEAD_DIM, tm, A_HEAD_DIM), lambda j, i: (j, i, 0))
        o_shape = jax.ShapeDtypeStruct((ncols // A_HEAD_DIM, m, A_HEAD_DIM), out_dtype)
    else:
        o_spec = pl.BlockSpec((tm, tn), lambda j, i: (i, j))
        o_shape = jax.ShapeDtypeStruct((m, ncols), out_dtype)
    return pl.pallas_call(
        functools.partial(_in_proj_kernel, head_major=head_major),
        grid=(ncols // tn, m // tm),
        in_specs=in_specs,
        out_specs=[o_spec, pl.BlockSpec((ms, tn), lambda j, i: (0, j))],
        out_shape=[o_shape, jax.ShapeDtypeStruct((ms, ncols), F32)],
        scratch_shapes=[pltpu.VMEM((k, tn), BF16)],
        compiler_params=_params(("parallel", "arbitrary")),
        name="in_proj_hm" if head_major else "in_proj",
    )(u, us, w)


def _attn_kernel(slope_ref, q0, q1, q2, k0, k1, k2, v0, v1, v2, za_ref, o_ref,
                 qbuf, kbuf, vbuf, onat, lnat, *, seq):
    q_refs, k_refs, v_refs = (q0, q1, q2), (k0, k1, k2), (v0, v1, v2)
    h = pl.program_id(1)
    t = pl.program_id(2)
    nblk_tile = ATT_TILE // BAND

    @pl.when(t == 0)
    def _():
        kbuf[...] = jnp.zeros_like(kbuf)
        vbuf[...] = jnp.zeros_like(vbuf)

    qi = lax.broadcasted_iota(jnp.int32, (BAND, BAND), 0)
    kj = lax.broadcasted_iota(jnp.int32, (BAND, BAND), 1)
    own = kj < qi
    diag = kj == qi
    dist = jnp.where(own, qi - kj, qi - kj + BAND).astype(F32)
    scale = A_HEAD_DIM ** -0.5

    for g in range(N_GROUPS):
        dil = DILATIONS[g]
        n = ATT_TILE // dil
        nblk = n // BAND
        run = seq // dil + BAND
        for r in range(dil):
            rows = pl.ds(r, n, stride=dil) if dil > 1 else pl.ds(0, n)
            dst = pl.ds(pl.multiple_of(r * run + BAND + t * n, BAND), n)
            qbuf[g, r * n:(r + 1) * n, :] = q_refs[g][rows, :].astype(BF16)
            kbuf[g, dst, :] = k_refs[g][rows, :].astype(BF16)
            vbuf[g, dst, :] = v_refs[g][rows, :].astype(BF16)

        bias = -(slope_ref[g, h] * float(dil)) * dist

        def unit(u, carry, g=g, dil=dil, n=n, nblk=nblk, run=run, bias=bias):
            r = lax.shift_right_logical(u, nblk.bit_length() - 1)
            i = lax.bitwise_and(u, nblk - 1)
            kv_rows = pl.ds(pl.multiple_of(r * run + t * n + i * BAND, BAND), 2 * BAND)
            q = qbuf[g, pl.ds(pl.multiple_of(u * BAND, BAND), BAND), :]
            kk = kbuf[g, kv_rows, :]
            vv = vbuf[g, kv_rows, :]
            s = lax.dot_general(q, kk, (((1,), (1,)), ((), ())), preferred_element_type=F32)
            s_prev, s_own = s[:, :BAND], s[:, BAND:]
            sm = jnp.where(own, s_own, s_prev) * scale + bias
            no_prev = jnp.logical_and(i == 0, t == 0)
            sm = jnp.where(jnp.logical_and(no_prev, jnp.logical_not(own)), NEG, sm)
            s_self = jnp.sum(jnp.where(diag, s_own, 0.0), axis=-1, keepdims=True) * scale
            m = jnp.maximum(jnp.max(sm, axis=-1, keepdims=True), s_self)
            e = jnp.exp(sm - m)
            e_self = jnp.exp(s_self - m)
            den = jnp.sum(e, axis=-1, keepdims=True) + e_self
            p = jnp.concatenate([jnp.where(own, 0.0, e), jnp.where(own, e, 0.0)], axis=1)
            acc = jnp.dot(p.astype(BF16), vv, preferred_element_type=F32)
            acc = acc + e_self * vv[BAND:, :].astype(F32)
            out_rows = pl.ds(r + i * (BAND * dil), BAND, stride=dil) if dil > 1 else (
                pl.ds(pl.multiple_of(u * BAND, BAND), BAND))
            onat[g, out_rows, :] = acc / den
            lnat[g, out_rows, :] = jnp.broadcast_to(m + jnp.log(den), (BAND, A_HEAD_DIM))
            return carry

        lax.fori_loop(0, nblk_tile, unit, 0, unroll=ATT_UNROLL)

    l0, l1, l2 = lnat[0], lnat[1], lnat[2]
    mx = jnp.maximum(jnp.maximum(l0, l1), l2)
    w0, w1, w2 = jnp.exp(l0 - mx), jnp.exp(l1 - mx), jnp.exp(l2 - mx)
    mixed = (w0 * onat[0] + w1 * onat[1] + w2 * onat[2]) / (w0 + w1 + w2)
    o_ref[...] = (mixed * _silu(za_ref[...])).astype(o_ref.dtype)


def _attention(zh, slopes, batch, seq):
    nt = seq // ATT_TILE
    m = batch * seq

    def head_spec(c0):
        return pl.BlockSpec((None, ATT_TILE, A_HEAD_DIM), lambda b, h, t, s: (c0 + h, b * nt + t, 0))

    in_specs = [head_spec(c + g * A_HEADS) for c in (CH_Q, CH_K, CH_V) for g in range(N_GROUPS)]
    in_specs.append(head_spec(CH_ZA))
    buf_rows = seq + BAND * max(DILATIONS)
    grid_spec = pltpu.PrefetchScalarGridSpec(
        num_scalar_prefetch=1,
        grid=(batch, A_HEADS, nt),
        in_specs=in_specs,
        out_specs=pl.BlockSpec((ATT_TILE, A_HEAD_DIM), lambda b, h, t, s: (b * nt + t, h)),
        scratch_shapes=[
            pltpu.VMEM((N_GROUPS, ATT_TILE, A_HEAD_DIM), BF16),
            pltpu.VMEM((N_GROUPS, buf_rows, A_HEAD_DIM), BF16),
            pltpu.VMEM((N_GROUPS, buf_rows, A_HEAD_DIM), BF16),
            pltpu.VMEM((N_GROUPS, ATT_TILE, A_HEAD_DIM), F32),
            pltpu.VMEM((N_GROUPS, ATT_TILE, A_HEAD_DIM), F32),
        ],
    )
    return pl.pallas_call(
        functools.partial(_attn_kernel, seq=seq), grid_spec=grid_spec,
        out_shape=jax.ShapeDtypeStruct((m, A_WIDTH), BF16),
        compiler_params=_params(("parallel", "parallel", "arbitrary")),
        name="dilated_attn",
    )(slopes, *([zh] * 10))


def _retention_kernel(lg_ref, q_ref, k_ref, v_ref, zr_ref, gret_ref, y_ref, rout_ref, r_scr):
    h = pl.program_id(1)
    s = pl.program_id(2)
    log_g = lg_ref[h]

    @pl.when(s == 0)
    def _():
        r_scr[...] = jnp.zeros_like(r_scr)

    L = R_CHUNK
    pi = lax.broadcasted_iota(jnp.int32, (L, L), 0)
    pj = lax.broadcasted_iota(jnp.int32, (L, L), 1)
    rel = (pi - pj).astype(F32)
    intra = jnp.where(rel >= 0, jnp.exp(jnp.maximum(rel, 0.0) * log_g), 0.0)
    pos = lax.broadcasted_iota(jnp.int32, (L, 1), 0).astype(F32)
    q_decay = jnp.exp((pos + 1.0) * log_g)
    k_decay = jnp.exp((L - 1.0 - pos) * log_g)
    chunk_decay = jnp.exp(jnp.full((1, 1), float(L), F32) * log_g)
    gret = gret_ref[...]

    def chunk(c, carry):
        rows = pl.ds(pl.multiple_of(c * L, L), L)
        q = q_ref[rows, :]
        k = k_ref[rows, :] * (R_DIM ** -0.5)
        v = v_ref[rows, :]
        r_prev = r_scr[...]
        scores = lax.dot_general(q, k, (((1,), (1,)), ((), ())), preferred_element_type=F32) * intra
        inner = jnp.dot(scores.astype(BF16), v, preferred_element_type=F32)
        cross = jnp.dot(q, r_prev.astype(BF16), preferred_element_type=F32) * q_decay
        kd = (k.astype(F32) * k_decay).astype(BF16)
        r_scr[...] = r_prev * chunk_decay + lax.dot_general(
            kd, v, (((0,), (0,)), ((), ())), preferred_element_type=F32)
        y = inner + cross
        y = y * lax.rsqrt(jnp.mean(y * y, axis=-1, keepdims=True) + EPS)
        y = (y * gret) * _silu(zr_ref[rows, :].astype(F32))
        y_ref[rows, :] = y.astype(y_ref.dtype)
        return carry

    lax.fori_loop(0, q_ref.shape[0] // L, chunk, 0)

    @pl.when(s == pl.num_programs(2) - 1)
    def _():
        rout_ref[...] = r_scr[...]


def _retention(zr_all, log_g, g_ret, batch, seq, rb):
    ns = seq // rb
    m = batch * seq

    def col_spec(col):
        c0 = (col - HEAD_COLS) // R_DIM
        return pl.BlockSpec((rb, R_DIM), lambda b, h, s, lg: (b * ns + s, c0 + h))

    grid_spec = pltpu.PrefetchScalarGridSpec(
        num_scalar_prefetch=1,
        grid=(batch, R_HEADS, ns),
        in_specs=[col_spec(COL_QR), col_spec(COL_KR), col_spec(COL_VR), col_spec(COL_ZR),
                  pl.BlockSpec((1, R_DIM), lambda b, h, s, lg: (0, h))],
        out_specs=[pl.BlockSpec((rb, R_DIM), lambda b, h, s, lg: (b * ns + s, h)),
                   pl.BlockSpec((None, None, R_DIM, R_DIM), lambda b, h, s, lg: (b, h, 0, 0))],
        scratch_shapes=[pltpu.VMEM((R_DIM, R_DIM), F32)],
    )
    return pl.pallas_call(
        _retention_kernel, grid_spec=grid_spec,
        out_shape=[jax.ShapeDtypeStruct((m, R_WIDTH), BF16),
                   jax.ShapeDtypeStruct((batch, R_HEADS, R_DIM, R_DIM), F32)],
        compiler_params=_params(("parallel", "parallel", "arbitrary")),
        name="retention",
    )(log_g, zr_all, zr_all, zr_all, zr_all, g_ret.reshape(1, R_WIDTH))


def _out_kernel(ya_ref, yr_ref, ga_ref, gb_ref, x_ref, p_ref,
                wa_ref, wr_ref, wo_ref, gpost_ref, wg_ref, wp_ref, o_ref):
    a = jnp.dot(ya_ref[...].astype(BF16), wa_ref[...], preferred_element_type=F32)
    b = jnp.dot(yr_ref[...].astype(BF16), wr_ref[...], preferred_element_type=F32)
    merged = (jax.nn.sigmoid(ga_ref[...].astype(F32)) * a
              + jax.nn.sigmoid(gb_ref[...].astype(F32)) * b)
    y = jnp.dot(merged.astype(BF16), wo_ref[...], preferred_element_type=F32)
    y = y * lax.rsqrt(jnp.mean(y * y, axis=-1, keepdims=True) + EPS)
    hres = x_ref[...] + y * gpost_ref[...]
    gate = jax.nn.sigmoid(jnp.dot(hres.astype(BF16), wg_ref[...], preferred_element_type=F32))
    emb = jnp.dot(p_ref[...].astype(BF16), wp_ref[...], preferred_element_type=F32)
    o_ref[...] = hres + gate * emb


def _out_proj(ya, yr, gates, ga_blk, gb_blk, x, p, wa, wr, wo, g_post, wg, wp, tm):
    m = x.shape[0]

    def rows(width, cb=0):
        return pl.BlockSpec((tm, width), lambda i: (i, cb))

    def whole(arr):
        return pl.BlockSpec(arr.shape, lambda i: (0, 0), pipeline_mode=pl.Buffered(1))

    gp = g_post.reshape(1, D_MODEL)
    return pl.pallas_call(
        _out_kernel,
        grid=(m // tm,),
        in_specs=[rows(A_WIDTH), rows(R_WIDTH), rows(D_MODEL, ga_blk), rows(D_MODEL, gb_blk),
                  rows(D_MODEL), rows(PLE_DIM),
                  whole(wa), whole(wr), whole(wo), whole(gp), whole(wg), whole(wp)],
        out_specs=rows(D_MODEL),
        out_shape=jax.ShapeDtypeStruct((m, D_MODEL), F32),
        compiler_params=_params(("parallel",)),
        name="out_proj",
    )(ya, yr, gates, gates, x, p, wa, wr, wo, gp, wg, wp)


def _attn_step_kernel(slope_ref, z_ref, c0_ref, c1_ref, c2_ref, o_ref):
    c_refs = (c0_ref, c1_ref, c2_ref)
    scale = A_HEAD_DIM ** -0.5
    steps = float(BAND) - lax.broadcasted_iota(jnp.int32, (BAND, 1, 1), 0).astype(F32)
    outs, lses = [], []
    for g in range(N_GROUPS):
        q = z_ref[CH_Q + g * A_HEADS:CH_Q + (g + 1) * A_HEADS, :] * scale
        kn = z_ref[CH_K + g * A_HEADS:CH_K + (g + 1) * A_HEADS, :]
        vn = z_ref[CH_V + g * A_HEADS:CH_V + (g + 1) * A_HEADS, :]
        kc = c_refs[g][:, 0]
        vc = c_refs[g][:, 1]
        slope = slope_ref[g][None]
        s = jnp.sum(kc * q[None], axis=-1, keepdims=True) - (slope * float(DILATIONS[g])) * steps
        s_new = jnp.sum(kn * q, axis=-1, keepdims=True)
        m = jnp.maximum(jnp.max(s, axis=0), s_new)
        e = jnp.exp(s - m[None])
        e_new = jnp.exp(s_new - m)
        den = jnp.sum(e, axis=0) + e_new
        outs.append((jnp.sum(e * vc, axis=0) + e_new * vn) / den)
        lses.append(m + jnp.log(den))
    mx = jnp.maximum(jnp.maximum(lses[0], lses[1]), lses[2])
    ws = [jnp.exp(l - mx) for l in lses]
    mixed = (ws[0] * outs[0] + ws[1] * outs[1] + ws[2] * outs[2]) / (ws[0] + ws[1] + ws[2])
    o_ref[...] = mixed * _silu(z_ref[CH_ZA:CH_ZA + A_HEADS, :])


def _attention_step(zs_h, caches, slopes):
    b = zs_h.shape[0]
    z3 = zs_h.reshape(b, N_HEAD_CHUNKS, A_HEAD_DIM)
    views = [c.reshape(b, BAND, DILATIONS[g], 2, A_HEADS, A_HEAD_DIM) for g, c in enumerate(caches)]
    cache_spec = pl.BlockSpec((None, BAND, None, 2, A_HEADS, A_HEAD_DIM),
                              lambda i: (i, 0, 0, 0, 0, 0))
    out = pl.pallas_call(
        _attn_step_kernel,
        grid=(b,),
        in_specs=[pl.BlockSpec((N_GROUPS, A_HEADS, 1), lambda i: (0, 0, 0)),
                  pl.BlockSpec((None, N_HEAD_CHUNKS, A_HEAD_DIM), lambda i: (i, 0, 0)),
                  cache_spec, cache_spec, cache_spec],
        out_specs=pl.BlockSpec((None, A_HEADS, A_HEAD_DIM), lambda i: (i, 0, 0)),
        out_shape=jax.ShapeDtypeStruct((b, A_HEADS, A_HEAD_DIM), F32),
        compiler_params=_params(("parallel",)),
        name="dilated_attn_step",
    )(slopes.reshape(N_GROUPS, A_HEADS, 1), z3, *views)
    return out.reshape(b, A_WIDTH)


def _retention_step_kernel(lg_ref, q_ref, k_ref, v_ref, zr_ref, gret_ref, r_ref, y_ref, rout_ref):
    h = pl.program_id(1)
    gamma = jnp.exp(jnp.full((1, 1), 1.0, F32) * lg_ref[h])
    q = q_ref[...]
    k = k_ref[...] * (R_DIM ** -0.5)
    v = v_ref[...]
    r_prev = r_ref[...]
    qb = jnp.broadcast_to(q, (8, R_DIM)).astype(BF16)
    cross = jnp.dot(qb, r_prev.astype(BF16), preferred_element_type=F32)[0:1] * gamma
    inner = jnp.sum(q * k, axis=-1, keepdims=True) * v
    k_col = jnp.broadcast_to(k, (R_DIM, R_DIM)).T
    rout_ref[...] = r_prev * gamma + k_col * v
    y = inner + cross
    y = y * lax.rsqrt(jnp.mean(y * y, axis=-1, keepdims=True) + EPS)
    y_ref[...] = (y * gret_ref[...]) * _silu(zr_ref[...])


def _retention_step(zs_r, state, log_g, g_ret):
    b = zs_r.shape[0]
    z3 = zs_r.reshape(b, 1, REST_COLS)

    def col_spec(col):
        c0 = (col - HEAD_COLS) // R_DIM
        return pl.BlockSpec((None, 1, R_DIM), lambda i, h, lg: (i, 0, c0 + h))

    state_spec = pl.BlockSpec((None, None, R_DIM, R_DIM), lambda i, h, lg: (i, h, 0, 0))
    grid_spec = pltpu.PrefetchScalarGridSpec(
        num_scalar_prefetch=1,
        grid=(b, R_HEADS),
        in_specs=[col_spec(COL_QR), col_spec(COL_KR), col_spec(COL_VR), col_spec(COL_ZR),
                  pl.BlockSpec((1, R_DIM), lambda i, h, lg: (0, h)), state_spec],
        out_specs=[pl.BlockSpec((None, 1, R_DIM), lambda i, h, lg: (i, 0, h)), state_spec],
    )
    y, new_state = pl.pallas_call(
        _retention_step_kernel, grid_spec=grid_spec,
        out_shape=[jax.ShapeDtypeStruct((b, 1, R_WIDTH), F32),
                   jax.ShapeDtypeStruct(state.shape, F32)],
        compiler_params=_params(("parallel", "parallel")),
        name="retention_step",
    )(log_g, z3, z3, z3, z3, g_ret.reshape(1, R_WIDTH), state)
    return y.reshape(b, R_WIDTH), new_state


def _window_step_copies(zs_ref, c_refs, ws_refs, sems):
    copies = []
    for g in range(N_GROUPS):
        nb, w = c_refs[g].shape[:2]
        for kv, ch in enumerate((CH_K, CH_V)):
            copies.append(pltpu.make_async_copy(
                zs_ref.at[:, pl.ds(ch + g * A_HEADS, A_HEADS), :], ws_refs[g].at[:, w - 1, kv],
                sems.at[g]))
        for b in range(nb):
            copies.append(pltpu.make_async_copy(
                c_refs[g].at[b, pl.ds(1, w - 1)], ws_refs[g].at[b, pl.ds(0, w - 1)], sems.at[g]))
    return copies


def _window_step_kernel(zs_ref, c0, c1, c2, ws0, ws1, ws2, sems):
    copies = _window_step_copies(zs_ref, (c0, c1, c2), (ws0, ws1, ws2), sems)
    for c in copies:
        c.start()
    for c in copies:
        c.wait()


def _window_step_outputs(zs_h, caches):
    b = zs_h.shape[0]
    any_spec = pl.BlockSpec(memory_space=pl.ANY)
    return pl.pallas_call(
        _window_step_kernel,
        in_specs=[any_spec] * 4,
        out_specs=[any_spec] * 3,
        out_shape=[jax.ShapeDtypeStruct(c.shape, F32) for c in caches],
        scratch_shapes=[pltpu.SemaphoreType.DMA((N_GROUPS,))],
        name="window_step",
    )(zs_h.reshape(b, N_HEAD_CHUNKS, A_HEAD_DIM), *caches)


def _window_prompt_kernel(i_ref, o_ref):
    for hh in range(A_HEADS):
        o_ref[:, hh, :] = i_ref[hh]


def _window_prompt(zh, g, batch, seq, rb):
    keep = min(WINDOWS[g], seq)
    rb = min(rb, keep)
    row0 = (seq - keep) // rb
    out = pl.pallas_call(
        _window_prompt_kernel,
        grid=(batch, 2, keep // rb),
        in_specs=[pl.BlockSpec((A_HEADS, rb, A_HEAD_DIM),
                               lambda b, kv, s: ((CH_K + kv * (CH_V - CH_K)) // A_HEADS + g,
                                                 b * (seq // rb) + row0 + s, 0))],
        out_specs=pl.BlockSpec((rb, A_HEADS, A_HEAD_DIM), lambda b, kv, s: (b * (keep // rb) + s, kv, 0)),
        out_shape=jax.ShapeDtypeStruct((batch * keep, 2 * A_HEADS, A_HEAD_DIM), F32),
        compiler_params=_params(("parallel", "parallel", "parallel")),
        name="window_prompt",
    )(zh)
    return out.reshape(batch, keep, 2, A_HEADS, A_HEAD_DIM)


def _alibi_slopes():
    n = N_GROUPS * A_HEADS
    return jnp.exp2(-8.0 * (jnp.arange(n, dtype=F32) + 1.0) / n).reshape(N_GROUPS, A_HEADS)


def _retention_log_decay():
    return jnp.log1p(-jnp.exp2(-5.0 - jnp.arange(R_HEADS, dtype=F32)))


def kernel(x_prompt, x_sample, cache_win0, cache_win1, cache_win2, state_ret, p_prompt, p_sample,
           g_pre, w_in, g_ret, w_a_out, w_r_out, w_o, g_post, w_ple_gate, w_ple_proj):
    batch, seq, _ = x_prompt.shape
    dec_batch = x_sample.shape[0]
    assert g_pre.shape[0] == 1 and x_sample.shape[1] == 1
    assert seq % ATT_TILE == 0
    m = batch * seq
    slopes = _alibi_slopes()
    log_g = _retention_log_decay()

    wa, wr, wo = w_a_out[0].astype(BF16), w_r_out[0].astype(BF16), w_o[0].astype(BF16)
    wg, wp = w_ple_gate[0].astype(BF16), w_ple_proj[0].astype(BF16)

    xp = x_prompt.reshape(m, D_MODEL)
    xs = x_sample.reshape(dec_batch, D_MODEL)
    u = _rmsnorm(xp, g_pre[0], 512)
    us = _rmsnorm(xs, g_pre[0], dec_batch)
    zh, zs_h = _in_proj(u, us, w_in[0], 0, HEAD_COLS, F32, 1024, 1024, head_major=True)
    zr, zs_r = _in_proj(u, us, w_in[0], HEAD_COLS, REST_COLS, BF16, 1024, 1024, head_major=False)
    ga_blk = (COL_GA - HEAD_COLS) // D_MODEL

    ya = _attention(zh, slopes, batch, seq)
    yr, ret_prompt = _retention(zr, log_g, g_ret[0], batch, seq, 1024)
    y_prompt = _out_proj(ya, yr, zr, ga_blk, ga_blk + 1, xp, p_prompt[0].reshape(m, PLE_DIM),
                         wa, wr, wo, g_post[0], wg, wp, 256).reshape(batch, seq, D_MODEL)

    caches = (cache_win0[0], cache_win1[0], cache_win2[0])
    ya_s = _attention_step(zs_h, caches, slopes)
    yr_s, ret_sample = _retention_step(zs_r, state_ret[0], log_g, g_ret[0])
    y_sample = _out_proj(ya_s, yr_s, zs_r, ga_blk, ga_blk + 1, xs, p_sample[0].reshape(dec_batch, PLE_DIM),
                         wa, wr, wo, g_post[0], wg, wp, dec_batch).reshape(dec_batch, 1, D_MODEL)

    win_p = [_window_prompt(zh, g, batch, seq, 512)[None] for g in range(N_GROUPS)]
    win_s = _window_step_outputs(zs_h, caches)
    return (y_prompt, y_sample, win_p[0], win_p[1], win_p[2], ret_prompt[None],
            win_s[0][None], win_s[1][None], win_s[2][None], ret_sample[None])
```

```python
import functools

import jax
import jax.numpy as jnp
from jax import lax
from jax.experimental import pallas as pl
from jax.experimental.pallas import tpu as pltpu

F32 = jnp.float32
BF16 = jnp.bfloat16

D_MODEL = 2048
N_GROUPS = 3
DILATIONS = (1, 4, 16)
WINDOWS = (128, 512, 2048)
BAND = 128
A_HEADS = 8
A_HEAD_DIM = 128
A_QKV = N_GROUPS * A_HEADS * A_HEAD_DIM
A_WIDTH = A_HEADS * A_HEAD_DIM
R_HEADS = 8
R_DIM = 256
R_WIDTH = R_HEADS * R_DIM
R_CHUNK = 128
PLE_DIM = 256
EPS = 1e-6
N_IN = 3 * A_QKV + A_WIDTH + 4 * R_WIDTH + 2 * D_MODEL

COL_QA, COL_KA, COL_VA = 0, A_QKV, 2 * A_QKV
COL_ZA = 3 * A_QKV
COL_QR = COL_ZA + A_WIDTH
COL_KR = COL_QR + R_WIDTH
COL_VR = COL_KR + R_WIDTH
COL_ZR = COL_VR + R_WIDTH
COL_GA = COL_ZR + R_WIDTH
COL_GB = COL_GA + D_MODEL

HEAD_COLS = COL_QR
REST_COLS = N_IN - HEAD_COLS
CH_Q, CH_K, CH_V, CH_ZA = (c // A_HEAD_DIM for c in (COL_QA, COL_KA, COL_VA, COL_ZA))
N_HEAD_CHUNKS = HEAD_COLS // A_HEAD_DIM

NEG = -1e30
ATT_TILE = BAND * max(DILATIONS)

VMEM_LIMIT = 56 * 1024 * 1024


def _params(semantics, vmem=VMEM_LIMIT):
    return pltpu.CompilerParams(dimension_semantics=semantics, vmem_limit_bytes=vmem)


def _silu(x):
    return x * jax.nn.sigmoid(x)


def _rmsnorm_kernel(x_ref, g_ref, o_ref):
    x = x_ref[...]
    y = x * lax.rsqrt(jnp.mean(x * x, axis=-1, keepdims=True) + EPS)
    o_ref[...] = (y * g_ref[...]).astype(o_ref.dtype)


def _rmsnorm(x, g, tm):
    m, d = x.shape
    return pl.pallas_call(
        _rmsnorm_kernel,
        grid=(m // tm,),
        in_specs=[pl.BlockSpec((tm, d), lambda i: (i, 0)),
                  pl.BlockSpec((1, d), lambda i: (0, 0))],
        out_specs=pl.BlockSpec((tm, d), lambda i: (i, 0)),
        out_shape=jax.ShapeDtypeStruct((m, d), BF16),
        compiler_params=_params(("parallel",)),
        name="rmsnorm",
    )(x, g.reshape(1, d))


def _in_proj_kernel(u_ref, us_ref, w_ref, o_ref, os_ref, wb_ref, *, head_major):
    @pl.when(pl.program_id(1) == 0)
    def _():
        wb_ref[...] = w_ref[...].astype(BF16)
        os_ref[...] = jnp.dot(us_ref[...], wb_ref[...], preferred_element_type=F32)

    acc = jnp.dot(u_ref[...], wb_ref[...], preferred_element_type=F32)
    if head_major:
        for c in range(o_ref.shape[0]):
            o_ref[c] = acc[:, c * A_HEAD_DIM:(c + 1) * A_HEAD_DIM].astype(o_ref.dtype)
    else:
        o_ref[...] = acc.astype(o_ref.dtype)


def _in_proj(u, us, w, col0, ncols, out_dtype, tm, tn, head_major):
    m, k = u.shape
    ms = us.shape[0]
    j0 = col0 // tn
    in_specs = [pl.BlockSpec((tm, k), lambda j, i: (i, 0)),
                pl.BlockSpec((ms, k), lambda j, i: (0, 0)),
                pl.BlockSpec((k, tn), lambda j, i: (0, j0 + j))]
    if head_major:
        o_spec = pl.BlockSpec((tn // A_HEAD_DIM, tm, A_HEAD_DIM), lambda j, i: (j, i, 0))
        o_shape = jax.ShapeDtypeStruct((ncols // A_HEAD_DIM, m, A_HEAD_DIM), out_dtype)
    else:
        o_spec = pl.BlockSpec((tm, tn), lambda j, i: (i, j))
        o_shape = jax.ShapeDtypeStruct((m, ncols), out_dtype)
    return pl.pallas_call(
        functools.partial(_in_proj_kernel, head_major=head_major),
        grid=(ncols // tn, m // tm),
        in_specs=in_specs,
        out_specs=[o_spec, pl.BlockSpec((ms, tn), lambda j, i: (0, j))],
        out_shape=[o_shape, jax.ShapeDtypeStruct((ms, ncols), F32)],
        scratch_shapes=[pltpu.VMEM((k, tn), BF16)],
        compiler_params=_params(("parallel", "arbitrary")),
        name="in_proj_hm" if head_major else "in_proj",
    )(u, us, w)


def _attn_kernel(slope_ref, q0, q1, q2, k0, k1, k2, v0, v1, v2, za_ref, o_ref,
                 qbuf, kbuf, vbuf, onat, lnat, *, seq):
    q_refs, k_refs, v_refs = (q0, q1, q2), (k0, k1, k2), (v0, v1, v2)
    h = pl.program_id(1)
    t = pl.program_id(2)

    @pl.when(t == 0)
    def _():
        kbuf[...] = jnp.zeros_like(kbuf)
        vbuf[...] = jnp.zeros_like(vbuf)

    qi = lax.broadcasted_iota(jnp.int32, (BAND, 2 * BAND), 0)
    kj = lax.broadcasted_iota(jnp.int32, (BAND, 2 * BAND), 1)
    dist = qi + BAND - kj
    valid = jnp.logical_and(dist >= 0, dist <= BAND)
    distf = dist.astype(F32)
    first_pen = jnp.where(jnp.logical_and(t == 0, kj < BAND), NEG, 0.0)
    scale = A_HEAD_DIM ** -0.5

    for g in range(N_GROUPS):
        dil = DILATIONS[g]
        n = ATT_TILE // dil
        nblk = n // BAND
        run = seq // dil + BAND
        for r in range(dil):
            rows = pl.ds(r, n, stride=dil) if dil > 1 else pl.ds(0, n)
            dst = pl.ds(pl.multiple_of(r * run + BAND + t * n, BAND), n)
            qbuf[g, r * n:(r + 1) * n, :] = (q_refs[g][rows, :] * scale).astype(BF16)
            kbuf[g, dst, :] = k_refs[g][rows, :].astype(BF16)
            vbuf[g, dst, :] = v_refs[g][rows, :].astype(BF16)

        bias = jnp.where(valid, -(slope_ref[g, h] * float(dil)) * distf, NEG)
        bias_first = bias + first_pen

        for r in range(dil):
            for i in range(nblk):
                u = r * nblk + i
                kv_rows = pl.ds(pl.multiple_of(r * run + i * BAND + t * n, BAND), 2 * BAND)
                q = qbuf[g, u * BAND:(u + 1) * BAND, :]
                s = lax.dot_general(q, kbuf[g, kv_rows, :], (((1,), (1,)), ((), ())),
                                    preferred_element_type=F32)
                s = s + (bias_first if i == 0 else bias)
                m = jnp.max(s, axis=-1, keepdims=True)
                e = jnp.exp(s - m)
                den = jnp.sum(e, axis=-1, keepdims=True)
                acc = jnp.dot(e.astype(BF16), vbuf[g, kv_rows, :], preferred_element_type=F32)
                out_rows = (pl.ds(r + i * (BAND * dil), BAND, stride=dil) if dil > 1
                            else pl.ds(u * BAND, BAND))
                onat[g, out_rows, :] = acc / den
                lnat[g, out_rows, :] = jnp.broadcast_to(m + jnp.log(den), (BAND, A_HEAD_DIM))

    l0, l1, l2 = lnat[0], lnat[1], lnat[2]
    mx = jnp.maximum(jnp.maximum(l0, l1), l2)
    w0, w1, w2 = jnp.exp(l0 - mx), jnp.exp(l1 - mx), jnp.exp(l2 - mx)
    mixed = (w0 * onat[0] + w1 * onat[1] + w2 * onat[2]) / (w0 + w1 + w2)
    o_ref[...] = (mixed * _silu(za_ref[...])).astype(o_ref.dtype)


def _attention(zh, slopes, batch, seq):
    nt = seq // ATT_TILE
    m = batch * seq

    def head_spec(c0):
        return pl.BlockSpec((None, ATT_TILE, A_HEAD_DIM), lambda b, h, t, s: (c0 + h, b * nt + t, 0))

    in_specs = [head_spec(c + g * A_HEADS) for c in (CH_Q, CH_K, CH_V) for g in range(N_GROUPS)]
    in_specs.append(head_spec(CH_ZA))
    buf_rows = seq + BAND * max(DILATIONS)
    grid_spec = pltpu.PrefetchScalarGridSpec(
        num_scalar_prefetch=1,
        grid=(batch, A_HEADS, nt),
        in_specs=in_specs,
        out_specs=pl.BlockSpec((ATT_TILE, A_HEAD_DIM), lambda b, h, t, s: (b * nt + t, h)),
        scratch_shapes=[
            pltpu.VMEM((N_GROUPS, ATT_TILE, A_HEAD_DIM), BF16),
            pltpu.VMEM((N_GROUPS, buf_rows, A_HEAD_DIM), BF16),
            pltpu.VMEM((N_GROUPS, buf_rows, A_HEAD_DIM), BF16),
            pltpu.VMEM((N_GROUPS, ATT_TILE, A_HEAD_DIM), F32),
            pltpu.VMEM((N_GROUPS, ATT_TILE, A_HEAD_DIM), F32),
        ],
    )
    return pl.pallas_call(
        functools.partial(_attn_kernel, seq=seq), grid_spec=grid_spec,
        out_shape=jax.ShapeDtypeStruct((m, A_WIDTH), BF16),
        compiler_params=_params(("parallel", "parallel", "arbitrary")),
        name="dilated_attn",
    )(slopes, *([zh] * 10))


def _retention_kernel(lg_ref, q_ref, k_ref, v_ref, zr_ref, gret_ref, y_ref, rout_ref, r_scr):
    h = pl.program_id(1)
    s = pl.program_id(2)
    log_g = lg_ref[h]

    @pl.when(s == 0)
    def _():
        r_scr[...] = jnp.zeros_like(r_scr)

    L = R_CHUNK
    pi = lax.broadcasted_iota(jnp.int32, (L, L), 0)
    pj = lax.broadcasted_iota(jnp.int32, (L, L), 1)
    rel = (pi - pj).astype(F32)
    intra = jnp.where(rel >= 0, jnp.exp(jnp.maximum(rel, 0.0) * log_g), 0.0)
    pos = lax.broadcasted_iota(jnp.int32, (L, 1), 0).astype(F32)
    q_decay = jnp.exp((pos + 1.0) * log_g)
    k_decay = jnp.exp((L - 1.0 - pos) * log_g)
    chunk_decay = jnp.exp(jnp.full((1, 1), float(L), F32) * log_g)
    gret = gret_ref[...]

    def chunk(c, carry):
        rows = pl.ds(pl.multiple_of(c * L, L), L)
        q = q_ref[rows, :]
        k = k_ref[rows, :] * (R_DIM ** -0.5)
        v = v_ref[rows, :]
        r_prev = r_scr[...]
        scores = lax.dot_general(q, k, (((1,), (1,)), ((), ())), preferred_element_type=F32) * intra
        inner = jnp.dot(scores.astype(BF16), v, preferred_element_type=F32)
        cross = jnp.dot(q, r_prev.astype(BF16), preferred_element_type=F32) * q_decay
        kd = (k.astype(F32) * k_decay).astype(BF16)
        r_scr[...] = r_prev * chunk_decay + lax.dot_general(
            kd, v, (((0,), (0,)), ((), ())), preferred_element_type=F32)
        y = inner + cross
        y = y * lax.rsqrt(jnp.mean(y * y, axis=-1, keepdims=True) + EPS)
        y = (y * gret) * _silu(zr_ref[rows, :].astype(F32))
        y_ref[rows, :] = y.astype(y_ref.dtype)
        return carry

    lax.fori_loop(0, q_ref.shape[0] // L, chunk, 0, unroll=True)

    @pl.when(s == pl.num_programs(2) - 1)
    def _():
        rout_ref[...] = r_scr[...]


def _retention(zr_all, log_g, g_ret, batch, seq, rb):
    ns = seq // rb
    m = batch * seq

    def col_spec(col):
        c0 = (col - HEAD_COLS) // R_DIM
        return pl.BlockSpec((rb, R_DIM), lambda b, h, s, lg: (b * ns + s, c0 + h))

    grid_spec = pltpu.PrefetchScalarGridSpec(
        num_scalar_prefetch=1,
        grid=(batch, R_HEADS, ns),
        in_specs=[col_spec(COL_QR), col_spec(COL_KR), col_spec(COL_VR), col_spec(COL_ZR),
                  pl.BlockSpec((1, R_DIM), lambda b, h, s, lg: (0, h))],
        out_specs=[pl.BlockSpec((rb, R_DIM), lambda b, h, s, lg: (b * ns + s, h)),
                   pl.BlockSpec((None, None, R_DIM, R_DIM), lambda b, h, s, lg: (b, h, 0, 0))],
        scratch_shapes=[pltpu.VMEM((R_DIM, R_DIM), F32)],
    )
    return pl.pallas_call(
        _retention_kernel, grid_spec=grid_spec,
        out_shape=[jax.ShapeDtypeStruct((m, R_WIDTH), BF16),
                   jax.ShapeDtypeStruct((batch, R_HEADS, R_DIM, R_DIM), F32)],
        compiler_params=_params(("parallel", "parallel", "arbitrary")),
        name="retention",
    )(log_g, zr_all, zr_all, zr_all, zr_all, g_ret.reshape(1, R_WIDTH))


def _out_kernel(ya_ref, yr_ref, ga_ref, gb_ref, x_ref, p_ref,
                wa_ref, wr_ref, wo_ref, gpost_ref, wg_ref, wp_ref, o_ref):
    a = jnp.dot(ya_ref[...].astype(BF16), wa_ref[...], preferred_element_type=F32)
    b = jnp.dot(yr_ref[...].astype(BF16), wr_ref[...], preferred_element_type=F32)
    merged = (jax.nn.sigmoid(ga_ref[...].astype(F32)) * a
              + jax.nn.sigmoid(gb_ref[...].astype(F32)) * b)
    y = jnp.dot(merged.astype(BF16), wo_ref[...], preferred_element_type=F32)
    y = y * lax.rsqrt(jnp.mean(y * y, axis=-1, keepdims=True) + EPS)
    hres = x_ref[...] + y * gpost_ref[...]
    gate = jax.nn.sigmoid(jnp.dot(hres.astype(BF16), wg_ref[...], preferred_element_type=F32))
    emb = jnp.dot(p_ref[...].astype(BF16), wp_ref[...], preferred_element_type=F32)
    o_ref[...] = hres + gate * emb


def _out_proj(ya, yr, gates, ga_blk, gb_blk, x, p, wa, wr, wo, g_post, wg, wp, tm):
    m = x.shape[0]

    def rows(width, cb=0):
        return pl.BlockSpec((tm, width), lambda i: (i, cb))

    def whole(arr):
        return pl.BlockSpec(arr.shape, lambda i: (0, 0), pipeline_mode=pl.Buffered(1))

    gp = g_post.reshape(1, D_MODEL)
    return pl.pallas_call(
        _out_kernel,
        grid=(m // tm,),
        in_specs=[rows(A_WIDTH), rows(R_WIDTH), rows(D_MODEL, ga_blk), rows(D_MODEL, gb_blk),
                  rows(D_MODEL), rows(PLE_DIM),
                  whole(wa), whole(wr), whole(wo), whole(gp), whole(wg), whole(wp)],
        out_specs=rows(D_MODEL),
        out_shape=jax.ShapeDtypeStruct((m, D_MODEL), F32),
        compiler_params=_params(("parallel",)),
        name="out_proj",
    )(ya, yr, gates, gates, x, p, wa, wr, wo, gp, wg, wp)


def _attn_step_kernel(slope_ref, z_ref, c0_ref, c1_ref, c2_ref, o_ref):
    c_refs = (c0_ref, c1_ref, c2_ref)
    scale = A_HEAD_DIM ** -0.5
    steps = float(BAND) - lax.broadcasted_iota(jnp.int32, (BAND, 1, 1), 0).astype(F32)
    outs, lses = [], []
    for g in range(N_GROUPS):
        q = z_ref[CH_Q + g * A_HEADS:CH_Q + (g + 1) * A_HEADS, :] * scale
        kn = z_ref[CH_K + g * A_HEADS:CH_K + (g + 1) * A_HEADS, :]
        vn = z_ref[CH_V + g * A_HEADS:CH_V + (g + 1) * A_HEADS, :]
        kc = c_refs[g][:, 0]
        vc = c_refs[g][:, 1]
        slope = slope_ref[g][None]
        s = jnp.sum(kc * q[None], axis=-1, keepdims=True) - (slope * float(DILATIONS[g])) * steps
        s_new = jnp.sum(kn * q, axis=-1, keepdims=True)
        m = jnp.maximum(jnp.max(s, axis=0), s_new)
        e = jnp.exp(s - m[None])
        e_new = jnp.exp(s_new - m)
        den = jnp.sum(e, axis=0) + e_new
        outs.append((jnp.sum(e * vc, axis=0) + e_new * vn) / den)
        lses.append(m + jnp.log(den))
    mx = jnp.maximum(jnp.maximum(lses[0], lses[1]), lses[2])
    ws = [jnp.exp(l - mx) for l in lses]
    mixed = (ws[0] * outs[0] + ws[1] * outs[1] + ws[2] * outs[2]) / (ws[0] + ws[1] + ws[2])
    o_ref[...] = mixed * _silu(z_ref[CH_ZA:CH_ZA + A_HEADS, :])


def _attention_step(zs_h, caches, slopes):
    b = zs_h.shape[0]
    z3 = zs_h.reshape(b, N_HEAD_CHUNKS, A_HEAD_DIM)
    views = [c.reshape(b, BAND, DILATIONS[g], 2, A_HEADS, A_HEAD_DIM) for g, c in enumerate(caches)]
    cache_spec = pl.BlockSpec((None, BAND, None, 2, A_HEADS, A_HEAD_DIM),
                              lambda i: (i, 0, 0, 0, 0, 0))
    out = pl.pallas_call(
        _attn_step_kernel,
        grid=(b,),
        in_specs=[pl.BlockSpec((N_GROUPS, A_HEADS, 1), lambda i: (0, 0, 0)),
                  pl.BlockSpec((None, N_HEAD_CHUNKS, A_HEAD_DIM), lambda i: (i, 0, 0)),
                  cache_spec, cache_spec, cache_spec],
        out_specs=pl.BlockSpec((None, A_HEADS, A_HEAD_DIM), lambda i: (i, 0, 0)),
        out_shape=jax.ShapeDtypeStruct((b, A_HEADS, A_HEAD_DIM), F32),
        compiler_params=_params(("parallel",)),
        name="dilated_attn_step",
    )(slopes.reshape(N_GROUPS, A_HEADS, 1), z3, *views)
    return out.reshape(b, A_WIDTH)


def _retention_step_kernel(lg_ref, q_ref, k_ref, v_ref, zr_ref, gret_ref, r_ref, y_ref, rout_ref):
    h = pl.program_id(1)
    gamma = jnp.exp(jnp.full((1, 1), 1.0, F32) * lg_ref[h])
    q = q_ref[...]
    k = k_ref[...] * (R_DIM ** -0.5)
    v = v_ref[...]
    r_prev = r_ref[...]
    qb = jnp.broadcast_to(q, (8, R_DIM)).astype(BF16)
    cross = jnp.dot(qb, r_prev.astype(BF16), preferred_element_type=F32)[0:1] * gamma
    inner = jnp.sum(q * k, axis=-1, keepdims=True) * v
    k_col = jnp.broadcast_to(k, (R_DIM, R_DIM)).T
    rout_ref[...] = r_prev * gamma + k_col * v
    y = inner + cross
    y = y * lax.rsqrt(jnp.mean(y * y, axis=-1, keepdims=True) + EPS)
    y_ref[...] = (y * gret_ref[...]) * _silu(zr_ref[...])


def _retention_step(zs_r, state, log_g, g_ret):
    b = zs_r.shape[0]
    z3 = zs_r.reshape(b, 1, REST_COLS)

    def col_spec(col):
        c0 = (col - HEAD_COLS) // R_DIM
        return pl.BlockSpec((None, 1, R_DIM), lambda i, h, lg: (i, 0, c0 + h))

    state_spec = pl.BlockSpec((None, None, R_DIM, R_DIM), lambda i, h, lg: (i, h, 0, 0))
    grid_spec = pltpu.PrefetchScalarGridSpec(
        num_scalar_prefetch=1,
        grid=(b, R_HEADS),
        in_specs=[col_spec(COL_QR), col_spec(COL_KR), col_spec(COL_VR), col_spec(COL_ZR),
                  pl.BlockSpec((1, R_DIM), lambda i, h, lg: (0, h)), state_spec],
        out_specs=[pl.BlockSpec((None, 1, R_DIM), lambda i, h, lg: (i, 0, h)), state_spec],
    )
    y, new_state = pl.pallas_call(
        _retention_step_kernel, grid_spec=grid_spec,
        out_shape=[jax.ShapeDtypeStruct((b, 1, R_WIDTH), F32),
                   jax.ShapeDtypeStruct(state.shape, F32)],
        compiler_params=_params(("parallel", "parallel")),
        name="retention_step",
    )(log_g, z3, z3, z3, z3, g_ret.reshape(1, R_WIDTH), state)
    return y.reshape(b, R_WIDTH), new_state


def _window_step_kernel(cur_ref, nxt_ref, z_ref, o_ref, *, g):
    rb = o_ref.shape[0]
    o_ref[0:rb - 1] = cur_ref[1:rb]
    last = pl.program_id(1) == pl.num_programs(1) - 1

    @pl.when(last)
    def _():
        o_ref[rb - 1, 0] = z_ref[CH_K + g * A_HEADS:CH_K + (g + 1) * A_HEADS, :]
        o_ref[rb - 1, 1] = z_ref[CH_V + g * A_HEADS:CH_V + (g + 1) * A_HEADS, :]

    @pl.when(jnp.logical_not(last))
    def _():
        o_ref[rb - 1] = nxt_ref[0]


def _window_step(zs_h, cache, g, rb):
    b, w = cache.shape[:2]
    rb = min(rb, w)
    tile = (2, A_HEADS, A_HEAD_DIM)
    return pl.pallas_call(
        functools.partial(_window_step_kernel, g=g),
        grid=(b, w // rb),
        in_specs=[pl.BlockSpec((None, rb) + tile, lambda i, s: (i, s, 0, 0, 0)),
                  pl.BlockSpec((None, 1) + tile, lambda i, s: (i, jnp.minimum((s + 1) * rb, w - 1), 0, 0, 0)),
                  pl.BlockSpec((None, N_HEAD_CHUNKS, A_HEAD_DIM), lambda i, s: (i, 0, 0))],
        out_specs=pl.BlockSpec((None, rb) + tile, lambda i, s: (i, s, 0, 0, 0)),
        out_shape=jax.ShapeDtypeStruct(cache.shape, F32),
        compiler_params=_params(("parallel", "parallel")),
        name="window_step",
    )(cache, cache, zs_h.reshape(b, N_HEAD_CHUNKS, A_HEAD_DIM))


def _window_prompt_kernel(i_ref, o_ref):
    for hh in range(A_HEADS):
        o_ref[:, hh, :] = i_ref[hh]


def _window_prompt(zh, g, batch, seq, rb):
    keep = min(WINDOWS[g], seq)
    rb = min(rb, keep)
    row0 = (seq - keep) // rb
    out = pl.pallas_call(
        _window_prompt_kernel,
        grid=(batch, 2, keep // rb),
        in_specs=[pl.BlockSpec((A_HEADS, rb, A_HEAD_DIM),
                               lambda b, kv, s: ((CH_K + kv * (CH_V - CH_K)) // A_HEADS + g,
                                                 b * (seq // rb) + row0 + s, 0))],
        out_specs=pl.BlockSpec((rb, A_HEADS, A_HEAD_DIM), lambda b, kv, s: (b * (keep // rb) + s, kv, 0)),
        out_shape=jax.ShapeDtypeStruct((batch * keep, 2 * A_HEADS, A_HEAD_DIM), F32),
        compiler_params=_params(("parallel", "parallel", "parallel")),
        name="window_prompt",
    )(zh)
    return out.reshape(batch, keep, 2, A_HEADS, A_HEAD_DIM)


def _alibi_slopes():
    n = N_GROUPS * A_HEADS
    return jnp.exp2(-8.0 * (jnp.arange(n, dtype=F32) + 1.0) / n).reshape(N_GROUPS, A_HEADS)


def _retention_log_decay():
    return jnp.log1p(-jnp.exp2(-5.0 - jnp.arange(R_HEADS, dtype=F32)))


def kernel(x_prompt, x_sample, cache_win0, cache_win1, cache_win2, state_ret, p_prompt, p_sample,
           g_pre, w_in, g_ret, w_a_out, w_r_out, w_o, g_post, w_ple_gate, w_ple_proj):
    batch, seq, _ = x_prompt.shape
    dec_batch = x_sample.shape[0]
    assert g_pre.shape[0] == 1 and x_sample.shape[1] == 1
    assert seq % ATT_TILE == 0
    m = batch * seq
    slopes = _alibi_slopes()
    log_g = _retention_log_decay()

    wa, wr, wo = w_a_out[0].astype(BF16), w_r_out[0].astype(BF16), w_o[0].astype(BF16)
    wg, wp = w_ple_gate[0].astype(BF16), w_ple_proj[0].astype(BF16)

    xp = x_prompt.reshape(m, D_MODEL)
    xs = x_sample.reshape(dec_batch, D_MODEL)
    u = _rmsnorm(xp, g_pre[0], 512)
    us = _rmsnorm(xs, g_pre[0], dec_batch)
    zh, zs_h = _in_proj(u, us, w_in[0], 0, HEAD_COLS, F32, 1024, 1024, head_major=True)
    zr, zs_r = _in_proj(u, us, w_in[0], HEAD_COLS, REST_COLS, BF16, 1024, 1024, head_major=False)
    ga_blk = (COL_GA - HEAD_COLS) // D_MODEL

    ya = _attention(zh, slopes, batch, seq)
    yr, ret_prompt = _retention(zr, log_g, g_ret[0], batch, seq, 1024)
    y_prompt = _out_proj(ya, yr, zr, ga_blk, ga_blk + 1, xp, p_prompt[0].reshape(m, PLE_DIM),
                         wa, wr, wo, g_post[0], wg, wp, 256).reshape(batch, seq, D_MODEL)

    caches = (cache_win0[0], cache_win1[0], cache_win2[0])
    ya_s = _attention_step(zs_h, caches, slopes)
    yr_s, ret_sample = _retention_step(zs_r, state_ret[0], log_g, g_ret[0])
    y_sample = _out_proj(ya_s, yr_s, zs_r, ga_blk, ga_blk + 1, xs, p_sample[0].reshape(dec_batch, PLE_DIM),
                         wa, wr, wo, g_post[0], wg, wp, dec_batch).reshape(dec_batch, 1, D_MODEL)

    win_p = [_window_prompt(zh, g, batch, seq, 512)[None] for g in range(N_GROUPS)]
    win_s = [_window_step(zs_h, caches[g], g, 512) for g in range(N_GROUPS)]
    return (y_prompt, y_sample, win_p[0], win_p[1], win_p[2], ret_prompt[None],
            win_s[0][None], win_s[1][None], win_s[2][None], ret_sample[None])
```

```python
import functools

import jax
import jax.numpy as jnp
from jax import lax
from jax.experimental import pallas as pl
from jax.experimental.pallas import tpu as pltpu

F32 = jnp.float32
BF16 = jnp.bfloat16

D_MODEL = 2048
N_GROUPS = 3
DILATIONS = (1, 4, 16)
WINDOWS = (128, 512, 2048)
BAND = 128
A_HEADS = 8
A_HEAD_DIM = 128
A_QKV = N_GROUPS * A_HEADS * A_HEAD_DIM
A_WIDTH = A_HEADS * A_HEAD_DIM
R_HEADS = 8
R_DIM = 256
R_WIDTH = R_HEADS * R_DIM
R_CHUNK = 128
PLE_DIM = 256
EPS = 1e-6
N_IN = 3 * A_QKV + A_WIDTH + 4 * R_WIDTH + 2 * D_MODEL

COL_QA, COL_KA, COL_VA = 0, A_QKV, 2 * A_QKV
COL_ZA = 3 * A_QKV
COL_QR = COL_ZA + A_WIDTH
COL_KR = COL_QR + R_WIDTH
COL_VR = COL_KR + R_WIDTH
COL_ZR = COL_VR + R_WIDTH
COL_GA = COL_ZR + R_WIDTH
COL_GB = COL_GA + D_MODEL

HEAD_COLS = COL_QR
REST_COLS = N_IN - HEAD_COLS
CH_Q, CH_K, CH_V, CH_ZA = (c // A_HEAD_DIM for c in (COL_QA, COL_KA, COL_VA, COL_ZA))
N_HEAD_CHUNKS = HEAD_COLS // A_HEAD_DIM

NEG = -1e30
ATT_TILE = BAND * max(DILATIONS)

VMEM_LIMIT = 56 * 1024 * 1024


def _params(semantics, vmem=VMEM_LIMIT):
    return pltpu.CompilerParams(dimension_semantics=semantics, vmem_limit_bytes=vmem)


def _silu(x):
    return x * jax.nn.sigmoid(x)


def _rmsnorm_kernel(x_ref, g_ref, o_ref):
    x = x_ref[...]
    y = x * lax.rsqrt(jnp.mean(x * x, axis=-1, keepdims=True) + EPS)
    o_ref[...] = (y * g_ref[...]).astype(o_ref.dtype)


def _rmsnorm(x, g, tm):
    m, d = x.shape
    return pl.pallas_call(
        _rmsnorm_kernel,
        grid=(m // tm,),
        in_specs=[pl.BlockSpec((tm, d), lambda i: (i, 0)),
                  pl.BlockSpec((1, d), lambda i: (0, 0))],
        out_specs=pl.BlockSpec((tm, d), lambda i: (i, 0)),
        out_shape=jax.ShapeDtypeStruct((m, d), BF16),
        compiler_params=_params(("parallel",)),
        name="rmsnorm",
    )(x, g.reshape(1, d))


def _in_proj_kernel(*refs, head_major, shift_steps):
    ns = len(shift_steps)
    u_ref, us_ref, w_ref = refs[:3]
    shift_in = refs[3:3 + 2 * ns]
    o_ref, os_ref = refs[3 + 2 * ns:5 + 2 * ns]
    shift_out = refs[5 + 2 * ns:5 + 3 * ns]
    wb_ref = refs[5 + 3 * ns]

    @pl.when(pl.program_id(1) == 0)
    def _():
        wb_ref[...] = w_ref[...].astype(BF16)
        os_ref[...] = jnp.dot(us_ref[...], wb_ref[...], preferred_element_type=F32)

    acc = jnp.dot(u_ref[...], wb_ref[...], preferred_element_type=F32)
    if head_major:
        for c in range(o_ref.shape[0]):
            o_ref[c] = acc[:, c * A_HEAD_DIM:(c + 1) * A_HEAD_DIM].astype(o_ref.dtype)
    else:
        o_ref[...] = acc.astype(o_ref.dtype)

    step = pl.program_id(0) * pl.num_programs(1) + pl.program_id(1)
    for n, (k0, nsteps) in enumerate(shift_steps):
        cur_ref, nxt_ref, out_ref = shift_in[2 * n], shift_in[2 * n + 1], shift_out[n]

        @pl.when(jnp.logical_and(step >= k0, step < k0 + nsteps))
        def _(cur_ref=cur_ref, nxt_ref=nxt_ref, out_ref=out_ref):
            rb = out_ref.shape[0]
            out_ref[0:rb - 1] = cur_ref[1:rb]
            out_ref[rb - 1] = nxt_ref[0]


def _shift_specs(cache, rb, k0, ni):
    b, w = cache.shape[:2]
    nb = w // rb
    tile = (2, A_HEADS, A_HEAD_DIM)

    def loc(j, i):
        l = jnp.clip(j * ni + i - k0, 0, b * nb - 1)
        return l // nb, l % nb

    def cur_map(j, i):
        bi, si = loc(j, i)
        return bi, si, 0, 0, 0

    def nxt_map(j, i):
        bi, si = loc(j, i)
        return bi, jnp.minimum((si + 1) * rb, w - 1), 0, 0, 0

    return (pl.BlockSpec((None, rb) + tile, cur_map), pl.BlockSpec((None, 1) + tile, nxt_map),
            pl.BlockSpec((None, rb) + tile, cur_map), b * nb)


def _in_proj(u, us, w, col0, ncols, out_dtype, tm, tn, head_major, shift_caches=(), shift_rb=256):
    m, k = u.shape
    ms = us.shape[0]
    j0 = col0 // tn
    nj, ni = ncols // tn, m // tm
    in_specs = [pl.BlockSpec((tm, k), lambda j, i: (i, 0)),
                pl.BlockSpec((ms, k), lambda j, i: (0, 0)),
                pl.BlockSpec((k, tn), lambda j, i: (0, j0 + j))]
    if head_major:
        o_spec = pl.BlockSpec((tn // A_HEAD_DIM, tm, A_HEAD_DIM), lambda j, i: (j, i, 0))
        o_shape = jax.ShapeDtypeStruct((ncols // A_HEAD_DIM, m, A_HEAD_DIM), out_dtype)
    else:
        o_spec = pl.BlockSpec((tm, tn), lambda j, i: (i, j))
        o_shape = jax.ShapeDtypeStruct((m, ncols), out_dtype)
    out_specs = [o_spec, pl.BlockSpec((ms, tn), lambda j, i: (0, j))]
    out_shape = [o_shape, jax.ShapeDtypeStruct((ms, ncols), F32)]
    shift_steps, shift_args, k0 = [], [], 0
    for cache in shift_caches:
        cur, nxt, out, nsteps = _shift_specs(cache, min(shift_rb, cache.shape[1]), k0, ni)
        in_specs += [cur, nxt]
        out_specs.append(out)
        out_shape.append(jax.ShapeDtypeStruct(cache.shape, cache.dtype))
        shift_args += [cache, cache]
        shift_steps.append((k0, nsteps))
        k0 += nsteps
    assert k0 <= nj * ni
    return pl.pallas_call(
        functools.partial(_in_proj_kernel, head_major=head_major, shift_steps=tuple(shift_steps)),
        grid=(nj, ni),
        in_specs=in_specs,
        out_specs=out_specs,
        out_shape=out_shape,
        scratch_shapes=[pltpu.VMEM((k, tn), BF16)],
        compiler_params=_params(("arbitrary", "arbitrary")),
        name="in_proj_hm" if head_major else "in_proj",
    )(u, us, w, *shift_args)


def _attn_kernel(slope_ref, q0, q1, q2, k0, k1, k2, v0, v1, v2, za_ref, o_ref,
                 qbuf, kbuf, vbuf, onat, lnat, *, seq):
    q_refs, k_refs, v_refs = (q0, q1, q2), (k0, k1, k2), (v0, v1, v2)
    h = pl.program_id(1)
    t = pl.program_id(2)

    @pl.when(t == 0)
    def _():
        kbuf[...] = jnp.zeros_like(kbuf)
        vbuf[...] = jnp.zeros_like(vbuf)

    qi = lax.broadcasted_iota(jnp.int32, (BAND, 2 * BAND), 0)
    kj = lax.broadcasted_iota(jnp.int32, (BAND, 2 * BAND), 1)
    dist = qi + BAND - kj
    valid = jnp.logical_and(dist >= 0, dist <= BAND)
    distf = dist.astype(F32)
    first_pen = jnp.where(jnp.logical_and(t == 0, kj < BAND), NEG, 0.0)
    scale = A_HEAD_DIM ** -0.5

    for g in range(N_GROUPS):
        dil = DILATIONS[g]
        n = ATT_TILE // dil
        nblk = n // BAND
        run = seq // dil + BAND
        for r in range(dil):
            rows = pl.ds(r, n, stride=dil) if dil > 1 else pl.ds(0, n)
            dst = pl.ds(pl.multiple_of(r * run + BAND + t * n, BAND), n)
            qbuf[g, r * n:(r + 1) * n, :] = (q_refs[g][rows, :] * scale).astype(BF16)
            kbuf[g, dst, :] = k_refs[g][rows, :].astype(BF16)
            vbuf[g, dst, :] = v_refs[g][rows, :].astype(BF16)

        bias = jnp.where(valid, -(slope_ref[g, h] * float(dil)) * distf, NEG)
        bias_first = bias + first_pen

        for r in range(dil):
            for i in range(nblk):
                u = r * nblk + i
                kv_rows = pl.ds(pl.multiple_of(r * run + i * BAND + t * n, BAND), 2 * BAND)
                q = qbuf[g, u * BAND:(u + 1) * BAND, :]
                s = lax.dot_general(q, kbuf[g, kv_rows, :], (((1,), (1,)), ((), ())),
                                    preferred_element_type=F32)
                s = s + (bias_first if i == 0 else bias)
                m = jnp.max(s, axis=-1, keepdims=True)
                e = jnp.exp(s - m)
                den = jnp.sum(e, axis=-1, keepdims=True)
                acc = jnp.dot(e.astype(BF16), vbuf[g, kv_rows, :], preferred_element_type=F32)
                out_rows = (pl.ds(r + i * (BAND * dil), BAND, stride=dil) if dil > 1
                            else pl.ds(u * BAND, BAND))
                onat[g, out_rows, :] = acc / den
                lnat[g, out_rows, :] = jnp.broadcast_to(m + jnp.log(den), (BAND, A_HEAD_DIM))

    l0, l1, l2 = lnat[0], lnat[1], lnat[2]
    mx = jnp.maximum(jnp.maximum(l0, l1), l2)
    w0, w1, w2 = jnp.exp(l0 - mx), jnp.exp(l1 - mx), jnp.exp(l2 - mx)
    mixed = (w0 * onat[0] + w1 * onat[1] + w2 * onat[2]) / (w0 + w1 + w2)
    o_ref[...] = (mixed * _silu(za_ref[...])).astype(o_ref.dtype)


def _attention(zh, slopes, batch, seq):
    nt = seq // ATT_TILE
    m = batch * seq

    def head_spec(c0):
        return pl.BlockSpec((None, ATT_TILE, A_HEAD_DIM), lambda b, h, t, s: (c0 + h, b * nt + t, 0))

    in_specs = [head_spec(c + g * A_HEADS) for c in (CH_Q, CH_K, CH_V) for g in range(N_GROUPS)]
    in_specs.append(head_spec(CH_ZA))
    buf_rows = seq + BAND * max(DILATIONS)
    grid_spec = pltpu.PrefetchScalarGridSpec(
        num_scalar_prefetch=1,
        grid=(batch, A_HEADS, nt),
        in_specs=in_specs,
        out_specs=pl.BlockSpec((ATT_TILE, A_HEAD_DIM), lambda b, h, t, s: (b * nt + t, h)),
        scratch_shapes=[
            pltpu.VMEM((N_GROUPS, ATT_TILE, A_HEAD_DIM), BF16),
            pltpu.VMEM((N_GROUPS, buf_rows, A_HEAD_DIM), BF16),
            pltpu.VMEM((N_GROUPS, buf_rows, A_HEAD_DIM), BF16),
            pltpu.VMEM((N_GROUPS, ATT_TILE, A_HEAD_DIM), F32),
            pltpu.VMEM((N_GROUPS, ATT_TILE, A_HEAD_DIM), F32),
        ],
    )
    return pl.pallas_call(
        functools.partial(_attn_kernel, seq=seq), grid_spec=grid_spec,
        out_shape=jax.ShapeDtypeStruct((m, A_WIDTH), BF16),
        compiler_params=_params(("parallel", "parallel", "arbitrary")),
        name="dilated_attn",
    )(slopes, *([zh] * 10))


def _retention_kernel(lg_ref, q_ref, k_ref, v_ref, zr_ref, gret_ref, y_ref, rout_ref, r_scr):
    h = pl.program_id(1)
    s = pl.program_id(2)
    log_g = lg_ref[h]

    @pl.when(s == 0)
    def _():
        r_scr[...] = jnp.zeros_like(r_scr)

    L = R_CHUNK
    pi = lax.broadcasted_iota(jnp.int32, (L, L), 0)
    pj = lax.broadcasted_iota(jnp.int32, (L, L), 1)
    rel = (pi - pj).astype(F32)
    intra = jnp.where(rel >= 0, jnp.exp(jnp.maximum(rel, 0.0) * log_g), 0.0)
    pos = lax.broadcasted_iota(jnp.int32, (L, 1), 0).astype(F32)
    q_decay = jnp.exp((pos + 1.0) * log_g)
    k_decay = jnp.exp((L - 1.0 - pos) * log_g)
    chunk_decay = jnp.exp(jnp.full((1, 1), float(L), F32) * log_g)
    gret = gret_ref[...]

    def chunk(c, carry):
        rows = pl.ds(pl.multiple_of(c * L, L), L)
        q = q_ref[rows, :]
        k = k_ref[rows, :] * (R_DIM ** -0.5)
        v = v_ref[rows, :]
        r_prev = r_scr[...]
        scores = lax.dot_general(q, k, (((1,), (1,)), ((), ())), preferred_element_type=F32) * intra
        inner = jnp.dot(scores.astype(BF16), v, preferred_element_type=F32)
        cross = jnp.dot(q, r_prev.astype(BF16), preferred_element_type=F32) * q_decay
        kd = (k.astype(F32) * k_decay).astype(BF16)
        r_scr[...] = r_prev * chunk_decay + lax.dot_general(
            kd, v, (((0,), (0,)), ((), ())), preferred_element_type=F32)
        y = inner + cross
        y = y * lax.rsqrt(jnp.mean(y * y, axis=-1, keepdims=True) + EPS)
        y = (y * gret) * _silu(zr_ref[rows, :].astype(F32))
        y_ref[rows, :] = y.astype(y_ref.dtype)
        return carry

    lax.fori_loop(0, q_ref.shape[0] // L, chunk, 0, unroll=True)

    @pl.when(s == pl.num_programs(2) - 1)
    def _():
        rout_ref[...] = r_scr[...]


def _retention(zr_all, log_g, g_ret, batch, seq, rb):
    ns = seq // rb
    m = batch * seq

    def col_spec(col):
        c0 = (col - HEAD_COLS) // R_DIM
        return pl.BlockSpec((rb, R_DIM), lambda b, h, s, lg: (b * ns + s, c0 + h))

    grid_spec = pltpu.PrefetchScalarGridSpec(
        num_scalar_prefetch=1,
        grid=(batch, R_HEADS, ns),
        in_specs=[col_spec(COL_QR), col_spec(COL_KR), col_spec(COL_VR), col_spec(COL_ZR),
                  pl.BlockSpec((1, R_DIM), lambda b, h, s, lg: (0, h))],
        out_specs=[pl.BlockSpec((rb, R_DIM), lambda b, h, s, lg: (b * ns + s, h)),
                   pl.BlockSpec((None, None, R_DIM, R_DIM), lambda b, h, s, lg: (b, h, 0, 0))],
        scratch_shapes=[pltpu.VMEM((R_DIM, R_DIM), F32)],
    )
    return pl.pallas_call(
        _retention_kernel, grid_spec=grid_spec,
        out_shape=[jax.ShapeDtypeStruct((m, R_WIDTH), BF16),
                   jax.ShapeDtypeStruct((batch, R_HEADS, R_DIM, R_DIM), F32)],
        compiler_params=_params(("parallel", "parallel", "arbitrary")),
        name="retention",
    )(log_g, zr_all, zr_all, zr_all, zr_all, g_ret.reshape(1, R_WIDTH))


def _out_kernel(ya_ref, yr_ref, ga_ref, gb_ref, x_ref, p_ref,
                wa_ref, wr_ref, wo_ref, gpost_ref, wg_ref, wp_ref, o_ref):
    a = jnp.dot(ya_ref[...].astype(BF16), wa_ref[...], preferred_element_type=F32)
    b = jnp.dot(yr_ref[...].astype(BF16), wr_ref[...], preferred_element_type=F32)
    merged = (jax.nn.sigmoid(ga_ref[...].astype(F32)) * a
              + jax.nn.sigmoid(gb_ref[...].astype(F32)) * b)
    y = jnp.dot(merged.astype(BF16), wo_ref[...], preferred_element_type=F32)
    y = y * lax.rsqrt(jnp.mean(y * y, axis=-1, keepdims=True) + EPS)
    hres = x_ref[...] + y * gpost_ref[...]
    gate = jax.nn.sigmoid(jnp.dot(hres.astype(BF16), wg_ref[...], preferred_element_type=F32))
    emb = jnp.dot(p_ref[...].astype(BF16), wp_ref[...], preferred_element_type=F32)
    o_ref[...] = hres + gate * emb


def _out_proj(ya, yr, gates, ga_blk, gb_blk, x, p, wa, wr, wo, g_post, wg, wp, tm):
    m = x.shape[0]

    def rows(width, cb=0):
        return pl.BlockSpec((tm, width), lambda i: (i, cb))

    def whole(arr):
        return pl.BlockSpec(arr.shape, lambda i: (0, 0), pipeline_mode=pl.Buffered(1))

    gp = g_post.reshape(1, D_MODEL)
    return pl.pallas_call(
        _out_kernel,
        grid=(m // tm,),
        in_specs=[rows(A_WIDTH), rows(R_WIDTH), rows(D_MODEL, ga_blk), rows(D_MODEL, gb_blk),
                  rows(D_MODEL), rows(PLE_DIM),
                  whole(wa), whole(wr), whole(wo), whole(gp), whole(wg), whole(wp)],
        out_specs=rows(D_MODEL),
        out_shape=jax.ShapeDtypeStruct((m, D_MODEL), F32),
        compiler_params=_params(("parallel",)),
        name="out_proj",
    )(ya, yr, gates, gates, x, p, wa, wr, wo, gp, wg, wp)


def _attn_step_kernel(slope_ref, z_ref, c0_ref, c1_ref, c2_ref, o_ref):
    c_refs = (c0_ref, c1_ref, c2_ref)
    scale = A_HEAD_DIM ** -0.5
    steps = float(BAND) - lax.broadcasted_iota(jnp.int32, (BAND, 1, 1), 0).astype(F32)
    outs, lses = [], []
    for g in range(N_GROUPS):
        q = z_ref[CH_Q + g * A_HEADS:CH_Q + (g + 1) * A_HEADS, :] * scale
        kn = z_ref[CH_K + g * A_HEADS:CH_K + (g + 1) * A_HEADS, :]
        vn = z_ref[CH_V + g * A_HEADS:CH_V + (g + 1) * A_HEADS, :]
        kc = c_refs[g][:, 0]
        vc = c_refs[g][:, 1]
        slope = slope_ref[g][None]
        s = jnp.sum(kc * q[None], axis=-1, keepdims=True) - (slope * float(DILATIONS[g])) * steps
        s_new = jnp.sum(kn * q, axis=-1, keepdims=True)
        m = jnp.maximum(jnp.max(s, axis=0), s_new)
        e = jnp.exp(s - m[None])
        e_new = jnp.exp(s_new - m)
        den = jnp.sum(e, axis=0) + e_new
        outs.append((jnp.sum(e * vc, axis=0) + e_new * vn) / den)
        lses.append(m + jnp.log(den))
    mx = jnp.maximum(jnp.maximum(lses[0], lses[1]), lses[2])
    ws = [jnp.exp(l - mx) for l in lses]
    mixed = (ws[0] * outs[0] + ws[1] * outs[1] + ws[2] * outs[2]) / (ws[0] + ws[1] + ws[2])
    o_ref[...] = mixed * _silu(z_ref[CH_ZA:CH_ZA + A_HEADS, :])


def _attention_step(zs_h, caches, slopes):
    b = zs_h.shape[0]
    z3 = zs_h.reshape(b, N_HEAD_CHUNKS, A_HEAD_DIM)
    views = [c.reshape(b, BAND, DILATIONS[g], 2, A_HEADS, A_HEAD_DIM) for g, c in enumerate(caches)]
    cache_spec = pl.BlockSpec((None, BAND, None, 2, A_HEADS, A_HEAD_DIM),
                              lambda i: (i, 0, 0, 0, 0, 0))
    out = pl.pallas_call(
        _attn_step_kernel,
        grid=(b,),
        in_specs=[pl.BlockSpec((N_GROUPS, A_HEADS, 1), lambda i: (0, 0, 0)),
                  pl.BlockSpec((None, N_HEAD_CHUNKS, A_HEAD_DIM), lambda i: (i, 0, 0)),
                  cache_spec, cache_spec, cache_spec],
        out_specs=pl.BlockSpec((None, A_HEADS, A_HEAD_DIM), lambda i: (i, 0, 0)),
        out_shape=jax.ShapeDtypeStruct((b, A_HEADS, A_HEAD_DIM), F32),
        compiler_params=_params(("parallel",)),
        name="dilated_attn_step",
    )(slopes.reshape(N_GROUPS, A_HEADS, 1), z3, *views)
    return out.reshape(b, A_WIDTH)


def _retention_step_kernel(lg_ref, q_ref, k_ref, v_ref, zr_ref, gret_ref, r_ref, y_ref, rout_ref):
    for hh in range(R_HEADS):
        cols = slice(hh * R_DIM, (hh + 1) * R_DIM)
        gamma = jnp.exp(jnp.full((1, 1), 1.0, F32) * lg_ref[hh])
        q = q_ref[:, cols]
        k = k_ref[:, cols] * (R_DIM ** -0.5)
        v = v_ref[:, cols]
        r_prev = r_ref[hh]
        qb = jnp.broadcast_to(q, (8, R_DIM)).astype(BF16)
        cross = jnp.dot(qb, r_prev.astype(BF16), preferred_element_type=F32)[0:1] * gamma
        inner = jnp.sum(q * k, axis=-1, keepdims=True) * v
        k_col = jnp.broadcast_to(k, (R_DIM, R_DIM)).T
        rout_ref[hh] = r_prev * gamma + k_col * v
        y = inner + cross
        y = y * lax.rsqrt(jnp.mean(y * y, axis=-1, keepdims=True) + EPS)
        y_ref[:, cols] = (y * gret_ref[:, cols]) * _silu(zr_ref[:, cols])


def _retention_step(zs_r, state, log_g, g_ret):
    b = zs_r.shape[0]
    z3 = zs_r.reshape(b, 1, REST_COLS)

    def col_spec(col):
        return pl.BlockSpec((None, 1, R_WIDTH), lambda i, lg: (i, 0, (col - HEAD_COLS) // R_WIDTH))

    state_spec = pl.BlockSpec((None, R_HEADS, R_DIM, R_DIM), lambda i, lg: (i, 0, 0, 0))
    grid_spec = pltpu.PrefetchScalarGridSpec(
        num_scalar_prefetch=1,
        grid=(b,),
        in_specs=[col_spec(COL_QR), col_spec(COL_KR), col_spec(COL_VR), col_spec(COL_ZR),
                  pl.BlockSpec((1, R_WIDTH), lambda i, lg: (0, 0)), state_spec],
        out_specs=[pl.BlockSpec((None, 1, R_WIDTH), lambda i, lg: (i, 0, 0)), state_spec],
    )
    y, new_state = pl.pallas_call(
        _retention_step_kernel, grid_spec=grid_spec,
        out_shape=[jax.ShapeDtypeStruct((b, 1, R_WIDTH), F32),
                   jax.ShapeDtypeStruct(state.shape, F32)],
        compiler_params=_params(("parallel",)),
        name="retention_step",
    )(log_g, z3, z3, z3, z3, g_ret.reshape(1, R_WIDTH), state)
    return y.reshape(b, R_WIDTH), new_state


def _window_new_row_kernel(z_ref, c0, c1, c2, o0, o1, o2):
    del c0, c1, c2
    for g, o_ref in enumerate((o0, o1, o2)):
        o_ref[0, 0] = z_ref[CH_K + g * A_HEADS:CH_K + (g + 1) * A_HEADS, :]
        o_ref[0, 1] = z_ref[CH_V + g * A_HEADS:CH_V + (g + 1) * A_HEADS, :]


def _window_new_row(zs_h, shifted):
    b = zs_h.shape[0]
    tile = (2, A_HEADS, A_HEAD_DIM)
    last_row = [pl.BlockSpec((None, 1) + tile, functools.partial(lambda i, w: (i, w - 1, 0, 0, 0), w=c.shape[1]))
                for c in shifted]
    return pl.pallas_call(
        _window_new_row_kernel,
        grid=(b,),
        in_specs=[pl.BlockSpec((None, N_HEAD_CHUNKS, A_HEAD_DIM), lambda i: (i, 0, 0))] + last_row,
        out_specs=last_row,
        out_shape=[jax.ShapeDtypeStruct(c.shape, c.dtype) for c in shifted],
        input_output_aliases={1: 0, 2: 1, 3: 2},
        compiler_params=_params(("parallel",)),
        name="window_new_row",
    )(zs_h.reshape(b, N_HEAD_CHUNKS, A_HEAD_DIM), *shifted)


def _window_prompt_kernel(i_ref, o_ref):
    for hh in range(A_HEADS):
        o_ref[:, hh, :] = i_ref[hh]


def _window_prompt(zh, g, batch, seq, rb):
    keep = min(WINDOWS[g], seq)
    rb = min(rb, keep)
    row0 = (seq - keep) // rb
    out = pl.pallas_call(
        _window_prompt_kernel,
        grid=(batch, 2, keep // rb),
        in_specs=[pl.BlockSpec((A_HEADS, rb, A_HEAD_DIM),
                               lambda b, kv, s: ((CH_K + kv * (CH_V - CH_K)) // A_HEADS + g,
                                                 b * (seq // rb) + row0 + s, 0))],
        out_specs=pl.BlockSpec((rb, A_HEADS, A_HEAD_DIM), lambda b, kv, s: (b * (keep // rb) + s, kv, 0)),
        out_shape=jax.ShapeDtypeStruct((batch * keep, 2 * A_HEADS, A_HEAD_DIM), F32),
        compiler_params=_params(("parallel", "parallel", "parallel")),
        name="window_prompt",
    )(zh)
    return out.reshape(batch, keep, 2, A_HEADS, A_HEAD_DIM)


def _alibi_slopes():
    n = N_GROUPS * A_HEADS
    return jnp.exp2(-8.0 * (jnp.arange(n, dtype=F32) + 1.0) / n).reshape(N_GROUPS, A_HEADS)


def _retention_log_decay():
    return jnp.log1p(-jnp.exp2(-5.0 - jnp.arange(R_HEADS, dtype=F32)))


def kernel(x_prompt, x_sample, cache_win0, cache_win1, cache_win2, state_ret, p_prompt, p_sample,
           g_pre, w_in, g_ret, w_a_out, w_r_out, w_o, g_post, w_ple_gate, w_ple_proj):
    batch, seq, _ = x_prompt.shape
    dec_batch = x_sample.shape[0]
    assert g_pre.shape[0] == 1 and x_sample.shape[1] == 1
    assert seq % ATT_TILE == 0
    m = batch * seq
    slopes = _alibi_slopes()
    log_g = _retention_log_decay()

    wa, wr, wo = w_a_out[0].astype(BF16), w_r_out[0].astype(BF16), w_o[0].astype(BF16)
    wg, wp = w_ple_gate[0].astype(BF16), w_ple_proj[0].astype(BF16)

    xp = x_prompt.reshape(m, D_MODEL)
    xs = x_sample.reshape(dec_batch, D_MODEL)
    u = _rmsnorm(xp, g_pre[0], 512)
    us = _rmsnorm(xs, g_pre[0], dec_batch)
    caches = (cache_win0[0], cache_win1[0], cache_win2[0])
    zh, zs_h, sh0, sh1 = _in_proj(u, us, w_in[0], 0, HEAD_COLS, F32, 1024, 1024, head_major=True,
                                  shift_caches=caches[:2])
    zr, zs_r, sh2 = _in_proj(u, us, w_in[0], HEAD_COLS, REST_COLS, BF16, 1024, 1024, head_major=False,
                             shift_caches=caches[2:])
    ga_blk = (COL_GA - HEAD_COLS) // D_MODEL

    ya = _attention(zh, slopes, batch, seq)
    yr, ret_prompt = _retention(zr, log_g, g_ret[0], batch, seq, 1024)
    y_prompt = _out_proj(ya, yr, zr, ga_blk, ga_blk + 1, xp, p_prompt[0].reshape(m, PLE_DIM),
                         wa, wr, wo, g_post[0], wg, wp, 256).reshape(batch, seq, D_MODEL)

    ya_s = _attention_step(zs_h, caches, slopes)
    yr_s, ret_sample = _retention_step(zs_r, state_ret[0], log_g, g_ret[0])
    y_sample = _out_proj(ya_s, yr_s, zs_r, ga_blk, ga_blk + 1, xs, p_sample[0].reshape(dec_batch, PLE_DIM),
                         wa, wr, wo, g_post[0], wg, wp, dec_batch).reshape(dec_batch, 1, D_MODEL)

    win_p = [_window_prompt(zh, g, batch, seq, 512)[None] for g in range(N_GROUPS)]
    win_s = _window_new_row(zs_h, (sh0, sh1, sh2))
    return (y_prompt, y_sample, win_p[0], win_p[1], win_p[2], ret_prompt[None],
            win_s[0][None], win_s[1][None], win_s[2][None], ret_sample[None])
```

```python
import functools

import jax
import jax.numpy as jnp
from jax import lax
from jax.experimental import pallas as pl
from jax.experimental.pallas import tpu as pltpu

F32 = jnp.float32
BF16 = jnp.bfloat16

D_MODEL = 2048
N_GROUPS = 3
DILATIONS = (1, 4, 16)
WINDOWS = (128, 512, 2048)
BAND = 128
A_HEADS = 8
A_HEAD_DIM = 128
A_QKV = N_GROUPS * A_HEADS * A_HEAD_DIM
A_WIDTH = A_HEADS * A_HEAD_DIM
R_HEADS = 8
R_DIM = 256
R_WIDTH = R_HEADS * R_DIM
R_CHUNK = 128
PLE_DIM = 256
EPS = 1e-6
N_IN = 3 * A_QKV + A_WIDTH + 4 * R_WIDTH + 2 * D_MODEL

COL_QA, COL_KA, COL_VA = 0, A_QKV, 2 * A_QKV
COL_ZA = 3 * A_QKV
COL_QR = COL_ZA + A_WIDTH
COL_KR = COL_QR + R_WIDTH
COL_VR = COL_KR + R_WIDTH
COL_ZR = COL_VR + R_WIDTH
COL_GA = COL_ZR + R_WIDTH
COL_GB = COL_GA + D_MODEL

HEAD_COLS = COL_QR
REST_COLS = N_IN - HEAD_COLS
CH_Q, CH_K, CH_V, CH_ZA = (c // A_HEAD_DIM for c in (COL_QA, COL_KA, COL_VA, COL_ZA))
N_HEAD_CHUNKS = HEAD_COLS // A_HEAD_DIM

NEG = -1e30
ATT_TILE = BAND * max(DILATIONS)

assert R_DIM ** -0.5 == 2.0 ** -4

TM_NORM = 512
TM_IN, TN_IN = 1024, 1024
RIDER_ROWS = 256
RET_ROWS = 1024
TM_OUT = 256
VMEM_LIMIT = 56 * 1024 * 1024


def _params(semantics, vmem=VMEM_LIMIT):
    return pltpu.CompilerParams(dimension_semantics=semantics, vmem_limit_bytes=vmem)


def _silu(x):
    return x * jax.nn.sigmoid(x)


def _rmsnorm_kernel(x_ref, g_ref, o_ref):
    x = x_ref[...]
    y = x * lax.rsqrt(jnp.mean(x * x, axis=-1, keepdims=True) + EPS)
    o_ref[...] = (y * g_ref[...]).astype(o_ref.dtype)


def _rmsnorm(x, g, tm):
    m, d = x.shape
    return pl.pallas_call(
        _rmsnorm_kernel,
        grid=(m // tm,),
        in_specs=[pl.BlockSpec((tm, d), lambda i: (i, 0)),
                  pl.BlockSpec((1, d), lambda i: (0, 0))],
        out_specs=pl.BlockSpec((tm, d), lambda i: (i, 0)),
        out_shape=jax.ShapeDtypeStruct((m, d), BF16),
        compiler_params=_params(("parallel",)),
        name="rmsnorm",
    )(x, g.reshape(1, d))


def _shift_copy(cur_ref, nxt_ref, out_ref):
    rb = out_ref.shape[0]
    out_ref[0:rb - 1] = cur_ref[1:rb]
    out_ref[rb - 1] = nxt_ref[0]


def _heads_to_rows_copy(in_ref, out_ref):
    for hh in range(A_HEADS):
        out_ref[:, hh, :] = in_ref[hh]


def _in_proj_kernel(*refs, head_major, silu_tiles, riders):
    n_in = sum(r[1] for r in riders)
    u_ref, us_ref, w_ref = refs[:3]
    rider_in = refs[3:3 + n_in]
    o_ref, os_ref = refs[3 + n_in:5 + n_in]
    rider_out = refs[5 + n_in:5 + n_in + len(riders)]
    wb_ref = refs[5 + n_in + len(riders)]
    j = pl.program_id(0)

    @pl.when(pl.program_id(1) == 0)
    def _():
        wb_ref[...] = w_ref[...].astype(BF16)
        os_ref[...] = jnp.dot(us_ref[...], wb_ref[...], preferred_element_type=F32)

    def project(act):
        acc = jnp.dot(u_ref[...], wb_ref[...], preferred_element_type=F32)
        if act is not None:
            acc = act(acc)
        if head_major:
            for c in range(o_ref.shape[0]):
                o_ref[c] = acc[:, c * A_HEAD_DIM:(c + 1) * A_HEAD_DIM].astype(o_ref.dtype)
        else:
            o_ref[...] = acc.astype(o_ref.dtype)

    gated = jnp.logical_and(j >= silu_tiles[0], j < silu_tiles[1])
    pl.when(gated)(functools.partial(project, _silu))
    pl.when(jnp.logical_not(gated))(functools.partial(project, None))

    step = j * pl.num_programs(1) + pl.program_id(1)
    pos = 0
    for (copy, nin, k0, nsteps), out_ref in zip(riders, rider_out):
        active = jnp.logical_and(step >= k0, step < k0 + nsteps)
        pl.when(active)(functools.partial(copy, *rider_in[pos:pos + nin], out_ref))
        pos += nin


def _shift_rider(cache, rb, k0, ni):
    b, w = cache.shape[:2]
    nb = w // rb
    tile = (2, A_HEADS, A_HEAD_DIM)

    def loc(j, i):
        l = jnp.clip(j * ni + i - k0, 0, b * nb - 1)
        return l // nb, l % nb

    def cur_map(j, i):
        bi, si = loc(j, i)
        return bi, si, 0, 0, 0

    def nxt_map(j, i):
        bi, si = loc(j, i)
        return bi, jnp.minimum((si + 1) * rb, w - 1), 0, 0, 0

    return dict(copy=_shift_copy, args=[cache, cache], steps=b * nb,
                in_specs=[pl.BlockSpec((None, rb) + tile, cur_map), pl.BlockSpec((None, 1) + tile, nxt_map)],
                out_spec=pl.BlockSpec((None, rb) + tile, cur_map),
                out_shape=jax.ShapeDtypeStruct(cache.shape, cache.dtype))


def _window_prompt_rider(zh, g, batch, seq, rb, k0, ni):
    keep = min(WINDOWS[g], seq)
    rb = min(rb, keep)
    nsb = keep // rb
    row0 = (seq - keep) // rb

    def loc(j, i):
        l = jnp.clip(j * ni + i - k0, 0, batch * 2 * nsb - 1)
        return l // (2 * nsb), (l // nsb) % 2, l % nsb

    def in_map(j, i):
        b, kv, s = loc(j, i)
        return (CH_K + kv * (CH_V - CH_K)) // A_HEADS + g, b * (seq // rb) + row0 + s, 0

    def out_map(j, i):
        b, kv, s = loc(j, i)
        return b * nsb + s, kv, 0

    return dict(copy=_heads_to_rows_copy, args=[zh], steps=batch * 2 * nsb,
                in_specs=[pl.BlockSpec((A_HEADS, rb, A_HEAD_DIM), in_map)],
                out_spec=pl.BlockSpec((rb, A_HEADS, A_HEAD_DIM), out_map),
                out_shape=jax.ShapeDtypeStruct((batch * keep, 2 * A_HEADS, A_HEAD_DIM), zh.dtype))


def _in_proj(u, us, w, col0, ncols, out_dtype, tm, tn, head_major, silu_cols, make_riders=()):
    m, k = u.shape
    ms = us.shape[0]
    j0 = col0 // tn
    nj, ni = ncols // tn, m // tm
    in_specs = [pl.BlockSpec((tm, k), lambda j, i: (i, 0)),
                pl.BlockSpec((ms, k), lambda j, i: (0, 0)),
                pl.BlockSpec((k, tn), lambda j, i: (0, j0 + j))]
    if head_major:
        o_spec = pl.BlockSpec((tn // A_HEAD_DIM, tm, A_HEAD_DIM), lambda j, i: (j, i, 0))
        o_shape = jax.ShapeDtypeStruct((ncols // A_HEAD_DIM, m, A_HEAD_DIM), out_dtype)
    else:
        o_spec = pl.BlockSpec((tm, tn), lambda j, i: (i, j))
        o_shape = jax.ShapeDtypeStruct((m, ncols), out_dtype)
    out_specs = [o_spec, pl.BlockSpec((ms, tn), lambda j, i: (0, j))]
    out_shape = [o_shape, jax.ShapeDtypeStruct((ms, ncols), F32)]
    riders, rider_args = [], []
    for k0, make in make_riders:
        rider = make(k0, ni)
        in_specs += rider["in_specs"]
        out_specs.append(rider["out_spec"])
        out_shape.append(rider["out_shape"])
        rider_args += rider["args"]
        riders.append((rider["copy"], len(rider["in_specs"]), k0, rider["steps"]))
        assert k0 + rider["steps"] <= nj * ni
    silu_tiles = ((silu_cols[0] - col0) // tn, (silu_cols[1] - col0) // tn)
    return pl.pallas_call(
        functools.partial(_in_proj_kernel, head_major=head_major, silu_tiles=silu_tiles,
                          riders=tuple(riders)),
        grid=(nj, ni),
        in_specs=in_specs,
        out_specs=out_specs,
        out_shape=out_shape,
        scratch_shapes=[pltpu.VMEM((k, tn), BF16)],
        compiler_params=_params(("arbitrary", "arbitrary")),
        name="in_proj_hm" if head_major else "in_proj",
    )(u, us, w, *rider_args)


def _attn_kernel(slope_ref, q0, q1, q2, k0, k1, k2, v0, v1, v2, za_ref, o_ref,
                 qbuf, kbuf, vbuf, onat, lnat, *, seq):
    q_refs, k_refs, v_refs = (q0, q1, q2), (k0, k1, k2), (v0, v1, v2)
    h = pl.program_id(1)
    t = pl.program_id(2)

    @pl.when(t == 0)
    def _():
        for g, dil in enumerate(DILATIONS):
            for r in range(dil):
                pad = pl.ds(r * (seq // dil + BAND), BAND)
                kbuf[g, pad, :] = jnp.zeros((BAND, A_HEAD_DIM), BF16)
                vbuf[g, pad, :] = jnp.zeros((BAND, A_HEAD_DIM), BF16)

    qi = lax.broadcasted_iota(jnp.int32, (BAND, 2 * BAND), 0)
    kj = lax.broadcasted_iota(jnp.int32, (BAND, 2 * BAND), 1)
    dist = qi + BAND - kj
    valid = jnp.logical_and(dist >= 0, dist <= BAND)
    distf = dist.astype(F32)
    first_pen = jnp.where(jnp.logical_and(t == 0, kj < BAND), NEG, 0.0)
    scale = A_HEAD_DIM ** -0.5

    for g in range(N_GROUPS):
        dil = DILATIONS[g]
        n = ATT_TILE // dil
        nblk = n // BAND
        run = seq // dil + BAND
        for r in range(dil):
            rows = pl.ds(r, n, stride=dil) if dil > 1 else pl.ds(0, n)
            dst = pl.ds(pl.multiple_of(r * run + BAND + t * n, BAND), n)
            qbuf[g, r * n:(r + 1) * n, :] = (q_refs[g][rows, :] * scale).astype(BF16)
            kbuf[g, dst, :] = k_refs[g][rows, :].astype(BF16)
            vbuf[g, dst, :] = v_refs[g][rows, :].astype(BF16)

        bias = jnp.where(valid, -(slope_ref[g, h] * float(dil)) * distf, NEG)
        bias_first = bias + first_pen

        for r in range(dil):
            for i in range(nblk):
                u = r * nblk + i
                kv_rows = pl.ds(pl.multiple_of(r * run + i * BAND + t * n, BAND), 2 * BAND)
                q = qbuf[g, u * BAND:(u + 1) * BAND, :]
                s = lax.dot_general(q, kbuf[g, kv_rows, :], (((1,), (1,)), ((), ())),
                                    preferred_element_type=F32)
                s = s + (bias_first if i == 0 else bias)
                m = jnp.max(s, axis=-1, keepdims=True)
                e = jnp.exp(s - m)
                den = jnp.sum(e, axis=-1, keepdims=True)
                acc = jnp.dot(e.astype(BF16), vbuf[g, kv_rows, :], preferred_element_type=F32)
                out_rows = (pl.ds(r + i * (BAND * dil), BAND, stride=dil) if dil > 1
                            else pl.ds(u * BAND, BAND))
                onat[g, out_rows, :] = acc / den
                lnat[g, out_rows, :] = jnp.broadcast_to(m + jnp.log(den), (BAND, A_HEAD_DIM))

    l0, l1, l2 = lnat[0], lnat[1], lnat[2]
    mx = jnp.maximum(jnp.maximum(l0, l1), l2)
    w0, w1, w2 = jnp.exp(l0 - mx), jnp.exp(l1 - mx), jnp.exp(l2 - mx)
    mixed = (w0 * onat[0] + w1 * onat[1] + w2 * onat[2]) / (w0 + w1 + w2)
    o_ref[...] = (mixed * za_ref[...]).astype(o_ref.dtype)


def _attention(zh, slopes, batch, seq):
    nt = seq // ATT_TILE
    m = batch * seq

    def head_spec(c0):
        return pl.BlockSpec((None, ATT_TILE, A_HEAD_DIM), lambda b, h, t, s: (c0 + h, b * nt + t, 0))

    in_specs = [head_spec(c + g * A_HEADS) for c in (CH_Q, CH_K, CH_V) for g in range(N_GROUPS)]
    in_specs.append(head_spec(CH_ZA))
    buf_rows = seq + BAND * max(DILATIONS)
    grid_spec = pltpu.PrefetchScalarGridSpec(
        num_scalar_prefetch=1,
        grid=(batch, A_HEADS, nt),
        in_specs=in_specs,
        out_specs=pl.BlockSpec((ATT_TILE, A_HEAD_DIM), lambda b, h, t, s: (b * nt + t, h)),
        scratch_shapes=[
            pltpu.VMEM((N_GROUPS, ATT_TILE, A_HEAD_DIM), BF16),
            pltpu.VMEM((N_GROUPS, buf_rows, A_HEAD_DIM), BF16),
            pltpu.VMEM((N_GROUPS, buf_rows, A_HEAD_DIM), BF16),
            pltpu.VMEM((N_GROUPS, ATT_TILE, A_HEAD_DIM), F32),
            pltpu.VMEM((N_GROUPS, ATT_TILE, A_HEAD_DIM), F32),
        ],
    )
    return pl.pallas_call(
        functools.partial(_attn_kernel, seq=seq), grid_spec=grid_spec,
        out_shape=jax.ShapeDtypeStruct((m, A_WIDTH), BF16),
        compiler_params=_params(("parallel", "parallel", "arbitrary")),
        name="dilated_attn",
    )(slopes, *([zh] * 10))


def _retention_kernel(lg_ref, q_ref, k_ref, v_ref, zr_ref, gret_ref, y_ref, rout_ref, r_scr):
    h = pl.program_id(1)
    s = pl.program_id(2)
    log_g = lg_ref[h]

    @pl.when(s == 0)
    def _():
        r_scr[...] = jnp.zeros_like(r_scr)

    L = R_CHUNK
    pi = lax.broadcasted_iota(jnp.int32, (L, L), 0)
    pj = lax.broadcasted_iota(jnp.int32, (L, L), 1)
    rel = (pi - pj).astype(F32)
    k_scale = R_DIM ** -0.5
    intra = jnp.where(rel >= 0, jnp.exp(jnp.maximum(rel, 0.0) * log_g), 0.0) * k_scale
    pos = lax.broadcasted_iota(jnp.int32, (L, 1), 0).astype(F32)
    q_decay = jnp.exp((pos + 1.0) * log_g)
    k_decay = jnp.exp((L - 1.0 - pos) * log_g) * k_scale
    chunk_decay = jnp.exp(jnp.full((1, 1), float(L), F32) * log_g)
    gret = gret_ref[...]

    def chunk(c, carry):
        rows = pl.ds(pl.multiple_of(c * L, L), L)
        q = q_ref[rows, :]
        k = k_ref[rows, :]
        v = v_ref[rows, :]
        r_prev = r_scr[...]
        scores = lax.dot_general(q, k, (((1,), (1,)), ((), ())), preferred_element_type=F32) * intra
        inner = jnp.dot(scores.astype(BF16), v, preferred_element_type=F32)
        cross = jnp.dot(q, r_prev.astype(BF16), preferred_element_type=F32) * q_decay
        kd = (k.astype(F32) * k_decay).astype(BF16)
        r_scr[...] = r_prev * chunk_decay + lax.dot_general(
            kd, v, (((0,), (0,)), ((), ())), preferred_element_type=F32)
        y = inner + cross
        y = y * lax.rsqrt(jnp.mean(y * y, axis=-1, keepdims=True) + EPS)
        y = (y * gret) * zr_ref[rows, :].astype(F32)
        y_ref[rows, :] = y.astype(y_ref.dtype)
        return carry

    lax.fori_loop(0, q_ref.shape[0] // L, chunk, 0, unroll=True)

    @pl.when(s == pl.num_programs(2) - 1)
    def _():
        rout_ref[...] = r_scr[...]


def _retention(zr_all, log_g, g_ret, batch, seq, rb):
    ns = seq // rb
    m = batch * seq

    def col_spec(col):
        c0 = (col - HEAD_COLS) // R_DIM
        return pl.BlockSpec((rb, R_DIM), lambda b, h, s, lg: (b * ns + s, c0 + h))

    grid_spec = pltpu.PrefetchScalarGridSpec(
        num_scalar_prefetch=1,
        grid=(batch, R_HEADS, ns),
        in_specs=[col_spec(COL_QR), col_spec(COL_KR), col_spec(COL_VR), col_spec(COL_ZR),
                  pl.BlockSpec((1, R_DIM), lambda b, h, s, lg: (0, h))],
        out_specs=[pl.BlockSpec((rb, R_DIM), lambda b, h, s, lg: (b * ns + s, h)),
                   pl.BlockSpec((None, None, R_DIM, R_DIM), lambda b, h, s, lg: (b, h, 0, 0))],
        scratch_shapes=[pltpu.VMEM((R_DIM, R_DIM), F32)],
    )
    return pl.pallas_call(
        _retention_kernel, grid_spec=grid_spec,
        out_shape=[jax.ShapeDtypeStruct((m, R_WIDTH), BF16),
                   jax.ShapeDtypeStruct((batch, R_HEADS, R_DIM, R_DIM), F32)],
        compiler_params=_params(("parallel", "parallel", "arbitrary")),
        name="retention",
    )(log_g, zr_all, zr_all, zr_all, zr_all, g_ret.reshape(1, R_WIDTH))


def _out_kernel(ya_ref, yr_ref, ga_ref, gb_ref, x_ref, p_ref,
                wa_ref, wr_ref, wo_ref, gpost_ref, wg_ref, wp_ref, o_ref):
    a = jnp.dot(ya_ref[...].astype(BF16), wa_ref[...], preferred_element_type=F32)
    b = jnp.dot(yr_ref[...].astype(BF16), wr_ref[...], preferred_element_type=F32)
    merged = (jax.nn.sigmoid(ga_ref[...].astype(F32)) * a
              + jax.nn.sigmoid(gb_ref[...].astype(F32)) * b)
    y = jnp.dot(merged.astype(BF16), wo_ref[...], preferred_element_type=F32)
    y = y * lax.rsqrt(jnp.mean(y * y, axis=-1, keepdims=True) + EPS)
    hres = x_ref[...] + y * gpost_ref[...]
    gate = jax.nn.sigmoid(jnp.dot(hres.astype(BF16), wg_ref[...], preferred_element_type=F32))
    emb = jnp.dot(p_ref[...].astype(BF16), wp_ref[...], preferred_element_type=F32)
    o_ref[...] = hres + gate * emb


def _out_proj(ya, yr, gates, ga_blk, gb_blk, x, p, wa, wr, wo, g_post, wg, wp, tm):
    m = x.shape[0]

    def rows(width, cb=0):
        return pl.BlockSpec((tm, width), lambda i: (i, cb))

    def whole(arr):
        return pl.BlockSpec(arr.shape, lambda i: (0, 0), pipeline_mode=pl.Buffered(1))

    gp = g_post.reshape(1, D_MODEL)
    return pl.pallas_call(
        _out_kernel,
        grid=(m // tm,),
        in_specs=[rows(A_WIDTH), rows(R_WIDTH), rows(D_MODEL, ga_blk), rows(D_MODEL, gb_blk),
                  rows(D_MODEL), rows(PLE_DIM),
                  whole(wa), whole(wr), whole(wo), whole(gp), whole(wg), whole(wp)],
        out_specs=rows(D_MODEL),
        out_shape=jax.ShapeDtypeStruct((m, D_MODEL), F32),
        compiler_params=_params(("parallel",)),
        name="out_proj",
    )(ya, yr, gates, gates, x, p, wa, wr, wo, gp, wg, wp)


def _attn_step_kernel(slope_ref, z_ref, c0_ref, c1_ref, c2_ref, o_ref):
    c_refs = (c0_ref, c1_ref, c2_ref)
    scale = A_HEAD_DIM ** -0.5
    steps = float(BAND) - lax.broadcasted_iota(jnp.int32, (BAND, 1, 1), 0).astype(F32)
    outs, lses = [], []
    for g in range(N_GROUPS):
        q = z_ref[CH_Q + g * A_HEADS:CH_Q + (g + 1) * A_HEADS, :] * scale
        kn = z_ref[CH_K + g * A_HEADS:CH_K + (g + 1) * A_HEADS, :]
        vn = z_ref[CH_V + g * A_HEADS:CH_V + (g + 1) * A_HEADS, :]
        kc = c_refs[g][:, 0]
        vc = c_refs[g][:, 1]
        slope = slope_ref[g][None]
        s = jnp.sum(kc * q[None], axis=-1, keepdims=True) - (slope * float(DILATIONS[g])) * steps
        s_new = jnp.sum(kn * q, axis=-1, keepdims=True)
        m = jnp.maximum(jnp.max(s, axis=0), s_new)
        e = jnp.exp(s - m[None])
        e_new = jnp.exp(s_new - m)
        den = jnp.sum(e, axis=0) + e_new
        outs.append((jnp.sum(e * vc, axis=0) + e_new * vn) / den)
        lses.append(m + jnp.log(den))
    mx = jnp.maximum(jnp.maximum(lses[0], lses[1]), lses[2])
    ws = [jnp.exp(l - mx) for l in lses]
    mixed = (ws[0] * outs[0] + ws[1] * outs[1] + ws[2] * outs[2]) / (ws[0] + ws[1] + ws[2])
    o_ref[...] = mixed * _silu(z_ref[CH_ZA:CH_ZA + A_HEADS, :])


def _attention_step(zs_h, caches, slopes):
    b = zs_h.shape[0]
    z3 = zs_h.reshape(b, N_HEAD_CHUNKS, A_HEAD_DIM)
    views = [c.reshape(b, BAND, DILATIONS[g], 2, A_HEADS, A_HEAD_DIM) for g, c in enumerate(caches)]
    cache_spec = pl.BlockSpec((None, BAND, None, 2, A_HEADS, A_HEAD_DIM),
                              lambda i: (i, 0, 0, 0, 0, 0))
    out = pl.pallas_call(
        _attn_step_kernel,
        grid=(b,),
        in_specs=[pl.BlockSpec((N_GROUPS, A_HEADS, 1), lambda i: (0, 0, 0)),
                  pl.BlockSpec((None, N_HEAD_CHUNKS, A_HEAD_DIM), lambda i: (i, 0, 0)),
                  cache_spec, cache_spec, cache_spec],
        out_specs=pl.BlockSpec((None, A_HEADS, A_HEAD_DIM), lambda i: (i, 0, 0)),
        out_shape=jax.ShapeDtypeStruct((b, A_HEADS, A_HEAD_DIM), F32),
        compiler_params=_params(("parallel",)),
        name="dilated_attn_step",
    )(slopes.reshape(N_GROUPS, A_HEADS, 1), z3, *views)
    return out.reshape(b, A_WIDTH)


def _retention_step_kernel(lg_ref, q_ref, k_ref, v_ref, zr_ref, gret_ref, r_ref, y_ref, rout_ref):
    for hh in range(R_HEADS):
        cols = slice(hh * R_DIM, (hh + 1) * R_DIM)
        gamma = jnp.exp(jnp.full((1, 1), 1.0, F32) * lg_ref[hh])
        q = q_ref[:, cols]
        k = k_ref[:, cols] * (R_DIM ** -0.5)
        v = v_ref[:, cols]
        r_prev = r_ref[hh]
        qb = jnp.broadcast_to(q, (8, R_DIM)).astype(BF16)
        cross = jnp.dot(qb, r_prev.astype(BF16), preferred_element_type=F32)[0:1] * gamma
        inner = jnp.sum(q * k, axis=-1, keepdims=True) * v
        k_col = jnp.broadcast_to(k, (R_DIM, R_DIM)).T
        rout_ref[hh] = r_prev * gamma + k_col * v
        y = inner + cross
        y = y * lax.rsqrt(jnp.mean(y * y, axis=-1, keepdims=True) + EPS)
        y_ref[:, cols] = (y * gret_ref[:, cols]) * _silu(zr_ref[:, cols])


def _retention_step(zs_r, state, log_g, g_ret):
    b = zs_r.shape[0]
    z3 = zs_r.reshape(b, 1, REST_COLS)

    def col_spec(col):
        return pl.BlockSpec((None, 1, R_WIDTH), lambda i, lg: (i, 0, (col - HEAD_COLS) // R_WIDTH))

    state_spec = pl.BlockSpec((None, R_HEADS, R_DIM, R_DIM), lambda i, lg: (i, 0, 0, 0))
    grid_spec = pltpu.PrefetchScalarGridSpec(
        num_scalar_prefetch=1,
        grid=(b,),
        in_specs=[col_spec(COL_QR), col_spec(COL_KR), col_spec(COL_VR), col_spec(COL_ZR),
                  pl.BlockSpec((1, R_WIDTH), lambda i, lg: (0, 0)), state_spec],
        out_specs=[pl.BlockSpec((None, 1, R_WIDTH), lambda i, lg: (i, 0, 0)), state_spec],
    )
    y, new_state = pl.pallas_call(
        _retention_step_kernel, grid_spec=grid_spec,
        out_shape=[jax.ShapeDtypeStruct((b, 1, R_WIDTH), F32),
                   jax.ShapeDtypeStruct(state.shape, F32)],
        compiler_params=_params(("parallel",)),
        name="retention_step",
    )(log_g, z3, z3, z3, z3, g_ret.reshape(1, R_WIDTH), state)
    return y.reshape(b, R_WIDTH), new_state


def _window_new_row_kernel(z_ref, c0, c1, c2, o0, o1, o2):
    del c0, c1, c2
    for g, o_ref in enumerate((o0, o1, o2)):
        o_ref[0, 0] = z_ref[CH_K + g * A_HEADS:CH_K + (g + 1) * A_HEADS, :]
        o_ref[0, 1] = z_ref[CH_V + g * A_HEADS:CH_V + (g + 1) * A_HEADS, :]


def _window_new_row(zs_h, shifted):
    b = zs_h.shape[0]
    tile = (2, A_HEADS, A_HEAD_DIM)
    last_row = [pl.BlockSpec((None, 1) + tile, functools.partial(lambda i, w: (i, w - 1, 0, 0, 0), w=c.shape[1]))
                for c in shifted]
    return pl.pallas_call(
        _window_new_row_kernel,
        grid=(b,),
        in_specs=[pl.BlockSpec((None, N_HEAD_CHUNKS, A_HEAD_DIM), lambda i: (i, 0, 0))] + last_row,
        out_specs=last_row,
        out_shape=[jax.ShapeDtypeStruct(c.shape, c.dtype) for c in shifted],
        input_output_aliases={1: 0, 2: 1, 3: 2},
        compiler_params=_params(("parallel",)),
        name="window_new_row",
    )(zs_h.reshape(b, N_HEAD_CHUNKS, A_HEAD_DIM), *shifted)


def _alibi_slopes():
    n = N_GROUPS * A_HEADS
    return jnp.exp2(-8.0 * (jnp.arange(n, dtype=F32) + 1.0) / n).reshape(N_GROUPS, A_HEADS)


def _retention_log_decay():
    return jnp.log1p(-jnp.exp2(-5.0 - jnp.arange(R_HEADS, dtype=F32)))


def kernel(x_prompt, x_sample, cache_win0, cache_win1, cache_win2, state_ret, p_prompt, p_sample,
           g_pre, w_in, g_ret, w_a_out, w_r_out, w_o, g_post, w_ple_gate, w_ple_proj):
    batch, seq, _ = x_prompt.shape
    dec_batch = x_sample.shape[0]
    assert g_pre.shape[0] == 1 and x_sample.shape[1] == 1
    assert seq % ATT_TILE == 0
    m = batch * seq
    slopes = _alibi_slopes()
    log_g = _retention_log_decay()

    wa, wr, wo = w_a_out[0].astype(BF16), w_r_out[0].astype(BF16), w_o[0].astype(BF16)
    wg, wp = w_ple_gate[0].astype(BF16), w_ple_proj[0].astype(BF16)

    xp = x_prompt.reshape(m, D_MODEL)
    xs = x_sample.reshape(dec_batch, D_MODEL)
    u = _rmsnorm(xp, g_pre[0], TM_NORM)
    us = _rmsnorm(xs, g_pre[0], dec_batch)
    caches = (cache_win0[0], cache_win1[0], cache_win2[0])
    shift = [functools.partial(_shift_rider, c, min(RIDER_ROWS, c.shape[1])) for c in caches]
    n0, n1, n2 = (dec_batch * (c.shape[1] // min(RIDER_ROWS, c.shape[1])) for c in caches)
    zh, zs_h, sh1, sh0 = _in_proj(u, us, w_in[0], 0, HEAD_COLS, F32, TM_IN, TN_IN, True,
                                  (COL_ZA, COL_QR), [(0, shift[1]), (n1, shift[0])])
    layout = [functools.partial(_window_prompt_rider, zh, g, batch, seq, RIDER_ROWS) for g in range(N_GROUPS)]
    np0 = batch * 2 * (min(WINDOWS[0], seq) // min(RIDER_ROWS, WINDOWS[0]))
    zr, zs_r, sh2, wp2, wp1, wp0 = _in_proj(
        u, us, w_in[0], HEAD_COLS, REST_COLS, BF16, TM_IN, TN_IN, False, (COL_ZR, COL_GA),
        [(0, shift[2]), (n2, layout[2]), (np0, layout[1]), (0, layout[0])])
    win_p = [wp.reshape(1, batch, -1, 2, A_HEADS, A_HEAD_DIM) for wp in (wp0, wp1, wp2)]
    ga_blk = (COL_GA - HEAD_COLS) // D_MODEL

    ya = _attention(zh, slopes, batch, seq)
    yr, ret_prompt = _retention(zr, log_g, g_ret[0], batch, seq, RET_ROWS)
    y_prompt = _out_proj(ya, yr, zr, ga_blk, ga_blk + 1, xp, p_prompt[0].reshape(m, PLE_DIM),
                         wa, wr, wo, g_post[0], wg, wp, TM_OUT).reshape(batch, seq, D_MODEL)

    ya_s = _attention_step(zs_h, caches, slopes)
    yr_s, ret_sample = _retention_step(zs_r, state_ret[0], log_g, g_ret[0])
    y_sample = _out_proj(ya_s, yr_s, zs_r, ga_blk, ga_blk + 1, xs, p_sample[0].reshape(dec_batch, PLE_DIM),
                         wa, wr, wo, g_post[0], wg, wp, dec_batch).reshape(dec_batch, 1, D_MODEL)

    win_s = _window_new_row(zs_h, (sh0, sh1, sh2))
    return (y_prompt, y_sample, win_p[0], win_p[1], win_p[2], ret_prompt[None],
            win_s[0][None], win_s[1][None], win_s[2][None], ret_sample[None])
```

```python
import functools

import jax
import jax.numpy as jnp
from jax import lax
from jax.experimental import pallas as pl
from jax.experimental.pallas import tpu as pltpu

F32 = jnp.float32
BF16 = jnp.bfloat16

D_MODEL = 2048
N_GROUPS = 3
DILATIONS = (1, 4, 16)
WINDOWS = (128, 512, 2048)
BAND = 128
A_HEADS = 8
A_HEAD_DIM = 128
A_QKV = N_GROUPS * A_HEADS * A_HEAD_DIM
A_WIDTH = A_HEADS * A_HEAD_DIM
R_HEADS = 8
R_DIM = 256
R_WIDTH = R_HEADS * R_DIM
R_CHUNK = 128
PLE_DIM = 256
EPS = 1e-6
N_IN = 3 * A_QKV + A_WIDTH + 4 * R_WIDTH + 2 * D_MODEL

COL_QA, COL_KA, COL_VA = 0, A_QKV, 2 * A_QKV
COL_ZA = 3 * A_QKV
COL_QR = COL_ZA + A_WIDTH
COL_KR = COL_QR + R_WIDTH
COL_VR = COL_KR + R_WIDTH
COL_ZR = COL_VR + R_WIDTH
COL_GA = COL_ZR + R_WIDTH
COL_GB = COL_GA + D_MODEL

HEAD_COLS = COL_QR
REST_COLS = N_IN - HEAD_COLS
CH_Q, CH_K, CH_V, CH_ZA = (c // A_HEAD_DIM for c in (COL_QA, COL_KA, COL_VA, COL_ZA))
N_HEAD_CHUNKS = HEAD_COLS // A_HEAD_DIM

NEG = -1e30
ATT_TILE = BAND * max(DILATIONS)

assert R_DIM ** -0.5 == 2.0 ** -4

TM_NORM = 512
TM_IN, TN_IN = 1024, 1024
RIDER_ROWS = 256
RET_ROWS = 1024
RET_HEADS = 4
TM_OUT = 256
VMEM_LIMIT = 56 * 1024 * 1024


def _params(semantics, vmem=VMEM_LIMIT):
    return pltpu.CompilerParams(dimension_semantics=semantics, vmem_limit_bytes=vmem)


def _silu(x):
    return x * jax.nn.sigmoid(x)


def _rmsnorm_kernel(x_ref, g_ref, o_ref):
    x = x_ref[...]
    y = x * lax.rsqrt(jnp.mean(x * x, axis=-1, keepdims=True) + EPS)
    o_ref[...] = (y * g_ref[...]).astype(o_ref.dtype)


def _rmsnorm(x, g, tm):
    m, d = x.shape
    return pl.pallas_call(
        _rmsnorm_kernel,
        grid=(m // tm,),
        in_specs=[pl.BlockSpec((tm, d), lambda i: (i, 0)),
                  pl.BlockSpec((1, d), lambda i: (0, 0))],
        out_specs=pl.BlockSpec((tm, d), lambda i: (i, 0)),
        out_shape=jax.ShapeDtypeStruct((m, d), BF16),
        compiler_params=_params(("parallel",)),
        name="rmsnorm",
    )(x, g.reshape(1, d))


def _shift_copy(cur_ref, nxt_ref, out_ref):
    rb = out_ref.shape[0]
    out_ref[0:rb - 1] = cur_ref[1:rb]
    out_ref[rb - 1] = nxt_ref[0]


def _heads_to_rows_copy(in_ref, out_ref):
    for hh in range(A_HEADS):
        out_ref[:, hh, :] = in_ref[hh]


def _in_proj_kernel(*refs, head_major, silu_tiles, riders):
    n_in = sum(r[1] for r in riders)
    u_ref, us_ref, w_ref = refs[:3]
    rider_in = refs[3:3 + n_in]
    o_ref, os_ref = refs[3 + n_in:5 + n_in]
    rider_out = refs[5 + n_in:5 + n_in + len(riders)]
    wb_ref = refs[5 + n_in + len(riders)]
    j = pl.program_id(0)

    @pl.when(pl.program_id(1) == 0)
    def _():
        wb_ref[...] = w_ref[...].astype(BF16)
        os_ref[...] = jnp.dot(us_ref[...], wb_ref[...], preferred_element_type=F32)

    def project(act):
        acc = jnp.dot(u_ref[...], wb_ref[...], preferred_element_type=F32)
        if act is not None:
            acc = act(acc)
        if head_major:
            for c in range(o_ref.shape[0]):
                o_ref[c] = acc[:, c * A_HEAD_DIM:(c + 1) * A_HEAD_DIM].astype(o_ref.dtype)
        else:
            o_ref[...] = acc.astype(o_ref.dtype)

    gated = jnp.logical_and(j >= silu_tiles[0], j < silu_tiles[1])
    pl.when(gated)(functools.partial(project, _silu))
    pl.when(jnp.logical_not(gated))(functools.partial(project, None))

    step = j * pl.num_programs(1) + pl.program_id(1)
    pos = 0
    for (copy, nin, k0, nsteps), out_ref in zip(riders, rider_out):
        active = jnp.logical_and(step >= k0, step < k0 + nsteps)
        pl.when(active)(functools.partial(copy, *rider_in[pos:pos + nin], out_ref))
        pos += nin


def _shift_rider(cache, rb, k0, ni):
    b, w = cache.shape[:2]
    nb = w // rb
    tile = (2, A_HEADS, A_HEAD_DIM)

    def loc(j, i):
        l = jnp.clip(j * ni + i - k0, 0, b * nb - 1)
        return l // nb, l % nb

    def cur_map(j, i):
        bi, si = loc(j, i)
        return bi, si, 0, 0, 0

    def nxt_map(j, i):
        bi, si = loc(j, i)
        return bi, jnp.minimum((si + 1) * rb, w - 1), 0, 0, 0

    return dict(copy=_shift_copy, args=[cache, cache], steps=b * nb,
                in_specs=[pl.BlockSpec((None, rb) + tile, cur_map), pl.BlockSpec((None, 1) + tile, nxt_map)],
                out_spec=pl.BlockSpec((None, rb) + tile, cur_map),
                out_shape=jax.ShapeDtypeStruct(cache.shape, cache.dtype))


def _window_prompt_rider(zh, g, batch, seq, rb, k0, ni):
    keep = min(WINDOWS[g], seq)
    rb = min(rb, keep)
    nsb = keep // rb
    row0 = (seq - keep) // rb

    def loc(j, i):
        l = jnp.clip(j * ni + i - k0, 0, batch * 2 * nsb - 1)
        return l // (2 * nsb), (l // nsb) % 2, l % nsb

    def in_map(j, i):
        b, kv, s = loc(j, i)
        return (CH_K + kv * (CH_V - CH_K)) // A_HEADS + g, b * (seq // rb) + row0 + s, 0

    def out_map(j, i):
        b, kv, s = loc(j, i)
        return b * nsb + s, kv, 0

    return dict(copy=_heads_to_rows_copy, args=[zh], steps=batch * 2 * nsb,
                in_specs=[pl.BlockSpec((A_HEADS, rb, A_HEAD_DIM), in_map)],
                out_spec=pl.BlockSpec((rb, A_HEADS, A_HEAD_DIM), out_map),
                out_shape=jax.ShapeDtypeStruct((batch * keep, 2 * A_HEADS, A_HEAD_DIM), zh.dtype))


def _in_proj(u, us, w, col0, ncols, out_dtype, tm, tn, head_major, silu_cols, make_riders=()):
    m, k = u.shape
    ms = us.shape[0]
    j0 = col0 // tn
    nj, ni = ncols // tn, m // tm
    in_specs = [pl.BlockSpec((tm, k), lambda j, i: (i, 0)),
                pl.BlockSpec((ms, k), lambda j, i: (0, 0)),
                pl.BlockSpec((k, tn), lambda j, i: (0, j0 + j))]
    if head_major:
        o_spec = pl.BlockSpec((tn // A_HEAD_DIM, tm, A_HEAD_DIM), lambda j, i: (j, i, 0))
        o_shape = jax.ShapeDtypeStruct((ncols // A_HEAD_DIM, m, A_HEAD_DIM), out_dtype)
    else:
        o_spec = pl.BlockSpec((tm, tn), lambda j, i: (i, j))
        o_shape = jax.ShapeDtypeStruct((m, ncols), out_dtype)
    out_specs = [o_spec, pl.BlockSpec((ms, tn), lambda j, i: (0, j))]
    out_shape = [o_shape, jax.ShapeDtypeStruct((ms, ncols), F32)]
    riders, rider_args = [], []
    for k0, make in make_riders:
        rider = make(k0, ni)
        in_specs += rider["in_specs"]
        out_specs.append(rider["out_spec"])
        out_shape.append(rider["out_shape"])
        rider_args += rider["args"]
        riders.append((rider["copy"], len(rider["in_specs"]), k0, rider["steps"]))
        assert k0 + rider["steps"] <= nj * ni
    silu_tiles = ((silu_cols[0] - col0) // tn, (silu_cols[1] - col0) // tn)
    return pl.pallas_call(
        functools.partial(_in_proj_kernel, head_major=head_major, silu_tiles=silu_tiles,
                          riders=tuple(riders)),
        grid=(nj, ni),
        in_specs=in_specs,
        out_specs=out_specs,
        out_shape=out_shape,
        scratch_shapes=[pltpu.VMEM((k, tn), BF16)],
        compiler_params=_params(("arbitrary", "arbitrary")),
        name="in_proj_hm" if head_major else "in_proj",
    )(u, us, w, *rider_args)


def _attn_kernel(slope_ref, q0, q1, q2, k0, k1, k2, v0, v1, v2, za_ref, o_ref,
                 qbuf, kbuf, vbuf, onat, lnat, *, seq):
    q_refs, k_refs, v_refs = (q0, q1, q2), (k0, k1, k2), (v0, v1, v2)
    h = pl.program_id(1)
    t = pl.program_id(2)

    @pl.when(t == 0)
    def _():
        for g, dil in enumerate(DILATIONS):
            for r in range(dil):
                pad = pl.ds(r * (seq // dil + BAND), BAND)
                kbuf[g, pad, :] = jnp.zeros((BAND, A_HEAD_DIM), BF16)
                vbuf[g, pad, :] = jnp.zeros((BAND, A_HEAD_DIM), BF16)

    qi = lax.broadcasted_iota(jnp.int32, (BAND, 2 * BAND), 0)
    kj = lax.broadcasted_iota(jnp.int32, (BAND, 2 * BAND), 1)
    dist = qi + BAND - kj
    valid = jnp.logical_and(dist >= 0, dist <= BAND)
    distf = dist.astype(F32)
    first_pen = jnp.where(jnp.logical_and(t == 0, kj < BAND), NEG, 0.0)
    scale = A_HEAD_DIM ** -0.5

    for g in range(N_GROUPS):
        dil = DILATIONS[g]
        n = ATT_TILE // dil
        nblk = n // BAND
        run = seq // dil + BAND
        for r in range(dil):
            rows = pl.ds(r, n, stride=dil) if dil > 1 else pl.ds(0, n)
            dst = pl.ds(pl.multiple_of(r * run + BAND + t * n, BAND), n)
            qbuf[g, r * n:(r + 1) * n, :] = (q_refs[g][rows, :] * scale).astype(BF16)
            kbuf[g, dst, :] = k_refs[g][rows, :].astype(BF16)
            vbuf[g, dst, :] = v_refs[g][rows, :].astype(BF16)

        bias = jnp.where(valid, -(slope_ref[g, h] * float(dil)) * distf, NEG)
        bias_first = bias + first_pen

        for r in range(dil):
            for i in range(nblk):
                u = r * nblk + i
                kv_rows = pl.ds(pl.multiple_of(r * run + i * BAND + t * n, BAND), 2 * BAND)
                q = qbuf[g, u * BAND:(u + 1) * BAND, :]
                s = lax.dot_general(q, kbuf[g, kv_rows, :], (((1,), (1,)), ((), ())),
                                    preferred_element_type=F32)
                s = s + (bias_first if i == 0 else bias)
                m = jnp.max(s, axis=-1, keepdims=True)
                e = jnp.exp(s - m)
                den = jnp.sum(e, axis=-1, keepdims=True)
                acc = jnp.dot(e.astype(BF16), vbuf[g, kv_rows, :], preferred_element_type=F32)
                out_rows = (pl.ds(r + i * (BAND * dil), BAND, stride=dil) if dil > 1
                            else pl.ds(u * BAND, BAND))
                onat[g, out_rows, :] = acc / den
                lnat[g, out_rows, :] = jnp.broadcast_to(m + jnp.log(den), (BAND, A_HEAD_DIM))

    l0, l1, l2 = lnat[0], lnat[1], lnat[2]
    mx = jnp.maximum(jnp.maximum(l0, l1), l2)
    w0, w1, w2 = jnp.exp(l0 - mx), jnp.exp(l1 - mx), jnp.exp(l2 - mx)
    mixed = (w0 * onat[0] + w1 * onat[1] + w2 * onat[2]) / (w0 + w1 + w2)
    o_ref[...] = (mixed * za_ref[...]).astype(o_ref.dtype)


def _attention(zh, slopes, batch, seq):
    nt = seq // ATT_TILE
    m = batch * seq

    def head_spec(c0):
        return pl.BlockSpec((None, ATT_TILE, A_HEAD_DIM), lambda b, h, t, s: (c0 + h, b * nt + t, 0))

    in_specs = [head_spec(c + g * A_HEADS) for c in (CH_Q, CH_K, CH_V) for g in range(N_GROUPS)]
    in_specs.append(head_spec(CH_ZA))
    buf_rows = seq + BAND * max(DILATIONS)
    grid_spec = pltpu.PrefetchScalarGridSpec(
        num_scalar_prefetch=1,
        grid=(batch, A_HEADS, nt),
        in_specs=in_specs,
        out_specs=pl.BlockSpec((ATT_TILE, A_HEAD_DIM), lambda b, h, t, s: (b * nt + t, h)),
        scratch_shapes=[
            pltpu.VMEM((N_GROUPS, ATT_TILE, A_HEAD_DIM), BF16),
            pltpu.VMEM((N_GROUPS, buf_rows, A_HEAD_DIM), BF16),
            pltpu.VMEM((N_GROUPS, buf_rows, A_HEAD_DIM), BF16),
            pltpu.VMEM((N_GROUPS, ATT_TILE, A_HEAD_DIM), F32),
            pltpu.VMEM((N_GROUPS, ATT_TILE, A_HEAD_DIM), F32),
        ],
    )
    return pl.pallas_call(
        functools.partial(_attn_kernel, seq=seq), grid_spec=grid_spec,
        out_shape=jax.ShapeDtypeStruct((m, A_WIDTH), BF16),
        compiler_params=_params(("parallel", "parallel", "arbitrary")),
        name="dilated_attn",
    )(slopes, *([zh] * 10))


def _retention_kernel(lg_ref, q_ref, k_ref, v_ref, zr_ref, gret_ref, y_ref, rout_ref, r_scr):
    s = pl.program_id(2)
    heads = r_scr.shape[0]

    @pl.when(s == 0)
    def _():
        r_scr[...] = jnp.zeros_like(r_scr)

    L = R_CHUNK
    pi = lax.broadcasted_iota(jnp.int32, (L, L), 0)
    pj = lax.broadcasted_iota(jnp.int32, (L, L), 1)
    rel = (pi - pj).astype(F32)
    pos = lax.broadcasted_iota(jnp.int32, (L, 1), 0).astype(F32)
    k_scale = R_DIM ** -0.5
    decays = []
    for hp in range(heads):
        log_g = lg_ref[pl.program_id(1) * heads + hp]
        decays.append((
            jnp.where(rel >= 0, jnp.exp(jnp.maximum(rel, 0.0) * log_g), 0.0) * k_scale,
            jnp.exp((pos + 1.0) * log_g),
            jnp.exp((L - 1.0 - pos) * log_g) * k_scale,
            jnp.exp(jnp.full((1, 1), float(L), F32) * log_g)))

    def chunk(c, carry):
        rows = pl.ds(pl.multiple_of(c * L, L), L)
        for hp, (intra, q_decay, k_decay, chunk_decay) in enumerate(decays):
            cols = slice(hp * R_DIM, (hp + 1) * R_DIM)
            q = q_ref[rows, cols]
            k = k_ref[rows, cols]
            v = v_ref[rows, cols]
            r_prev = r_scr[hp]
            scores = lax.dot_general(q, k, (((1,), (1,)), ((), ())), preferred_element_type=F32) * intra
            inner = jnp.dot(scores.astype(BF16), v, preferred_element_type=F32)
            cross = jnp.dot(q, r_prev.astype(BF16), preferred_element_type=F32) * q_decay
            kd = (k.astype(F32) * k_decay).astype(BF16)
            r_scr[hp] = r_prev * chunk_decay + lax.dot_general(
                kd, v, (((0,), (0,)), ((), ())), preferred_element_type=F32)
            y = inner + cross
            y = y * lax.rsqrt(jnp.mean(y * y, axis=-1, keepdims=True) + EPS)
            y = (y * gret_ref[:, cols]) * zr_ref[rows, cols].astype(F32)
            y_ref[rows, cols] = y.astype(y_ref.dtype)
        return carry

    lax.fori_loop(0, q_ref.shape[0] // L, chunk, 0, unroll=True)

    @pl.when(s == pl.num_programs(2) - 1)
    def _():
        rout_ref[...] = r_scr[...]


def _retention(zr_all, log_g, g_ret, batch, seq, rb, heads):
    ns = seq // rb
    m = batch * seq
    width = heads * R_DIM

    def col_spec(col):
        c0 = (col - HEAD_COLS) // width
        return pl.BlockSpec((rb, width), lambda b, h, s, lg: (b * ns + s, c0 + h))

    grid_spec = pltpu.PrefetchScalarGridSpec(
        num_scalar_prefetch=1,
        grid=(batch, R_HEADS // heads, ns),
        in_specs=[col_spec(COL_QR), col_spec(COL_KR), col_spec(COL_VR), col_spec(COL_ZR),
                  pl.BlockSpec((1, width), lambda b, h, s, lg: (0, h))],
        out_specs=[pl.BlockSpec((rb, width), lambda b, h, s, lg: (b * ns + s, h)),
                   pl.BlockSpec((None, heads, R_DIM, R_DIM), lambda b, h, s, lg: (b, h, 0, 0))],
        scratch_shapes=[pltpu.VMEM((heads, R_DIM, R_DIM), F32)],
    )
    return pl.pallas_call(
        _retention_kernel, grid_spec=grid_spec,
        out_shape=[jax.ShapeDtypeStruct((m, R_WIDTH), BF16),
                   jax.ShapeDtypeStruct((batch, R_HEADS, R_DIM, R_DIM), F32)],
        compiler_params=_params(("parallel", "parallel", "arbitrary")),
        name="retention",
    )(log_g, zr_all, zr_all, zr_all, zr_all, g_ret.reshape(1, R_WIDTH))


def _out_kernel(ya_ref, yr_ref, ga_ref, gb_ref, x_ref, p_ref,
                wa_ref, wr_ref, wo_ref, gpost_ref, wg_ref, wp_ref, o_ref):
    a = jnp.dot(ya_ref[...].astype(BF16), wa_ref[...], preferred_element_type=F32)
    b = jnp.dot(yr_ref[...].astype(BF16), wr_ref[...], preferred_element_type=F32)
    merged = (jax.nn.sigmoid(ga_ref[...].astype(F32)) * a
              + jax.nn.sigmoid(gb_ref[...].astype(F32)) * b)
    y = jnp.dot(merged.astype(BF16), wo_ref[...], preferred_element_type=F32)
    y = y * lax.rsqrt(jnp.mean(y * y, axis=-1, keepdims=True) + EPS)
    hres = x_ref[...] + y * gpost_ref[...]
    gate = jax.nn.sigmoid(jnp.dot(hres.astype(BF16), wg_ref[...], preferred_element_type=F32))
    emb = jnp.dot(p_ref[...].astype(BF16), wp_ref[...], preferred_element_type=F32)
    o_ref[...] = hres + gate * emb


def _out_proj(ya, yr, gates, ga_blk, gb_blk, x, p, wa, wr, wo, g_post, wg, wp, tm):
    m = x.shape[0]

    def rows(width, cb=0):
        return pl.BlockSpec((tm, width), lambda i: (i, cb))

    def whole(arr):
        return pl.BlockSpec(arr.shape, lambda i: (0, 0), pipeline_mode=pl.Buffered(1))

    gp = g_post.reshape(1, D_MODEL)
    return pl.pallas_call(
        _out_kernel,
        grid=(m // tm,),
        in_specs=[rows(A_WIDTH), rows(R_WIDTH), rows(D_MODEL, ga_blk), rows(D_MODEL, gb_blk),
                  rows(D_MODEL), rows(PLE_DIM),
                  whole(wa), whole(wr), whole(wo), whole(gp), whole(wg), whole(wp)],
        out_specs=rows(D_MODEL),
        out_shape=jax.ShapeDtypeStruct((m, D_MODEL), F32),
        compiler_params=_params(("parallel",)),
        name="out_proj",
    )(ya, yr, gates, gates, x, p, wa, wr, wo, gp, wg, wp)


def _attn_step_kernel(slope_ref, z_ref, c0_ref, c1_ref, c2_ref, o_ref):
    c_refs = (c0_ref, c1_ref, c2_ref)
    scale = A_HEAD_DIM ** -0.5
    steps = float(BAND) - lax.broadcasted_iota(jnp.int32, (BAND, 1, 1), 0).astype(F32)
    outs, lses = [], []
    for g in range(N_GROUPS):
        q = z_ref[CH_Q + g * A_HEADS:CH_Q + (g + 1) * A_HEADS, :] * scale
        kn = z_ref[CH_K + g * A_HEADS:CH_K + (g + 1) * A_HEADS, :]
        vn = z_ref[CH_V + g * A_HEADS:CH_V + (g + 1) * A_HEADS, :]
        kc = c_refs[g][:, 0]
        vc = c_refs[g][:, 1]
        slope = slope_ref[g][None]
        s = jnp.sum(kc * q[None], axis=-1, keepdims=True) - (slope * float(DILATIONS[g])) * steps
        s_new = jnp.sum(kn * q, axis=-1, keepdims=True)
        m = jnp.maximum(jnp.max(s, axis=0), s_new)
        e = jnp.exp(s - m[None])
        e_new = jnp.exp(s_new - m)
        den = jnp.sum(e, axis=0) + e_new
        outs.append((jnp.sum(e * vc, axis=0) + e_new * vn) / den)
        lses.append(m + jnp.log(den))
    mx = jnp.maximum(jnp.maximum(lses[0], lses[1]), lses[2])
    ws = [jnp.exp(l - mx) for l in lses]
    mixed = (ws[0] * outs[0] + ws[1] * outs[1] + ws[2] * outs[2]) / (ws[0] + ws[1] + ws[2])
    o_ref[...] = mixed * _silu(z_ref[CH_ZA:CH_ZA + A_HEADS, :])


def _attention_step(zs_h, caches, slopes):
    b = zs_h.shape[0]
    z3 = zs_h.reshape(b, N_HEAD_CHUNKS, A_HEAD_DIM)
    views = [c.reshape(b, BAND, DILATIONS[g], 2, A_HEADS, A_HEAD_DIM) for g, c in enumerate(caches)]
    cache_spec = pl.BlockSpec((None, BAND, None, 2, A_HEADS, A_HEAD_DIM),
                              lambda i: (i, 0, 0, 0, 0, 0))
    out = pl.pallas_call(
        _attn_step_kernel,
        grid=(b,),
        in_specs=[pl.BlockSpec((N_GROUPS, A_HEADS, 1), lambda i: (0, 0, 0)),
                  pl.BlockSpec((None, N_HEAD_CHUNKS, A_HEAD_DIM), lambda i: (i, 0, 0)),
                  cache_spec, cache_spec, cache_spec],
        out_specs=pl.BlockSpec((None, A_HEADS, A_HEAD_DIM), lambda i: (i, 0, 0)),
        out_shape=jax.ShapeDtypeStruct((b, A_HEADS, A_HEAD_DIM), F32),
        compiler_params=_params(("parallel",)),
        name="dilated_attn_step",
    )(slopes.reshape(N_GROUPS, A_HEADS, 1), z3, *views)
    return out.reshape(b, A_WIDTH)


def _retention_step_kernel(lg_ref, q_ref, k_ref, v_ref, zr_ref, gret_ref, r_ref, y_ref, rout_ref):
    for hh in range(R_HEADS):
        cols = slice(hh * R_DIM, (hh + 1) * R_DIM)
        gamma = jnp.exp(jnp.full((1, 1), 1.0, F32) * lg_ref[hh])
        q = q_ref[:, cols]
        k = k_ref[:, cols] * (R_DIM ** -0.5)
        v = v_ref[:, cols]
        r_prev = r_ref[hh]
        qb = jnp.broadcast_to(q, (8, R_DIM)).astype(BF16)
        cross = jnp.dot(qb, r_prev.astype(BF16), preferred_element_type=F32)[0:1] * gamma
        inner = jnp.sum(q * k, axis=-1, keepdims=True) * v
        k_col = jnp.broadcast_to(k, (R_DIM, R_DIM)).T
        rout_ref[hh] = r_prev * gamma + k_col * v
        y = inner + cross
        y = y * lax.rsqrt(jnp.mean(y * y, axis=-1, keepdims=True) + EPS)
        y_ref[:, cols] = (y * gret_ref[:, cols]) * _silu(zr_ref[:, cols])


def _retention_step(zs_r, state, log_g, g_ret):
    b = zs_r.shape[0]
    z3 = zs_r.reshape(b, 1, REST_COLS)

    def col_spec(col):
        return pl.BlockSpec((None, 1, R_WIDTH), lambda i, lg: (i, 0, (col - HEAD_COLS) // R_WIDTH))

    state_spec = pl.BlockSpec((None, R_HEADS, R_DIM, R_DIM), lambda i, lg: (i, 0, 0, 0))
    grid_spec = pltpu.PrefetchScalarGridSpec(
        num_scalar_prefetch=1,
        grid=(b,),
        in_specs=[col_spec(COL_QR), col_spec(COL_KR), col_spec(COL_VR), col_spec(COL_ZR),
                  pl.BlockSpec((1, R_WIDTH), lambda i, lg: (0, 0)), state_spec],
        out_specs=[pl.BlockSpec((None, 1, R_WIDTH), lambda i, lg: (i, 0, 0)), state_spec],
    )
    y, new_state = pl.pallas_call(
        _retention_step_kernel, grid_spec=grid_spec,
        out_shape=[jax.ShapeDtypeStruct((b, 1, R_WIDTH), F32),
                   jax.ShapeDtypeStruct(state.shape, F32)],
        compiler_params=_params(("parallel",)),
        name="retention_step",
    )(log_g, z3, z3, z3, z3, g_ret.reshape(1, R_WIDTH), state)
    return y.reshape(b, R_WIDTH), new_state


def _window_new_row_kernel(z_ref, c0, c1, c2, o0, o1, o2):
    del c0, c1, c2
    for g, o_ref in enumerate((o0, o1, o2)):
        o_ref[0, 0] = z_ref[CH_K + g * A_HEADS:CH_K + (g + 1) * A_HEADS, :]
        o_ref[0, 1] = z_ref[CH_V + g * A_HEADS:CH_V + (g + 1) * A_HEADS, :]


def _window_new_row(zs_h, shifted):
    b = zs_h.shape[0]
    tile = (2, A_HEADS, A_HEAD_DIM)
    last_row = [pl.BlockSpec((None, 1) + tile, functools.partial(lambda i, w: (i, w - 1, 0, 0, 0), w=c.shape[1]))
                for c in shifted]
    return pl.pallas_call(
        _window_new_row_kernel,
        grid=(b,),
        in_specs=[pl.BlockSpec((None, N_HEAD_CHUNKS, A_HEAD_DIM), lambda i: (i, 0, 0))] + last_row,
        out_specs=last_row,
        out_shape=[jax.ShapeDtypeStruct(c.shape, c.dtype) for c in shifted],
        input_output_aliases={1: 0, 2: 1, 3: 2},
        compiler_params=_params(("parallel",)),
        name="window_new_row",
    )(zs_h.reshape(b, N_HEAD_CHUNKS, A_HEAD_DIM), *shifted)


def _alibi_slopes():
    n = N_GROUPS * A_HEADS
    return jnp.exp2(-8.0 * (jnp.arange(n, dtype=F32) + 1.0) / n).reshape(N_GROUPS, A_HEADS)


def _retention_log_decay():
    return jnp.log1p(-jnp.exp2(-5.0 - jnp.arange(R_HEADS, dtype=F32)))


def kernel(x_prompt, x_sample, cache_win0, cache_win1, cache_win2, state_ret, p_prompt, p_sample,
           g_pre, w_in, g_ret, w_a_out, w_r_out, w_o, g_post, w_ple_gate, w_ple_proj):
    batch, seq, _ = x_prompt.shape
    dec_batch = x_sample.shape[0]
    assert g_pre.shape[0] == 1 and x_sample.shape[1] == 1
    assert seq % ATT_TILE == 0
    m = batch * seq
    slopes = _alibi_slopes()
    log_g = _retention_log_decay()

    wa, wr, wo = w_a_out[0].astype(BF16), w_r_out[0].astype(BF16), w_o[0].astype(BF16)
    wg, wp = w_ple_gate[0].astype(BF16), w_ple_proj[0].astype(BF16)

    xp = x_prompt.reshape(m, D_MODEL)
    xs = x_sample.reshape(dec_batch, D_MODEL)
    u = _rmsnorm(xp, g_pre[0], TM_NORM)
    us = _rmsnorm(xs, g_pre[0], dec_batch)
    caches = (cache_win0[0], cache_win1[0], cache_win2[0])
    shift = [functools.partial(_shift_rider, c, min(RIDER_ROWS, c.shape[1])) for c in caches]
    n0, n1, n2 = (dec_batch * (c.shape[1] // min(RIDER_ROWS, c.shape[1])) for c in caches)
    zh, zs_h, sh1, sh0 = _in_proj(u, us, w_in[0], 0, HEAD_COLS, F32, TM_IN, TN_IN, True,
                                  (COL_ZA, COL_QR), [(0, shift[1]), (n1, shift[0])])
    layout = [functools.partial(_window_prompt_rider, zh, g, batch, seq, RIDER_ROWS) for g in range(N_GROUPS)]
    np0 = batch * 2 * (min(WINDOWS[0], seq) // min(RIDER_ROWS, WINDOWS[0]))
    zr, zs_r, sh2, wp2, wp1, wp0 = _in_proj(
        u, us, w_in[0], HEAD_COLS, REST_COLS, BF16, TM_IN, TN_IN, False, (COL_ZR, COL_GA),
        [(0, shift[2]), (n2, layout[2]), (np0, layout[1]), (0, layout[0])])
    win_p = [wp.reshape(1, batch, -1, 2, A_HEADS, A_HEAD_DIM) for wp in (wp0, wp1, wp2)]
    ga_blk = (COL_GA - HEAD_COLS) // D_MODEL

    ya = _attention(zh, slopes, batch, seq)
    yr, ret_prompt = _retention(zr, log_g, g_ret[0], batch, seq, RET_ROWS, RET_HEADS)
    y_prompt = _out_proj(ya, yr, zr, ga_blk, ga_blk + 1, xp, p_prompt[0].reshape(m, PLE_DIM),
                         wa, wr, wo, g_post[0], wg, wp, TM_OUT).reshape(batch, seq, D_MODEL)

    ya_s = _attention_step(zs_h, caches, slopes)
    yr_s, ret_sample = _retention_step(zs_r, state_ret[0], log_g, g_ret[0])
    y_sample = _out_proj(ya_s, yr_s, zs_r, ga_blk, ga_blk + 1, xs, p_sample[0].reshape(dec_batch, PLE_DIM),
                         wa, wr, wo, g_post[0], wg, wp, dec_batch).reshape(dec_batch, 1, D_MODEL)

    win_s = _window_new_row(zs_h, (sh0, sh1, sh2))
    return (y_prompt, y_sample, win_p[0], win_p[1], win_p[2], ret_prompt[None],
            win_s[0][None], win_s[1][None], win_s[2][None], ret_sample[None])
```

```python
import functools

import jax
import jax.numpy as jnp
from jax import lax
from jax.experimental import pallas as pl
from jax.experimental.pallas import tpu as pltpu

F32 = jnp.float32
BF16 = jnp.bfloat16

D_MODEL = 2048
N_GROUPS = 3
DILATIONS = (1, 4, 16)
WINDOWS = (128, 512, 2048)
BAND = 128
A_HEADS = 8
A_HEAD_DIM = 128
A_QKV = N_GROUPS * A_HEADS * A_HEAD_DIM
A_WIDTH = A_HEADS * A_HEAD_DIM
R_HEADS = 8
R_DIM = 256
R_WIDTH = R_HEADS * R_DIM
R_CHUNK = 128
PLE_DIM = 256
EPS = 1e-6
N_IN = 3 * A_QKV + A_WIDTH + 4 * R_WIDTH + 2 * D_MODEL

COL_QA, COL_KA, COL_VA = 0, A_QKV, 2 * A_QKV
COL_ZA = 3 * A_QKV
COL_QR = COL_ZA + A_WIDTH
COL_KR = COL_QR + R_WIDTH
COL_VR = COL_KR + R_WIDTH
COL_ZR = COL_VR + R_WIDTH
COL_GA = COL_ZR + R_WIDTH
COL_GB = COL_GA + D_MODEL

HEAD_COLS = COL_QR
REST_COLS = N_IN - HEAD_COLS
CH_Q, CH_K, CH_V, CH_ZA = (c // A_HEAD_DIM for c in (COL_QA, COL_KA, COL_VA, COL_ZA))
N_HEAD_CHUNKS = HEAD_COLS // A_HEAD_DIM

NEG = -1e30
ATT_TILE = BAND * max(DILATIONS)

assert R_DIM ** -0.5 == 2.0 ** -4

TM_NORM = 512
TM_IN, TN_IN = 1024, 1024
RIDER_ROWS = 256
RET_ROWS = 1024
RET_HEADS = 4
TM_OUT = 256
VMEM_LIMIT = 60 * 1024 * 1024


def _params(semantics, vmem=VMEM_LIMIT):
    return pltpu.CompilerParams(dimension_semantics=semantics, vmem_limit_bytes=vmem)


def _silu(x):
    return x * jax.nn.sigmoid(x)


def _rmsnorm_kernel(x_ref, g_ref, o_ref):
    x = x_ref[...]
    y = x * lax.rsqrt(jnp.mean(x * x, axis=-1, keepdims=True) + EPS)
    o_ref[...] = (y * g_ref[...]).astype(o_ref.dtype)


def _rmsnorm(x, g, tm):
    m, d = x.shape
    return pl.pallas_call(
        _rmsnorm_kernel,
        grid=(m // tm,),
        in_specs=[pl.BlockSpec((tm, d), lambda i: (i, 0)),
                  pl.BlockSpec((1, d), lambda i: (0, 0))],
        out_specs=pl.BlockSpec((tm, d), lambda i: (i, 0)),
        out_shape=jax.ShapeDtypeStruct((m, d), BF16),
        compiler_params=_params(("parallel",)),
        name="rmsnorm",
    )(x, g.reshape(1, d))


def _shift_copy(cur_ref, nxt_ref, out_ref):
    rb = out_ref.shape[0]
    out_ref[0:rb - 1] = cur_ref[1:rb]
    out_ref[rb - 1] = nxt_ref[0]


def _heads_to_rows_copy(in_ref, out_ref):
    for hh in range(A_HEADS):
        out_ref[:, hh, :] = in_ref[hh]


def _in_proj_kernel(*refs, head_major, silu_tiles, riders):
    n_in = sum(r[1] for r in riders)
    u_ref, us_ref, w_ref = refs[:3]
    rider_in = refs[3:3 + n_in]
    o_ref, os_ref = refs[3 + n_in:5 + n_in]
    rider_out = refs[5 + n_in:5 + n_in + len(riders)]
    wb_ref = refs[5 + n_in + len(riders)]
    c = pl.program_id(2)
    j = 2 * pl.program_id(0) + c

    @pl.when(pl.program_id(1) == 0)
    def _():
        wb_ref[c] = w_ref[...].astype(BF16)
        os_ref[...] = jnp.dot(us_ref[...], wb_ref[c], preferred_element_type=F32)

    def project(act):
        acc = jnp.dot(u_ref[...], wb_ref[c], preferred_element_type=F32)
        if act is not None:
            acc = act(acc)
        if head_major:
            for ch in range(o_ref.shape[0]):
                o_ref[ch] = acc[:, ch * A_HEAD_DIM:(ch + 1) * A_HEAD_DIM].astype(o_ref.dtype)
        else:
            o_ref[...] = acc.astype(o_ref.dtype)

    gated = jnp.logical_and(j >= silu_tiles[0], j < silu_tiles[1])
    pl.when(gated)(functools.partial(project, _silu))
    pl.when(jnp.logical_not(gated))(functools.partial(project, None))

    step = (pl.program_id(0) * pl.num_programs(1) + pl.program_id(1)) * 2 + c
    pos = 0
    for (copy, nin, k0, nsteps), out_ref in zip(riders, rider_out):
        active = jnp.logical_and(step >= k0, step < k0 + nsteps)
        pl.when(active)(functools.partial(copy, *rider_in[pos:pos + nin], out_ref))
        pos += nin


def _shift_rider(cache, rb, k0, flat_step):
    b, w = cache.shape[:2]
    nb = w // rb
    tile = (2, A_HEADS, A_HEAD_DIM)

    def loc(*idx):
        l = jnp.clip(flat_step(*idx) - k0, 0, b * nb - 1)
        return l // nb, l % nb

    def cur_map(*idx):
        bi, si = loc(*idx)
        return bi, si, 0, 0, 0

    def nxt_map(*idx):
        bi, si = loc(*idx)
        return bi, jnp.minimum((si + 1) * rb, w - 1), 0, 0, 0

    return dict(copy=_shift_copy, args=[cache, cache], steps=b * nb,
                in_specs=[pl.BlockSpec((None, rb) + tile, cur_map), pl.BlockSpec((None, 1) + tile, nxt_map)],
                out_spec=pl.BlockSpec((None, rb) + tile, cur_map),
                out_shape=jax.ShapeDtypeStruct(cache.shape, cache.dtype))


def _window_prompt_rider(zh, g, batch, seq, rb, k0, flat_step):
    keep = min(WINDOWS[g], seq)
    rb = min(rb, keep)
    nsb = keep // rb
    row0 = (seq - keep) // rb

    def loc(*idx):
        l = jnp.clip(flat_step(*idx) - k0, 0, batch * 2 * nsb - 1)
        return l // (2 * nsb), (l // nsb) % 2, l % nsb

    def in_map(*idx):
        b, kv, s = loc(*idx)
        return (CH_K + kv * (CH_V - CH_K)) // A_HEADS + g, b * (seq // rb) + row0 + s, 0

    def out_map(*idx):
        b, kv, s = loc(*idx)
        return b * nsb + s, kv, 0

    return dict(copy=_heads_to_rows_copy, args=[zh], steps=batch * 2 * nsb,
                in_specs=[pl.BlockSpec((A_HEADS, rb, A_HEAD_DIM), in_map)],
                out_spec=pl.BlockSpec((rb, A_HEADS, A_HEAD_DIM), out_map),
                out_shape=jax.ShapeDtypeStruct((batch * keep, 2 * A_HEADS, A_HEAD_DIM), zh.dtype))


def _in_proj(u, us, w, col0, ncols, out_dtype, tm, tn, head_major, silu_cols, make_riders=()):
    m, k = u.shape
    ms = us.shape[0]
    j0 = col0 // tn
    nj, ni = ncols // tn, m // tm
    assert nj % 2 == 0

    def flat_step(p, i, c):
        return (p * ni + i) * 2 + c

    def sticky_tile(p, i, c):
        return 2 * p + jnp.where(i == 0, c, 1)

    in_specs = [pl.BlockSpec((tm, k), lambda p, i, c: (i, 0)),
                pl.BlockSpec((ms, k), lambda p, i, c: (0, 0)),
                pl.BlockSpec((k, tn), lambda p, i, c: (0, j0 + sticky_tile(p, i, c)))]
    if head_major:
        o_spec = pl.BlockSpec((tn // A_HEAD_DIM, tm, A_HEAD_DIM), lambda p, i, c: (2 * p + c, i, 0))
        o_shape = jax.ShapeDtypeStruct((ncols // A_HEAD_DIM, m, A_HEAD_DIM), out_dtype)
    else:
        o_spec = pl.BlockSpec((tm, tn), lambda p, i, c: (i, 2 * p + c))
        o_shape = jax.ShapeDtypeStruct((m, ncols), out_dtype)
    out_specs = [o_spec, pl.BlockSpec((ms, tn), lambda p, i, c: (0, sticky_tile(p, i, c)))]
    out_shape = [o_shape, jax.ShapeDtypeStruct((ms, ncols), F32)]
    riders, rider_args = [], []
    for k0, make in make_riders:
        rider = make(k0, flat_step)
        in_specs += rider["in_specs"]
        out_specs.append(rider["out_spec"])
        out_shape.append(rider["out_shape"])
        rider_args += rider["args"]
        riders.append((rider["copy"], len(rider["in_specs"]), k0, rider["steps"]))
        assert k0 + rider["steps"] <= nj * ni
    silu_tiles = ((silu_cols[0] - col0) // tn, (silu_cols[1] - col0) // tn)
    return pl.pallas_call(
        functools.partial(_in_proj_kernel, head_major=head_major, silu_tiles=silu_tiles,
                          riders=tuple(riders)),
        grid=(nj // 2, ni, 2),
        in_specs=in_specs,
        out_specs=out_specs,
        out_shape=out_shape,
        scratch_shapes=[pltpu.VMEM((2, k, tn), BF16)],
        compiler_params=_params(("arbitrary", "arbitrary", "arbitrary")),
        name="in_proj_hm" if head_major else "in_proj",
    )(u, us, w, *rider_args)


def _attn_kernel(slope_ref, q0, q1, q2, k0, k1, k2, v0, v1, v2, za_ref, o_ref,
                 qbuf, kbuf, vbuf, onat, lnat, *, seq):
    q_refs, k_refs, v_refs = (q0, q1, q2), (k0, k1, k2), (v0, v1, v2)
    h = pl.program_id(1)
    t = pl.program_id(2)

    @pl.when(t == 0)
    def _():
        for g, dil in enumerate(DILATIONS):
            for r in range(dil):
                pad = pl.ds(r * (seq // dil + BAND), BAND)
                kbuf[g, pad, :] = jnp.zeros((BAND, A_HEAD_DIM), BF16)
                vbuf[g, pad, :] = jnp.zeros((BAND, A_HEAD_DIM), BF16)

    qi = lax.broadcasted_iota(jnp.int32, (BAND, 2 * BAND), 0)
    kj = lax.broadcasted_iota(jnp.int32, (BAND, 2 * BAND), 1)
    dist = qi + BAND - kj
    valid = jnp.logical_and(dist >= 0, dist <= BAND)
    distf = dist.astype(F32)
    first_pen = jnp.where(jnp.logical_and(t == 0, kj < BAND), NEG, 0.0)
    scale = A_HEAD_DIM ** -0.5

    for g in range(N_GROUPS):
        dil = DILATIONS[g]
        n = ATT_TILE // dil
        nblk = n // BAND
        run = seq // dil + BAND
        for r in range(dil):
            rows = pl.ds(r, n, stride=dil) if dil > 1 else pl.ds(0, n)
            dst = pl.ds(pl.multiple_of(r * run + BAND + t * n, BAND), n)
            qbuf[g, r * n:(r + 1) * n, :] = (q_refs[g][rows, :] * scale).astype(BF16)
            kbuf[g, dst, :] = k_refs[g][rows, :].astype(BF16)
            vbuf[g, dst, :] = v_refs[g][rows, :].astype(BF16)

        bias = jnp.where(valid, -(slope_ref[g, h] * float(dil)) * distf, NEG)
        bias_first = bias + first_pen

        for r in range(dil):
            for i in range(nblk):
                u = r * nblk + i
                kv_rows = pl.ds(pl.multiple_of(r * run + i * BAND + t * n, BAND), 2 * BAND)
                q = qbuf[g, u * BAND:(u + 1) * BAND, :]
                s = lax.dot_general(q, kbuf[g, kv_rows, :], (((1,), (1,)), ((), ())),
                                    preferred_element_type=F32)
                s = s + (bias_first if i == 0 else bias)
                m = jnp.max(s, axis=-1, keepdims=True)
                e = jnp.exp(s - m)
                den = jnp.sum(e, axis=-1, keepdims=True)
                acc = jnp.dot(e.astype(BF16), vbuf[g, kv_rows, :], preferred_element_type=F32)
                out_rows = (pl.ds(r + i * (BAND * dil), BAND, stride=dil) if dil > 1
                            else pl.ds(u * BAND, BAND))
                onat[g, out_rows, :] = acc / den
                lnat[g, out_rows, :] = jnp.broadcast_to(m + jnp.log(den), (BAND, A_HEAD_DIM))

    l0, l1, l2 = lnat[0], lnat[1], lnat[2]
    mx = jnp.maximum(jnp.maximum(l0, l1), l2)
    w0, w1, w2 = jnp.exp(l0 - mx), jnp.exp(l1 - mx), jnp.exp(l2 - mx)
    mixed = (w0 * onat[0] + w1 * onat[1] + w2 * onat[2]) / (w0 + w1 + w2)
    o_ref[...] = (mixed * za_ref[...]).astype(o_ref.dtype)


def _attention(zh, slopes, batch, seq):
    nt = seq // ATT_TILE
    m = batch * seq

    def head_spec(c0):
        return pl.BlockSpec((None, ATT_TILE, A_HEAD_DIM), lambda b, h, t, s: (c0 + h, b * nt + t, 0))

    in_specs = [head_spec(c + g * A_HEADS) for c in (CH_Q, CH_K, CH_V) for g in range(N_GROUPS)]
    in_specs.append(head_spec(CH_ZA))
    buf_rows = seq + BAND * max(DILATIONS)
    grid_spec = pltpu.PrefetchScalarGridSpec(
        num_scalar_prefetch=1,
        grid=(batch, A_HEADS, nt),
        in_specs=in_specs,
        out_specs=pl.BlockSpec((ATT_TILE, A_HEAD_DIM), lambda b, h, t, s: (b * nt + t, h)),
        scratch_shapes=[
            pltpu.VMEM((N_GROUPS, ATT_TILE, A_HEAD_DIM), BF16),
            pltpu.VMEM((N_GROUPS, buf_rows, A_HEAD_DIM), BF16),
            pltpu.VMEM((N_GROUPS, buf_rows, A_HEAD_DIM), BF16),
            pltpu.VMEM((N_GROUPS, ATT_TILE, A_HEAD_DIM), F32),
            pltpu.VMEM((N_GROUPS, ATT_TILE, A_HEAD_DIM), F32),
        ],
    )
    return pl.pallas_call(
        functools.partial(_attn_kernel, seq=seq), grid_spec=grid_spec,
        out_shape=jax.ShapeDtypeStruct((m, A_WIDTH), BF16),
        compiler_params=_params(("parallel", "parallel", "arbitrary")),
        name="dilated_attn",
    )(slopes, *([zh] * 10))


def _retention_kernel(lg_ref, q_ref, k_ref, v_ref, zr_ref, gret_ref, y_ref, rout_ref, r_scr):
    s = pl.program_id(2)
    heads = r_scr.shape[0]

    @pl.when(s == 0)
    def _():
        r_scr[...] = jnp.zeros_like(r_scr)

    L = R_CHUNK
    pi = lax.broadcasted_iota(jnp.int32, (L, L), 0)
    pj = lax.broadcasted_iota(jnp.int32, (L, L), 1)
    rel = (pi - pj).astype(F32)
    pos = lax.broadcasted_iota(jnp.int32, (L, 1), 0).astype(F32)
    k_scale = R_DIM ** -0.5
    decays = []
    for hp in range(heads):
        log_g = lg_ref[pl.program_id(1) * heads + hp]
        decays.append((
            jnp.where(rel >= 0, jnp.exp(jnp.maximum(rel, 0.0) * log_g), 0.0) * k_scale,
            jnp.exp((pos + 1.0) * log_g),
            jnp.exp((L - 1.0 - pos) * log_g) * k_scale,
            jnp.exp(jnp.full((1, 1), float(L), F32) * log_g)))

    def chunk(c, carry):
        rows = pl.ds(pl.multiple_of(c * L, L), L)
        for hp, (intra, q_decay, k_decay, chunk_decay) in enumerate(decays):
            cols = slice(hp * R_DIM, (hp + 1) * R_DIM)
            q = q_ref[rows, cols]
            k = k_ref[rows, cols]
            v = v_ref[rows, cols]
            r_prev = r_scr[hp]
            scores = lax.dot_general(q, k, (((1,), (1,)), ((), ())), preferred_element_type=F32) * intra
            inner = jnp.dot(scores.astype(BF16), v, preferred_element_type=F32)
            cross = jnp.dot(q, r_prev.astype(BF16), preferred_element_type=F32) * q_decay
            kd = (k.astype(F32) * k_decay).astype(BF16)
            r_scr[hp] = r_prev * chunk_decay + lax.dot_general(
                kd, v, (((0,), (0,)), ((), ())), preferred_element_type=F32)
            y = inner + cross
            y = y * lax.rsqrt(jnp.mean(y * y, axis=-1, keepdims=True) + EPS)
            y = (y * gret_ref[:, cols]) * zr_ref[rows, cols].astype(F32)
            y_ref[rows, cols] = y.astype(y_ref.dtype)
        return carry

    lax.fori_loop(0, q_ref.shape[0] // L, chunk, 0, unroll=True)

    @pl.when(s == pl.num_programs(2) - 1)
    def _():
        rout_ref[...] = r_scr[...]


def _retention(zr_all, log_g, g_ret, batch, seq, rb, heads):
    ns = seq // rb
    m = batch * seq
    width = heads * R_DIM

    def col_spec(col):
        c0 = (col - HEAD_COLS) // width
        return pl.BlockSpec((rb, width), lambda b, h, s, lg: (b * ns + s, c0 + h))

    grid_spec = pltpu.PrefetchScalarGridSpec(
        num_scalar_prefetch=1,
        grid=(batch, R_HEADS // heads, ns),
        in_specs=[col_spec(COL_QR), col_spec(COL_KR), col_spec(COL_VR), col_spec(COL_ZR),
                  pl.BlockSpec((1, width), lambda b, h, s, lg: (0, h))],
        out_specs=[pl.BlockSpec((rb, width), lambda b, h, s, lg: (b * ns + s, h)),
                   pl.BlockSpec((None, heads, R_DIM, R_DIM), lambda b, h, s, lg: (b, h, 0, 0))],
        scratch_shapes=[pltpu.VMEM((heads, R_DIM, R_DIM), F32)],
    )
    return pl.pallas_call(
        _retention_kernel, grid_spec=grid_spec,
        out_shape=[jax.ShapeDtypeStruct((m, R_WIDTH), BF16),
                   jax.ShapeDtypeStruct((batch, R_HEADS, R_DIM, R_DIM), F32)],
        compiler_params=_params(("parallel", "parallel", "arbitrary")),
        name="retention",
    )(log_g, zr_all, zr_all, zr_all, zr_all, g_ret.reshape(1, R_WIDTH))


def _out_kernel(ya_ref, yr_ref, ga_ref, gb_ref, x_ref, p_ref,
                wa_ref, wr_ref, wo_ref, gpost_ref, wg_ref, wp_ref, o_ref):
    a = jnp.dot(ya_ref[...].astype(BF16), wa_ref[...], preferred_element_type=F32)
    b = jnp.dot(yr_ref[...].astype(BF16), wr_ref[...], preferred_element_type=F32)
    merged = (jax.nn.sigmoid(ga_ref[...].astype(F32)) * a
              + jax.nn.sigmoid(gb_ref[...].astype(F32)) * b)
    y = jnp.dot(merged.astype(BF16), wo_ref[...], preferred_element_type=F32)
    y = y * lax.rsqrt(jnp.mean(y * y, axis=-1, keepdims=True) + EPS)
    hres = x_ref[...] + y * gpost_ref[...]
    gate = jax.nn.sigmoid(jnp.dot(hres.astype(BF16), wg_ref[...], preferred_element_type=F32))
    emb = jnp.dot(p_ref[...].astype(BF16), wp_ref[...], preferred_element_type=F32)
    o_ref[...] = hres + gate * emb


def _out_proj(ya, yr, gates, ga_blk, gb_blk, x, p, wa, wr, wo, g_post, wg, wp, tm):
    m = x.shape[0]

    def rows(width, cb=0):
        return pl.BlockSpec((tm, width), lambda i: (i, cb))

    def whole(arr):
        return pl.BlockSpec(arr.shape, lambda i: (0, 0), pipeline_mode=pl.Buffered(1))

    gp = g_post.reshape(1, D_MODEL)
    return pl.pallas_call(
        _out_kernel,
        grid=(m // tm,),
        in_specs=[rows(A_WIDTH), rows(R_WIDTH), rows(D_MODEL, ga_blk), rows(D_MODEL, gb_blk),
                  rows(D_MODEL), rows(PLE_DIM),
                  whole(wa), whole(wr), whole(wo), whole(gp), whole(wg), whole(wp)],
        out_specs=rows(D_MODEL),
        out_shape=jax.ShapeDtypeStruct((m, D_MODEL), F32),
        compiler_params=_params(("parallel",)),
        name="out_proj",
    )(ya, yr, gates, gates, x, p, wa, wr, wo, gp, wg, wp)


def _attn_step_kernel(slope_ref, z_ref, c0_ref, c1_ref, c2_ref, o_ref):
    c_refs = (c0_ref, c1_ref, c2_ref)
    scale = A_HEAD_DIM ** -0.5
    steps = float(BAND) - lax.broadcasted_iota(jnp.int32, (BAND, 1, 1), 0).astype(F32)
    outs, lses = [], []
    for g in range(N_GROUPS):
        q = z_ref[CH_Q + g * A_HEADS:CH_Q + (g + 1) * A_HEADS, :] * scale
        kn = z_ref[CH_K + g * A_HEADS:CH_K + (g + 1) * A_HEADS, :]
        vn = z_ref[CH_V + g * A_HEADS:CH_V + (g + 1) * A_HEADS, :]
        kc = c_refs[g][:, 0]
        vc = c_refs[g][:, 1]
        slope = slope_ref[g][None]
        s = jnp.sum(kc * q[None], axis=-1, keepdims=True) - (slope * float(DILATIONS[g])) * steps
        s_new = jnp.sum(kn * q, axis=-1, keepdims=True)
        m = jnp.maximum(jnp.max(s, axis=0), s_new)
        e = jnp.exp(s - m[None])
        e_new = jnp.exp(s_new - m)
        den = jnp.sum(e, axis=0) + e_new
        outs.append((jnp.sum(e * vc, axis=0) + e_new * vn) / den)
        lses.append(m + jnp.log(den))
    mx = jnp.maximum(jnp.maximum(lses[0], lses[1]), lses[2])
    ws = [jnp.exp(l - mx) for l in lses]
    mixed = (ws[0] * outs[0] + ws[1] * outs[1] + ws[2] * outs[2]) / (ws[0] + ws[1] + ws[2])
    o_ref[...] = mixed * _silu(z_ref[CH_ZA:CH_ZA + A_HEADS, :])


def _attention_step(zs_h, caches, slopes):
    b = zs_h.shape[0]
    z3 = zs_h.reshape(b, N_HEAD_CHUNKS, A_HEAD_DIM)
    views = [c.reshape(b, BAND, DILATIONS[g], 2, A_HEADS, A_HEAD_DIM) for g, c in enumerate(caches)]
    cache_spec = pl.BlockSpec((None, BAND, None, 2, A_HEADS, A_HEAD_DIM),
                              lambda i: (i, 0, 0, 0, 0, 0))
    out = pl.pallas_call(
        _attn_step_kernel,
        grid=(b,),
        in_specs=[pl.BlockSpec((N_GROUPS, A_HEADS, 1), lambda i: (0, 0, 0)),
                  pl.BlockSpec((None, N_HEAD_CHUNKS, A_HEAD_DIM), lambda i: (i, 0, 0)),
                  cache_spec, cache_spec, cache_spec],
        out_specs=pl.BlockSpec((None, A_HEADS, A_HEAD_DIM), lambda i: (i, 0, 0)),
        out_shape=jax.ShapeDtypeStruct((b, A_HEADS, A_HEAD_DIM), F32),
        compiler_params=_params(("parallel",)),
        name="dilated_attn_step",
    )(slopes.reshape(N_GROUPS, A_HEADS, 1), z3, *views)
    return out.reshape(b, A_WIDTH)


def _retention_step_kernel(lg_ref, q_ref, k_ref, v_ref, zr_ref, gret_ref, r_ref, y_ref, rout_ref):
    for hh in range(R_HEADS):
        cols = slice(hh * R_DIM, (hh + 1) * R_DIM)
        gamma = jnp.exp(jnp.full((1, 1), 1.0, F32) * lg_ref[hh])
        q = q_ref[:, cols]
        k = k_ref[:, cols] * (R_DIM ** -0.5)
        v = v_ref[:, cols]
        r_prev = r_ref[hh]
        qb = jnp.broadcast_to(q, (8, R_DIM)).astype(BF16)
        cross = jnp.dot(qb, r_prev.astype(BF16), preferred_element_type=F32)[0:1] * gamma
        inner = jnp.sum(q * k, axis=-1, keepdims=True) * v
        k_col = jnp.broadcast_to(k, (R_DIM, R_DIM)).T
        rout_ref[hh] = r_prev * gamma + k_col * v
        y = inner + cross
        y = y * lax.rsqrt(jnp.mean(y * y, axis=-1, keepdims=True) + EPS)
        y_ref[:, cols] = (y * gret_ref[:, cols]) * _silu(zr_ref[:, cols])


def _retention_step(zs_r, state, log_g, g_ret):
    b = zs_r.shape[0]
    z3 = zs_r.reshape(b, 1, REST_COLS)

    def col_spec(col):
        return pl.BlockSpec((None, 1, R_WIDTH), lambda i, lg: (i, 0, (col - HEAD_COLS) // R_WIDTH))

    state_spec = pl.BlockSpec((None, R_HEADS, R_DIM, R_DIM), lambda i, lg: (i, 0, 0, 0))
    grid_spec = pltpu.PrefetchScalarGridSpec(
        num_scalar_prefetch=1,
        grid=(b,),
        in_specs=[col_spec(COL_QR), col_spec(COL_KR), col_spec(COL_VR), col_spec(COL_ZR),
                  pl.BlockSpec((1, R_WIDTH), lambda i, lg: (0, 0)), state_spec],
        out_specs=[pl.BlockSpec((None, 1, R_WIDTH), lambda i, lg: (i, 0, 0)), state_spec],
    )
    y, new_state = pl.pallas_call(
        _retention_step_kernel, grid_spec=grid_spec,
        out_shape=[jax.ShapeDtypeStruct((b, 1, R_WIDTH), F32),
                   jax.ShapeDtypeStruct(state.shape, F32)],
        compiler_params=_params(("parallel",)),
        name="retention_step",
    )(log_g, z3, z3, z3, z3, g_ret.reshape(1, R_WIDTH), state)
    return y.reshape(b, R_WIDTH), new_state


def _window_new_row_kernel(z_ref, c0, c1, c2, o0, o1, o2):
    del c0, c1, c2
    for g, o_ref in enumerate((o0, o1, o2)):
        o_ref[0, 0] = z_ref[CH_K + g * A_HEADS:CH_K + (g + 1) * A_HEADS, :]
        o_ref[0, 1] = z_ref[CH_V + g * A_HEADS:CH_V + (g + 1) * A_HEADS, :]


def _window_new_row(zs_h, shifted):
    b = zs_h.shape[0]
    tile = (2, A_HEADS, A_HEAD_DIM)
    last_row = [pl.BlockSpec((None, 1) + tile, functools.partial(lambda i, w: (i, w - 1, 0, 0, 0), w=c.shape[1]))
                for c in shifted]
    return pl.pallas_call(
        _window_new_row_kernel,
        grid=(b,),
        in_specs=[pl.BlockSpec((None, N_HEAD_CHUNKS, A_HEAD_DIM), lambda i: (i, 0, 0))] + last_row,
        out_specs=last_row,
        out_shape=[jax.ShapeDtypeStruct(c.shape, c.dtype) for c in shifted],
        input_output_aliases={1: 0, 2: 1, 3: 2},
        compiler_params=_params(("parallel",)),
        name="window_new_row",
    )(zs_h.reshape(b, N_HEAD_CHUNKS, A_HEAD_DIM), *shifted)


def _alibi_slopes():
    n = N_GROUPS * A_HEADS
    return jnp.exp2(-8.0 * (jnp.arange(n, dtype=F32) + 1.0) / n).reshape(N_GROUPS, A_HEADS)


def _retention_log_decay():
    return jnp.log1p(-jnp.exp2(-5.0 - jnp.arange(R_HEADS, dtype=F32)))


def kernel(x_prompt, x_sample, cache_win0, cache_win1, cache_win2, state_ret, p_prompt, p_sample,
           g_pre, w_in, g_ret, w_a_out, w_r_out, w_o, g_post, w_ple_gate, w_ple_proj):
    batch, seq, _ = x_prompt.shape
    dec_batch = x_sample.shape[0]
    assert g_pre.shape[0] == 1 and x_sample.shape[1] == 1
    assert seq % ATT_TILE == 0
    m = batch * seq
    slopes = _alibi_slopes()
    log_g = _retention_log_decay()

    wa, wr, wo = w_a_out[0].astype(BF16), w_r_out[0].astype(BF16), w_o[0].astype(BF16)
    wg, wp = w_ple_gate[0].astype(BF16), w_ple_proj[0].astype(BF16)

    xp = x_prompt.reshape(m, D_MODEL)
    xs = x_sample.reshape(dec_batch, D_MODEL)
    u = _rmsnorm(xp, g_pre[0], TM_NORM)
    us = _rmsnorm(xs, g_pre[0], dec_batch)
    caches = (cache_win0[0], cache_win1[0], cache_win2[0])
    shift = [functools.partial(_shift_rider, c, min(RIDER_ROWS, c.shape[1])) for c in caches]
    n0, n1, n2 = (dec_batch * (c.shape[1] // min(RIDER_ROWS, c.shape[1])) for c in caches)
    zh, zs_h, sh1, sh0 = _in_proj(u, us, w_in[0], 0, HEAD_COLS, F32, TM_IN, TN_IN, True,
                                  (COL_ZA, COL_QR), [(0, shift[1]), (n1, shift[0])])
    layout = [functools.partial(_window_prompt_rider, zh, g, batch, seq, RIDER_ROWS) for g in range(N_GROUPS)]
    np0 = batch * 2 * (min(WINDOWS[0], seq) // min(RIDER_ROWS, WINDOWS[0]))
    zr, zs_r, sh2, wp2, wp1, wp0 = _in_proj(
        u, us, w_in[0], HEAD_COLS, REST_COLS, BF16, TM_IN, TN_IN, False, (COL_ZR, COL_GA),
        [(0, shift[2]), (n2, layout[2]), (np0, layout[1]), (0, layout[0])])
    win_p = [wp.reshape(1, batch, -1, 2, A_HEADS, A_HEAD_DIM) for wp in (wp0, wp1, wp2)]
    ga_blk = (COL_GA - HEAD_COLS) // D_MODEL

    ya = _attention(zh, slopes, batch, seq)
    yr, ret_prompt = _retention(zr, log_g, g_ret[0], batch, seq, RET_ROWS, RET_HEADS)
    y_prompt = _out_proj(ya, yr, zr, ga_blk, ga_blk + 1, xp, p_prompt[0].reshape(m, PLE_DIM),
                         wa, wr, wo, g_post[0], wg, wp, TM_OUT).reshape(batch, seq, D_MODEL)

    ya_s = _attention_step(zs_h, caches, slopes)
    yr_s, ret_sample = _retention_step(zs_r, state_ret[0], log_g, g_ret[0])
    y_sample = _out_proj(ya_s, yr_s, zs_r, ga_blk, ga_blk + 1, xs, p_sample[0].reshape(dec_batch, PLE_DIM),
                         wa, wr, wo, g_post[0], wg, wp, dec_batch).reshape(dec_batch, 1, D_MODEL)

    win_s = _window_new_row(zs_h, (sh0, sh1, sh2))
    return (y_prompt, y_sample, win_p[0], win_p[1], win_p[2], ret_prompt[None],
            win_s[0][None], win_s[1][None], win_s[2][None], ret_sample[None])
```

```python
import functools

import jax
import jax.numpy as jnp
from jax import lax
from jax.experimental import pallas as pl
from jax.experimental.pallas import tpu as pltpu

F32 = jnp.float32
BF16 = jnp.bfloat16

D_MODEL = 2048
N_GROUPS = 3
DILATIONS = (1, 4, 16)
WINDOWS = (128, 512, 2048)
BAND = 128
A_HEADS = 8
A_HEAD_DIM = 128
A_QKV = N_GROUPS * A_HEADS * A_HEAD_DIM
A_WIDTH = A_HEADS * A_HEAD_DIM
R_HEADS = 8
R_DIM = 256
R_WIDTH = R_HEADS * R_DIM
R_CHUNK = 128
PLE_DIM = 256
EPS = 1e-6
N_IN = 3 * A_QKV + A_WIDTH + 4 * R_WIDTH + 2 * D_MODEL

COL_QA, COL_KA, COL_VA = 0, A_QKV, 2 * A_QKV
COL_ZA = 3 * A_QKV
COL_QR = COL_ZA + A_WIDTH
COL_KR = COL_QR + R_WIDTH
COL_VR = COL_KR + R_WIDTH
COL_ZR = COL_VR + R_WIDTH
COL_GA = COL_ZR + R_WIDTH
COL_GB = COL_GA + D_MODEL

HEAD_COLS = COL_QR
REST_COLS = N_IN - HEAD_COLS
CH_Q, CH_K, CH_V, CH_ZA = (c // A_HEAD_DIM for c in (COL_QA, COL_KA, COL_VA, COL_ZA))
N_HEAD_CHUNKS = HEAD_COLS // A_HEAD_DIM

NEG = -1e30
ATT_TILE = BAND * max(DILATIONS)

assert R_DIM ** -0.5 == 2.0 ** -4

TM_NORM = 512
TM_IN, TN_IN = 1024, 1024
RIDER_ROWS = 256
RET_ROWS = 1024
RET_HEADS = 4
TM_OUT = 256
VMEM_LIMIT = 60 * 1024 * 1024


def _params(semantics, vmem=VMEM_LIMIT):
    return pltpu.CompilerParams(dimension_semantics=semantics, vmem_limit_bytes=vmem)


def _silu(x):
    return x * jax.nn.sigmoid(x)


def _rmsnorm_kernel(x_ref, g_ref, o_ref):
    x = x_ref[...]
    y = x * lax.rsqrt(jnp.mean(x * x, axis=-1, keepdims=True) + EPS)
    o_ref[...] = (y * g_ref[...]).astype(o_ref.dtype)


def _rmsnorm(x, g, tm):
    m, d = x.shape
    return pl.pallas_call(
        _rmsnorm_kernel,
        grid=(m // tm,),
        in_specs=[pl.BlockSpec((tm, d), lambda i: (i, 0)),
                  pl.BlockSpec((1, d), lambda i: (0, 0))],
        out_specs=pl.BlockSpec((tm, d), lambda i: (i, 0)),
        out_shape=jax.ShapeDtypeStruct((m, d), BF16),
        compiler_params=_params(("parallel",)),
        name="rmsnorm",
    )(x, g.reshape(1, d))


def _heads_to_rows_copy(in_ref, out_ref):
    for hh in range(A_HEADS):
        out_ref[:, hh, :] = in_ref[hh]


def _shift_dma_step(step, cache_ref, out_ref, buf, rsem, wsem, tsem, k0, rb):
    rows = cache_ref.shape[0]
    chunks = -(-(rows - 1) // rb)
    ci = step - k0

    def start_row(n):
        return jnp.minimum(n * rb, rows - 1 - rb)

    def read(n):
        return pltpu.make_async_copy(cache_ref.at[pl.ds(start_row(n) + 1, rb)], buf.at[n % 2], rsem.at[n % 2])

    def write(n):
        return pltpu.make_async_copy(buf.at[n % 2], out_ref.at[pl.ds(start_row(n), rb)], wsem.at[n % 2])

    tail = pltpu.make_async_copy(cache_ref.at[pl.ds(rows - 1, 1)], out_ref.at[pl.ds(rows - 1, 1)], tsem.at[0])

    @pl.when(ci == 0)
    def _():
        read(ci).start()
        tail.start()

    @pl.when(ci == chunks)
    def _():
        tail.wait()

    @pl.when(jnp.logical_and(ci >= 1, ci <= chunks))
    def _():
        write(ci - 1).wait()

    @pl.when(jnp.logical_and(ci >= 0, ci < chunks))
    def _():
        read(ci).wait()
        write(ci).start()

    @pl.when(jnp.logical_and(ci >= 0, ci + 1 < chunks))
    def _():
        read(ci + 1).start()


def _shift_dma_steps(rows, rb):
    return -(-(rows - 1) // rb) + 1


def _in_proj_kernel(*refs, head_major, silu_tiles, riders, shift_k0s):
    n_in = sum(r[1] for r in riders)
    ns = len(shift_k0s)
    u_ref, us_ref, w_ref = refs[:3]
    rider_in = refs[3:3 + n_in]
    shift_in = refs[3 + n_in:3 + n_in + ns]
    outs = refs[3 + n_in + ns:]
    o_ref, os_ref = outs[:2]
    rider_out = outs[2:2 + len(riders)]
    shift_out = outs[2 + len(riders):2 + len(riders) + ns]
    wb_ref = outs[2 + len(riders) + ns]
    c = pl.program_id(2)
    j = 2 * pl.program_id(0) + c

    @pl.when(pl.program_id(1) == 0)
    def _():
        wb_ref[c] = w_ref[...].astype(BF16)
        os_ref[...] = jnp.dot(us_ref[...], wb_ref[c], preferred_element_type=F32)

    def project(act):
        acc = jnp.dot(u_ref[...], wb_ref[c], preferred_element_type=F32)
        if act is not None:
            acc = act(acc)
        if head_major:
            for ch in range(o_ref.shape[0]):
                o_ref[ch] = acc[:, ch * A_HEAD_DIM:(ch + 1) * A_HEAD_DIM].astype(o_ref.dtype)
        else:
            o_ref[...] = acc.astype(o_ref.dtype)

    gated = jnp.logical_and(j >= silu_tiles[0], j < silu_tiles[1])
    pl.when(gated)(functools.partial(project, _silu))
    pl.when(jnp.logical_not(gated))(functools.partial(project, None))

    step = (pl.program_id(0) * pl.num_programs(1) + pl.program_id(1)) * 2 + c
    pos = 0
    for (copy, nin, k0, nsteps), out_ref in zip(riders, rider_out):
        active = jnp.logical_and(step >= k0, step < k0 + nsteps)
        pl.when(active)(functools.partial(copy, *rider_in[pos:pos + nin], out_ref))
        pos += nin

    if ns:
        buf, rsem, wsem, tsem = outs[3 + len(riders) + ns:7 + len(riders) + ns]
        for cache_ref, out_ref, k0 in zip(shift_in, shift_out, shift_k0s):
            _shift_dma_step(step, cache_ref, out_ref, buf, rsem, wsem, tsem, k0, buf.shape[1])


def _window_prompt_rider(zh, g, batch, seq, rb, k0, flat_step):
    keep = min(WINDOWS[g], seq)
    rb = min(rb, keep)
    nsb = keep // rb
    row0 = (seq - keep) // rb

    def loc(*idx):
        l = jnp.clip(flat_step(*idx) - k0, 0, batch * 2 * nsb - 1)
        return l // (2 * nsb), (l // nsb) % 2, l % nsb

    def in_map(*idx):
        b, kv, s = loc(*idx)
        return (CH_K + kv * (CH_V - CH_K)) // A_HEADS + g, b * (seq // rb) + row0 + s, 0

    def out_map(*idx):
        b, kv, s = loc(*idx)
        return b * nsb + s, kv, 0

    return dict(copy=_heads_to_rows_copy, args=[zh], steps=batch * 2 * nsb,
                in_specs=[pl.BlockSpec((A_HEADS, rb, A_HEAD_DIM), in_map)],
                out_spec=pl.BlockSpec((rb, A_HEADS, A_HEAD_DIM), out_map),
                out_shape=jax.ShapeDtypeStruct((batch * keep, 2 * A_HEADS, A_HEAD_DIM), zh.dtype))


def _in_proj(u, us, w, col0, ncols, out_dtype, tm, tn, head_major, silu_cols, make_riders=(),
             shift_caches=()):
    m, k = u.shape
    ms = us.shape[0]
    j0 = col0 // tn
    nj, ni = ncols // tn, m // tm
    assert nj % 2 == 0

    def flat_step(p, i, c):
        return (p * ni + i) * 2 + c

    def sticky_tile(p, i, c):
        return 2 * p + jnp.where(i == 0, c, 1)

    in_specs = [pl.BlockSpec((tm, k), lambda p, i, c: (i, 0)),
                pl.BlockSpec((ms, k), lambda p, i, c: (0, 0)),
                pl.BlockSpec((k, tn), lambda p, i, c: (0, j0 + sticky_tile(p, i, c)))]
    if head_major:
        o_spec = pl.BlockSpec((tn // A_HEAD_DIM, tm, A_HEAD_DIM), lambda p, i, c: (2 * p + c, i, 0))
        o_shape = jax.ShapeDtypeStruct((ncols // A_HEAD_DIM, m, A_HEAD_DIM), out_dtype)
    else:
        o_spec = pl.BlockSpec((tm, tn), lambda p, i, c: (i, 2 * p + c))
        o_shape = jax.ShapeDtypeStruct((m, ncols), out_dtype)
    out_specs = [o_spec, pl.BlockSpec((ms, tn), lambda p, i, c: (0, sticky_tile(p, i, c)))]
    out_shape = [o_shape, jax.ShapeDtypeStruct((ms, ncols), F32)]
    riders, rider_args = [], []
    for k0, make in make_riders:
        rider = make(k0, flat_step)
        in_specs += rider["in_specs"]
        out_specs.append(rider["out_spec"])
        out_shape.append(rider["out_shape"])
        rider_args += rider["args"]
        riders.append((rider["copy"], len(rider["in_specs"]), k0, rider["steps"]))
        assert k0 + rider["steps"] <= nj * ni
    any_spec = pl.BlockSpec(memory_space=pl.ANY)
    scratch = [pltpu.VMEM((2, k, tn), BF16)]
    shift_k0s, k0 = [], 0
    for cache in shift_caches:
        in_specs.append(any_spec)
        out_specs.append(any_spec)
        out_shape.append(jax.ShapeDtypeStruct(cache.shape, cache.dtype))
        shift_k0s.append(k0)
        k0 += _shift_dma_steps(cache.shape[0], RIDER_ROWS)
    assert k0 <= nj * ni
    if shift_caches:
        scratch += [pltpu.VMEM((2, RIDER_ROWS) + shift_caches[0].shape[1:], shift_caches[0].dtype),
                    pltpu.SemaphoreType.DMA((2,)), pltpu.SemaphoreType.DMA((2,)),
                    pltpu.SemaphoreType.DMA((1,))]
    silu_tiles = ((silu_cols[0] - col0) // tn, (silu_cols[1] - col0) // tn)
    return pl.pallas_call(
        functools.partial(_in_proj_kernel, head_major=head_major, silu_tiles=silu_tiles,
                          riders=tuple(riders), shift_k0s=tuple(shift_k0s)),
        grid=(nj // 2, ni, 2),
        in_specs=in_specs,
        out_specs=out_specs,
        out_shape=out_shape,
        scratch_shapes=scratch,
        compiler_params=_params(("arbitrary", "arbitrary", "arbitrary")),
        name="in_proj_hm" if head_major else "in_proj",
    )(u, us, w, *rider_args, *shift_caches)


def _attn_kernel(slope_ref, q0, q1, q2, k0, k1, k2, v0, v1, v2, za_ref, o_ref,
                 qbuf, kbuf, vbuf, onat, lnat, *, seq):
    q_refs, k_refs, v_refs = (q0, q1, q2), (k0, k1, k2), (v0, v1, v2)
    h = pl.program_id(1)
    t = pl.program_id(2)

    @pl.when(t == 0)
    def _():
        for g, dil in enumerate(DILATIONS):
            for r in range(dil):
                pad = pl.ds(r * (seq // dil + BAND), BAND)
                kbuf[g, pad, :] = jnp.zeros((BAND, A_HEAD_DIM), BF16)
                vbuf[g, pad, :] = jnp.zeros((BAND, A_HEAD_DIM), BF16)

    qi = lax.broadcasted_iota(jnp.int32, (BAND, 2 * BAND), 0)
    kj = lax.broadcasted_iota(jnp.int32, (BAND, 2 * BAND), 1)
    dist = qi + BAND - kj
    valid = jnp.logical_and(dist >= 0, dist <= BAND)
    distf = dist.astype(F32)
    first_pen = jnp.where(jnp.logical_and(t == 0, kj < BAND), NEG, 0.0)
    scale = A_HEAD_DIM ** -0.5

    for g in range(N_GROUPS):
        dil = DILATIONS[g]
        n = ATT_TILE // dil
        nblk = n // BAND
        run = seq // dil + BAND
        for r in range(dil):
            rows = pl.ds(r, n, stride=dil) if dil > 1 else pl.ds(0, n)
            dst = pl.ds(pl.multiple_of(r * run + BAND + t * n, BAND), n)
            qbuf[g, r * n:(r + 1) * n, :] = (q_refs[g][rows, :] * scale).astype(BF16)
            kbuf[g, dst, :] = k_refs[g][rows, :].astype(BF16)
            vbuf[g, dst, :] = v_refs[g][rows, :].astype(BF16)

        bias = jnp.where(valid, -(slope_ref[g, h] * float(dil)) * distf, NEG)
        bias_first = bias + first_pen

        for r in range(dil):
            for i in range(nblk):
                u = r * nblk + i
                kv_rows = pl.ds(pl.multiple_of(r * run + i * BAND + t * n, BAND), 2 * BAND)
                q = qbuf[g, u * BAND:(u + 1) * BAND, :]
                s = lax.dot_general(q, kbuf[g, kv_rows, :], (((1,), (1,)), ((), ())),
                                    preferred_element_type=F32)
                s = s + (bias_first if i == 0 else bias)
                m = jnp.max(s, axis=-1, keepdims=True)
                e = jnp.exp(s - m)
                den = jnp.sum(e, axis=-1, keepdims=True)
                acc = jnp.dot(e.astype(BF16), vbuf[g, kv_rows, :], preferred_element_type=F32)
                out_rows = (pl.ds(r + i * (BAND * dil), BAND, stride=dil) if dil > 1
                            else pl.ds(u * BAND, BAND))
                onat[g, out_rows, :] = acc / den
                lnat[g, out_rows, :] = jnp.broadcast_to(m + jnp.log(den), (BAND, A_HEAD_DIM))

    l0, l1, l2 = lnat[0], lnat[1], lnat[2]
    mx = jnp.maximum(jnp.maximum(l0, l1), l2)
    w0, w1, w2 = jnp.exp(l0 - mx), jnp.exp(l1 - mx), jnp.exp(l2 - mx)
    mixed = (w0 * onat[0] + w1 * onat[1] + w2 * onat[2]) / (w0 + w1 + w2)
    o_ref[...] = (mixed * za_ref[...]).astype(o_ref.dtype)


def _attention(zh, slopes, batch, seq):
    nt = seq // ATT_TILE
    m = batch * seq

    def head_spec(c0):
        return pl.BlockSpec((None, ATT_TILE, A_HEAD_DIM), lambda b, h, t, s: (c0 + h, b * nt + t, 0))

    in_specs = [head_spec(c + g * A_HEADS) for c in (CH_Q, CH_K, CH_V) for g in range(N_GROUPS)]
    in_specs.append(head_spec(CH_ZA))
    buf_rows = seq + BAND * max(DILATIONS)
    grid_spec = pltpu.PrefetchScalarGridSpec(
        num_scalar_prefetch=1,
        grid=(batch, A_HEADS, nt),
        in_specs=in_specs,
        out_specs=pl.BlockSpec((ATT_TILE, A_HEAD_DIM), lambda b, h, t, s: (b * nt + t, h)),
        scratch_shapes=[
            pltpu.VMEM((N_GROUPS, ATT_TILE, A_HEAD_DIM), BF16),
            pltpu.VMEM((N_GROUPS, buf_rows, A_HEAD_DIM), BF16),
            pltpu.VMEM((N_GROUPS, buf_rows, A_HEAD_DIM), BF16),
            pltpu.VMEM((N_GROUPS, ATT_TILE, A_HEAD_DIM), F32),
            pltpu.VMEM((N_GROUPS, ATT_TILE, A_HEAD_DIM), F32),
        ],
    )
    return pl.pallas_call(
        functools.partial(_attn_kernel, seq=seq), grid_spec=grid_spec,
        out_shape=jax.ShapeDtypeStruct((m, A_WIDTH), BF16),
        compiler_params=_params(("parallel", "parallel", "arbitrary")),
        name="dilated_attn",
    )(slopes, *([zh] * 10))


def _retention_kernel(lg_ref, q_ref, k_ref, v_ref, zr_ref, gret_ref, y_ref, rout_ref, r_scr):
    s = pl.program_id(2)
    heads = r_scr.shape[0]

    @pl.when(s == 0)
    def _():
        r_scr[...] = jnp.zeros_like(r_scr)

    L = R_CHUNK
    pi = lax.broadcasted_iota(jnp.int32, (L, L), 0)
    pj = lax.broadcasted_iota(jnp.int32, (L, L), 1)
    rel = (pi - pj).astype(F32)
    pos = lax.broadcasted_iota(jnp.int32, (L, 1), 0).astype(F32)
    k_scale = R_DIM ** -0.5
    decays = []
    for hp in range(heads):
        log_g = lg_ref[pl.program_id(1) * heads + hp]
        decays.append((
            jnp.where(rel >= 0, jnp.exp(jnp.maximum(rel, 0.0) * log_g), 0.0) * k_scale,
            jnp.exp((pos + 1.0) * log_g),
            jnp.exp((L - 1.0 - pos) * log_g) * k_scale,
            jnp.exp(jnp.full((1, 1), float(L), F32) * log_g)))

    def chunk(c, carry):
        rows = pl.ds(pl.multiple_of(c * L, L), L)
        for hp, (intra, q_decay, k_decay, chunk_decay) in enumerate(decays):
            cols = slice(hp * R_DIM, (hp + 1) * R_DIM)
            q = q_ref[rows, cols]
            k = k_ref[rows, cols]
            v = v_ref[rows, cols]
            r_prev = r_scr[hp]
            scores = lax.dot_general(q, k, (((1,), (1,)), ((), ())), preferred_element_type=F32) * intra
            inner = jnp.dot(scores.astype(BF16), v, preferred_element_type=F32)
            cross = jnp.dot(q, r_prev.astype(BF16), preferred_element_type=F32) * q_decay
            kd = (k.astype(F32) * k_decay).astype(BF16)
            r_scr[hp] = r_prev * chunk_decay + lax.dot_general(
                kd, v, (((0,), (0,)), ((), ())), preferred_element_type=F32)
            y = inner + cross
            y = y * lax.rsqrt(jnp.mean(y * y, axis=-1, keepdims=True) + EPS)
            y = (y * gret_ref[:, cols]) * zr_ref[rows, cols].astype(F32)
            y_ref[rows, cols] = y.astype(y_ref.dtype)
        return carry

    lax.fori_loop(0, q_ref.shape[0] // L, chunk, 0, unroll=True)

    @pl.when(s == pl.num_programs(2) - 1)
    def _():
        rout_ref[...] = r_scr[...]


def _retention(zr_all, log_g, g_ret, batch, seq, rb, heads):
    ns = seq // rb
    m = batch * seq
    width = heads * R_DIM

    def col_spec(col):
        c0 = (col - HEAD_COLS) // width
        return pl.BlockSpec((rb, width), lambda b, h, s, lg: (b * ns + s, c0 + h))

    grid_spec = pltpu.PrefetchScalarGridSpec(
        num_scalar_prefetch=1,
        grid=(batch, R_HEADS // heads, ns),
        in_specs=[col_spec(COL_QR), col_spec(COL_KR), col_spec(COL_VR), col_spec(COL_ZR),
                  pl.BlockSpec((1, width), lambda b, h, s, lg: (0, h))],
        out_specs=[pl.BlockSpec((rb, width), lambda b, h, s, lg: (b * ns + s, h)),
                   pl.BlockSpec((None, heads, R_DIM, R_DIM), lambda b, h, s, lg: (b, h, 0, 0))],
        scratch_shapes=[pltpu.VMEM((heads, R_DIM, R_DIM), F32)],
    )
    return pl.pallas_call(
        _retention_kernel, grid_spec=grid_spec,
        out_shape=[jax.ShapeDtypeStruct((m, R_WIDTH), BF16),
                   jax.ShapeDtypeStruct((batch, R_HEADS, R_DIM, R_DIM), F32)],
        compiler_params=_params(("parallel", "parallel", "arbitrary")),
        name="retention",
    )(log_g, zr_all, zr_all, zr_all, zr_all, g_ret.reshape(1, R_WIDTH))


def _out_kernel(ya_ref, yr_ref, ga_ref, gb_ref, x_ref, p_ref,
                wa_ref, wr_ref, wo_ref, gpost_ref, wg_ref, wp_ref, o_ref):
    a = jnp.dot(ya_ref[...].astype(BF16), wa_ref[...], preferred_element_type=F32)
    b = jnp.dot(yr_ref[...].astype(BF16), wr_ref[...], preferred_element_type=F32)
    merged = (jax.nn.sigmoid(ga_ref[...].astype(F32)) * a
              + jax.nn.sigmoid(gb_ref[...].astype(F32)) * b)
    y = jnp.dot(merged.astype(BF16), wo_ref[...], preferred_element_type=F32)
    y = y * lax.rsqrt(jnp.mean(y * y, axis=-1, keepdims=True) + EPS)
    hres = x_ref[...] + y * gpost_ref[...]
    gate = jax.nn.sigmoid(jnp.dot(hres.astype(BF16), wg_ref[...], preferred_element_type=F32))
    emb = jnp.dot(p_ref[...].astype(BF16), wp_ref[...], preferred_element_type=F32)
    o_ref[...] = hres + gate * emb


def _out_proj(ya, yr, gates, ga_blk, gb_blk, x, p, wa, wr, wo, g_post, wg, wp, tm):
    m = x.shape[0]

    def rows(width, cb=0):
        return pl.BlockSpec((tm, width), lambda i: (i, cb))

    def whole(arr):
        return pl.BlockSpec(arr.shape, lambda i: (0, 0), pipeline_mode=pl.Buffered(1))

    gp = g_post.reshape(1, D_MODEL)
    return pl.pallas_call(
        _out_kernel,
        grid=(m // tm,),
        in_specs=[rows(A_WIDTH), rows(R_WIDTH), rows(D_MODEL, ga_blk), rows(D_MODEL, gb_blk),
                  rows(D_MODEL), rows(PLE_DIM),
                  whole(wa), whole(wr), whole(wo), whole(gp), whole(wg), whole(wp)],
        out_specs=rows(D_MODEL),
        out_shape=jax.ShapeDtypeStruct((m, D_MODEL), F32),
        compiler_params=_params(("parallel",)),
        name="out_proj",
    )(ya, yr, gates, gates, x, p, wa, wr, wo, gp, wg, wp)


def _attn_step_kernel(slope_ref, z_ref, c0_ref, c1_ref, c2_ref, o_ref):
    c_refs = (c0_ref, c1_ref, c2_ref)
    scale = A_HEAD_DIM ** -0.5
    steps = float(BAND) - lax.broadcasted_iota(jnp.int32, (BAND, 1, 1), 0).astype(F32)
    outs, lses = [], []
    for g in range(N_GROUPS):
        q = z_ref[CH_Q + g * A_HEADS:CH_Q + (g + 1) * A_HEADS, :] * scale
        kn = z_ref[CH_K + g * A_HEADS:CH_K + (g + 1) * A_HEADS, :]
        vn = z_ref[CH_V + g * A_HEADS:CH_V + (g + 1) * A_HEADS, :]
        kc = c_refs[g][:, 0]
        vc = c_refs[g][:, 1]
        slope = slope_ref[g][None]
        s = jnp.sum(kc * q[None], axis=-1, keepdims=True) - (slope * float(DILATIONS[g])) * steps
        s_new = jnp.sum(kn * q, axis=-1, keepdims=True)
        m = jnp.maximum(jnp.max(s, axis=0), s_new)
        e = jnp.exp(s - m[None])
        e_new = jnp.exp(s_new - m)
        den = jnp.sum(e, axis=0) + e_new
        outs.append((jnp.sum(e * vc, axis=0) + e_new * vn) / den)
        lses.append(m + jnp.log(den))
    mx = jnp.maximum(jnp.maximum(lses[0], lses[1]), lses[2])
    ws = [jnp.exp(l - mx) for l in lses]
    mixed = (ws[0] * outs[0] + ws[1] * outs[1] + ws[2] * outs[2]) / (ws[0] + ws[1] + ws[2])
    o_ref[...] = mixed * _silu(z_ref[CH_ZA:CH_ZA + A_HEADS, :])


def _attention_step(zs_h, caches, slopes):
    b = zs_h.shape[0]
    z3 = zs_h.reshape(b, N_HEAD_CHUNKS, A_HEAD_DIM)
    views = [c.reshape(b, BAND, DILATIONS[g], 2, A_HEADS, A_HEAD_DIM) for g, c in enumerate(caches)]
    cache_spec = pl.BlockSpec((None, BAND, None, 2, A_HEADS, A_HEAD_DIM),
                              lambda i: (i, 0, 0, 0, 0, 0))
    out = pl.pallas_call(
        _attn_step_kernel,
        grid=(b,),
        in_specs=[pl.BlockSpec((N_GROUPS, A_HEADS, 1), lambda i: (0, 0, 0)),
                  pl.BlockSpec((None, N_HEAD_CHUNKS, A_HEAD_DIM), lambda i: (i, 0, 0)),
                  cache_spec, cache_spec, cache_spec],
        out_specs=pl.BlockSpec((None, A_HEADS, A_HEAD_DIM), lambda i: (i, 0, 0)),
        out_shape=jax.ShapeDtypeStruct((b, A_HEADS, A_HEAD_DIM), F32),
        compiler_params=_params(("parallel",)),
        name="dilated_attn_step",
    )(slopes.reshape(N_GROUPS, A_HEADS, 1), z3, *views)
    return out.reshape(b, A_WIDTH)


def _retention_step_kernel(lg_ref, q_ref, k_ref, v_ref, zr_ref, gret_ref, r_ref, y_ref, rout_ref):
    for hh in range(R_HEADS):
        cols = slice(hh * R_DIM, (hh + 1) * R_DIM)
        gamma = jnp.exp(jnp.full((1, 1), 1.0, F32) * lg_ref[hh])
        q = q_ref[:, cols]
        k = k_ref[:, cols] * (R_DIM ** -0.5)
        v = v_ref[:, cols]
        r_prev = r_ref[hh]
        qb = jnp.broadcast_to(q, (8, R_DIM)).astype(BF16)
        cross = jnp.dot(qb, r_prev.astype(BF16), preferred_element_type=F32)[0:1] * gamma
        inner = jnp.sum(q * k, axis=-1, keepdims=True) * v
        k_col = jnp.broadcast_to(k, (R_DIM, R_DIM)).T
        rout_ref[hh] = r_prev * gamma + k_col * v
        y = inner + cross
        y = y * lax.rsqrt(jnp.mean(y * y, axis=-1, keepdims=True) + EPS)
        y_ref[:, cols] = (y * gret_ref[:, cols]) * _silu(zr_ref[:, cols])


def _retention_step(zs_r, state, log_g, g_ret):
    b = zs_r.shape[0]
    z3 = zs_r.reshape(b, 1, REST_COLS)

    def col_spec(col):
        return pl.BlockSpec((None, 1, R_WIDTH), lambda i, lg: (i, 0, (col - HEAD_COLS) // R_WIDTH))

    state_spec = pl.BlockSpec((None, R_HEADS, R_DIM, R_DIM), lambda i, lg: (i, 0, 0, 0))
    grid_spec = pltpu.PrefetchScalarGridSpec(
        num_scalar_prefetch=1,
        grid=(b,),
        in_specs=[col_spec(COL_QR), col_spec(COL_KR), col_spec(COL_VR), col_spec(COL_ZR),
                  pl.BlockSpec((1, R_WIDTH), lambda i, lg: (0, 0)), state_spec],
        out_specs=[pl.BlockSpec((None, 1, R_WIDTH), lambda i, lg: (i, 0, 0)), state_spec],
    )
    y, new_state = pl.pallas_call(
        _retention_step_kernel, grid_spec=grid_spec,
        out_shape=[jax.ShapeDtypeStruct((b, 1, R_WIDTH), F32),
                   jax.ShapeDtypeStruct(state.shape, F32)],
        compiler_params=_params(("parallel",)),
        name="retention_step",
    )(log_g, z3, z3, z3, z3, g_ret.reshape(1, R_WIDTH), state)
    return y.reshape(b, R_WIDTH), new_state


def _window_new_row_kernel(z_ref, c0, c1, c2, o0, o1, o2):
    del c0, c1, c2
    for g, o_ref in enumerate((o0, o1, o2)):
        o_ref[0, 0] = z_ref[CH_K + g * A_HEADS:CH_K + (g + 1) * A_HEADS, :]
        o_ref[0, 1] = z_ref[CH_V + g * A_HEADS:CH_V + (g + 1) * A_HEADS, :]


def _window_new_row(zs_h, shifted):
    b = zs_h.shape[0]
    tile = (2, A_HEADS, A_HEAD_DIM)
    last_row = [pl.BlockSpec((None, 1) + tile, functools.partial(lambda i, w: (i, w - 1, 0, 0, 0), w=c.shape[1]))
                for c in shifted]
    return pl.pallas_call(
        _window_new_row_kernel,
        grid=(b,),
        in_specs=[pl.BlockSpec((None, N_HEAD_CHUNKS, A_HEAD_DIM), lambda i: (i, 0, 0))]
                 + [pl.BlockSpec(memory_space=pl.ANY)] * len(shifted),
        out_specs=last_row,
        out_shape=[jax.ShapeDtypeStruct(c.shape, c.dtype) for c in shifted],
        input_output_aliases={1: 0, 2: 1, 3: 2},
        compiler_params=_params(("parallel",)),
        name="window_new_row",
    )(zs_h.reshape(b, N_HEAD_CHUNKS, A_HEAD_DIM), *shifted)


def _alibi_slopes():
    n = N_GROUPS * A_HEADS
    return jnp.exp2(-8.0 * (jnp.arange(n, dtype=F32) + 1.0) / n).reshape(N_GROUPS, A_HEADS)


def _retention_log_decay():
    return jnp.log1p(-jnp.exp2(-5.0 - jnp.arange(R_HEADS, dtype=F32)))


def kernel(x_prompt, x_sample, cache_win0, cache_win1, cache_win2, state_ret, p_prompt, p_sample,
           g_pre, w_in, g_ret, w_a_out, w_r_out, w_o, g_post, w_ple_gate, w_ple_proj):
    batch, seq, _ = x_prompt.shape
    dec_batch = x_sample.shape[0]
    assert g_pre.shape[0] == 1 and x_sample.shape[1] == 1
    assert seq % ATT_TILE == 0
    m = batch * seq
    slopes = _alibi_slopes()
    log_g = _retention_log_decay()

    wa, wr, wo = w_a_out[0].astype(BF16), w_r_out[0].astype(BF16), w_o[0].astype(BF16)
    wg, wp = w_ple_gate[0].astype(BF16), w_ple_proj[0].astype(BF16)

    xp = x_prompt.reshape(m, D_MODEL)
    xs = x_sample.reshape(dec_batch, D_MODEL)
    u = _rmsnorm(xp, g_pre[0], TM_NORM)
    us = _rmsnorm(xs, g_pre[0], dec_batch)
    caches = (cache_win0[0], cache_win1[0], cache_win2[0])
    flat = [c.reshape((-1,) + c.shape[2:]) for c in caches]
    zh, zs_h, sh1, sh0 = _in_proj(u, us, w_in[0], 0, HEAD_COLS, F32, TM_IN, TN_IN, True,
                                  (COL_ZA, COL_QR), shift_caches=[flat[1], flat[0]])
    layout = [functools.partial(_window_prompt_rider, zh, g, batch, seq, RIDER_ROWS) for g in range(N_GROUPS)]
    np0 = batch * 2 * (min(WINDOWS[0], seq) // min(RIDER_ROWS, WINDOWS[0]))
    np1 = batch * 2 * (min(WINDOWS[1], seq) // min(RIDER_ROWS, WINDOWS[1]))
    zr, zs_r, wp2, wp1, wp0, sh2 = _in_proj(
        u, us, w_in[0], HEAD_COLS, REST_COLS, BF16, TM_IN, TN_IN, False, (COL_ZR, COL_GA),
        [(np0 + np1, layout[2]), (np0, layout[1]), (0, layout[0])], shift_caches=[flat[2]])
    win_p = [wp.reshape(1, batch, -1, 2, A_HEADS, A_HEAD_DIM) for wp in (wp0, wp1, wp2)]
    shifted = [s.reshape(c.shape) for s, c in zip((sh0, sh1, sh2), caches)]
    ga_blk = (COL_GA - HEAD_COLS) // D_MODEL

    ya = _attention(zh, slopes, batch, seq)
    yr, ret_prompt = _retention(zr, log_g, g_ret[0], batch, seq, RET_ROWS, RET_HEADS)
    y_prompt = _out_proj(ya, yr, zr, ga_blk, ga_blk + 1, xp, p_prompt[0].reshape(m, PLE_DIM),
                         wa, wr, wo, g_post[0], wg, wp, TM_OUT).reshape(batch, seq, D_MODEL)

    ya_s = _attention_step(zs_h, caches, slopes)
    yr_s, ret_sample = _retention_step(zs_r, state_ret[0], log_g, g_ret[0])
    y_sample = _out_proj(ya_s, yr_s, zs_r, ga_blk, ga_blk + 1, xs, p_sample[0].reshape(dec_batch, PLE_DIM),
                         wa, wr, wo, g_post[0], wg, wp, dec_batch).reshape(dec_batch, 1, D_MODEL)

    win_s = _window_new_row(zs_h, shifted)
    return (y_prompt, y_sample, win_p[0], win_p[1], win_p[2], ret_prompt[None],
            win_s[0][None], win_s[1][None], win_s[2][None], ret_sample[None])
```

```python
import functools

import jax
import jax.numpy as jnp
from jax import lax
from jax.experimental import pallas as pl
from jax.experimental.pallas import tpu as pltpu

F32 = jnp.float32
BF16 = jnp.bfloat16

D_MODEL = 2048
N_GROUPS = 3
DILATIONS = (1, 4, 16)
WINDOWS = (128, 512, 2048)
BAND = 128
A_HEADS = 8
A_HEAD_DIM = 128
A_QKV = N_GROUPS * A_HEADS * A_HEAD_DIM
A_WIDTH = A_HEADS * A_HEAD_DIM
R_HEADS = 8
R_DIM = 256
R_WIDTH = R_HEADS * R_DIM
R_CHUNK = 128
PLE_DIM = 256
EPS = 1e-6
N_IN = 3 * A_QKV + A_WIDTH + 4 * R_WIDTH + 2 * D_MODEL

COL_QA, COL_KA, COL_VA = 0, A_QKV, 2 * A_QKV
COL_ZA = 3 * A_QKV
COL_QR = COL_ZA + A_WIDTH
COL_KR = COL_QR + R_WIDTH
COL_VR = COL_KR + R_WIDTH
COL_ZR = COL_VR + R_WIDTH
COL_GA = COL_ZR + R_WIDTH
COL_GB = COL_GA + D_MODEL

HEAD_COLS = COL_QR
REST_COLS = N_IN - HEAD_COLS
CH_Q, CH_K, CH_V, CH_ZA = (c // A_HEAD_DIM for c in (COL_QA, COL_KA, COL_VA, COL_ZA))
N_HEAD_CHUNKS = HEAD_COLS // A_HEAD_DIM

NEG = -1e30
ATT_TILE = BAND * max(DILATIONS)

assert R_DIM ** -0.5 == 2.0 ** -4

TM_NORM = 512
TM_IN, TN_IN = 1024, 1024
RIDER_ROWS = 256
RET_ROWS = 1024
RET_HEADS = 4
TM_OUT = 256
VMEM_LIMIT = 60 * 1024 * 1024


def _params(semantics, vmem=VMEM_LIMIT):
    return pltpu.CompilerParams(dimension_semantics=semantics, vmem_limit_bytes=vmem)


def _silu(x):
    return x * jax.nn.sigmoid(x)


def _rmsnorm_kernel(x_ref, g_ref, o_ref):
    x = x_ref[...]
    y = x * lax.rsqrt(jnp.mean(x * x, axis=-1, keepdims=True) + EPS)
    o_ref[...] = (y * g_ref[...]).astype(o_ref.dtype)


def _rmsnorm(x, g, tm):
    m, d = x.shape
    return pl.pallas_call(
        _rmsnorm_kernel,
        grid=(m // tm,),
        in_specs=[pl.BlockSpec((tm, d), lambda i: (i, 0)),
                  pl.BlockSpec((1, d), lambda i: (0, 0))],
        out_specs=pl.BlockSpec((tm, d), lambda i: (i, 0)),
        out_shape=jax.ShapeDtypeStruct((m, d), BF16),
        compiler_params=_params(("parallel",)),
        name="rmsnorm",
    )(x, g.reshape(1, d))


def _heads_to_rows_copy(in_ref, out_ref):
    for hh in range(A_HEADS):
        out_ref[:, hh, :] = in_ref[hh]


def _shift_dma_step(step, cache_ref, out_ref, buf, rsem, wsem, tsem, k0, rb):
    rows = cache_ref.shape[0]
    chunks = -(-(rows - 1) // rb)
    ci = step - k0

    def start_row(n):
        return jnp.minimum(n * rb, rows - 1 - rb)

    def read(n):
        return pltpu.make_async_copy(cache_ref.at[pl.ds(start_row(n) + 1, rb)], buf.at[n % 2], rsem.at[n % 2])

    def write(n):
        return pltpu.make_async_copy(buf.at[n % 2], out_ref.at[pl.ds(start_row(n), rb)], wsem.at[n % 2])

    tail = pltpu.make_async_copy(cache_ref.at[pl.ds(rows - 1, 1)], out_ref.at[pl.ds(rows - 1, 1)], tsem.at[0])

    @pl.when(ci == 0)
    def _():
        read(ci).start()
        tail.start()

    @pl.when(ci == chunks)
    def _():
        tail.wait()

    @pl.when(jnp.logical_and(ci >= 1, ci <= chunks))
    def _():
        write(ci - 1).wait()

    @pl.when(jnp.logical_and(ci >= 0, ci < chunks))
    def _():
        read(ci).wait()
        write(ci).start()

    @pl.when(jnp.logical_and(ci >= 0, ci + 1 < chunks))
    def _():
        read(ci + 1).start()


def _shift_dma_steps(rows, rb):
    return -(-(rows - 1) // rb) + 1


def _in_proj_kernel(*refs, head_major, silu_tiles, riders, shift_k0s):
    n_in = sum(r[1] for r in riders)
    ns = len(shift_k0s)
    u_ref, us_ref, w_ref = refs[:3]
    rider_in = refs[3:3 + n_in]
    shift_in = refs[3 + n_in:3 + n_in + ns]
    outs = refs[3 + n_in + ns:]
    o_ref, os_ref = outs[:2]
    rider_out = outs[2:2 + len(riders)]
    shift_out = outs[2 + len(riders):2 + len(riders) + ns]
    wb_ref = outs[2 + len(riders) + ns]
    c = pl.program_id(2)
    j = 2 * pl.program_id(0) + c

    @pl.when(pl.program_id(1) == 0)
    def _():
        wb_ref[c] = w_ref[...].astype(BF16)
        os_ref[...] = jnp.dot(us_ref[...], wb_ref[c], preferred_element_type=F32)

    def project(act):
        acc = jnp.dot(u_ref[...], wb_ref[c], preferred_element_type=F32)
        if act is not None:
            acc = act(acc)
        if head_major:
            for ch in range(o_ref.shape[0]):
                o_ref[ch] = acc[:, ch * A_HEAD_DIM:(ch + 1) * A_HEAD_DIM].astype(o_ref.dtype)
        else:
            o_ref[...] = acc.astype(o_ref.dtype)

    gated = jnp.logical_and(j >= silu_tiles[0], j < silu_tiles[1])
    pl.when(gated)(functools.partial(project, _silu))
    pl.when(jnp.logical_not(gated))(functools.partial(project, None))

    step = (pl.program_id(0) * pl.num_programs(1) + pl.program_id(1)) * 2 + c
    pos = 0
    for (copy, nin, k0, nsteps), out_ref in zip(riders, rider_out):
        active = jnp.logical_and(step >= k0, step < k0 + nsteps)
        pl.when(active)(functools.partial(copy, *rider_in[pos:pos + nin], out_ref))
        pos += nin

    if ns:
        buf, rsem, wsem, tsem = outs[3 + len(riders) + ns:7 + len(riders) + ns]
        for cache_ref, out_ref, k0 in zip(shift_in, shift_out, shift_k0s):
            _shift_dma_step(step, cache_ref, out_ref, buf, rsem, wsem, tsem, k0, buf.shape[1])


def _window_prompt_rider(zh, g, batch, seq, rb, k0, flat_step):
    keep = min(WINDOWS[g], seq)
    rb = min(rb, keep)
    nsb = keep // rb
    row0 = (seq - keep) // rb

    def loc(*idx):
        l = jnp.clip(flat_step(*idx) - k0, 0, batch * 2 * nsb - 1)
        return l // (2 * nsb), (l // nsb) % 2, l % nsb

    def in_map(*idx):
        b, kv, s = loc(*idx)
        return (CH_K + kv * (CH_V - CH_K)) // A_HEADS + g, b * (seq // rb) + row0 + s, 0

    def out_map(*idx):
        b, kv, s = loc(*idx)
        return b * nsb + s, kv, 0

    return dict(copy=_heads_to_rows_copy, args=[zh], steps=batch * 2 * nsb,
                in_specs=[pl.BlockSpec((A_HEADS, rb, A_HEAD_DIM), in_map)],
                out_spec=pl.BlockSpec((rb, A_HEADS, A_HEAD_DIM), out_map),
                out_shape=jax.ShapeDtypeStruct((batch * keep, 2 * A_HEADS, A_HEAD_DIM), zh.dtype))


def _in_proj(u, us, w, col0, ncols, out_dtype, tm, tn, head_major, silu_cols, make_riders=(),
             shift_caches=()):
    m, k = u.shape
    ms = us.shape[0]
    j0 = col0 // tn
    nj, ni = ncols // tn, m // tm
    assert nj % 2 == 0

    def flat_step(p, i, c):
        return (p * ni + i) * 2 + c

    def sticky_tile(p, i, c):
        return 2 * p + jnp.where(i == 0, c, 1)

    in_specs = [pl.BlockSpec((tm, k), lambda p, i, c: (i, 0)),
                pl.BlockSpec((ms, k), lambda p, i, c: (0, 0)),
                pl.BlockSpec((k, tn), lambda p, i, c: (0, j0 + sticky_tile(p, i, c)))]
    if head_major:
        o_spec = pl.BlockSpec((tn // A_HEAD_DIM, tm, A_HEAD_DIM), lambda p, i, c: (2 * p + c, i, 0))
        o_shape = jax.ShapeDtypeStruct((ncols // A_HEAD_DIM, m, A_HEAD_DIM), out_dtype)
    else:
        o_spec = pl.BlockSpec((tm, tn), lambda p, i, c: (i, 2 * p + c))
        o_shape = jax.ShapeDtypeStruct((m, ncols), out_dtype)
    out_specs = [o_spec, pl.BlockSpec((ms, tn), lambda p, i, c: (0, sticky_tile(p, i, c)))]
    out_shape = [o_shape, jax.ShapeDtypeStruct((ms, ncols), F32)]
    riders, rider_args = [], []
    for k0, make in make_riders:
        rider = make(k0, flat_step)
        in_specs += rider["in_specs"]
        out_specs.append(rider["out_spec"])
        out_shape.append(rider["out_shape"])
        rider_args += rider["args"]
        riders.append((rider["copy"], len(rider["in_specs"]), k0, rider["steps"]))
        assert k0 + rider["steps"] <= nj * ni
    any_spec = pl.BlockSpec(memory_space=pl.ANY)
    scratch = [pltpu.VMEM((2, k, tn), BF16)]
    shift_k0s, k0 = [], 0
    for cache in shift_caches:
        in_specs.append(any_spec)
        out_specs.append(any_spec)
        out_shape.append(jax.ShapeDtypeStruct(cache.shape, cache.dtype))
        shift_k0s.append(k0)
        k0 += _shift_dma_steps(cache.shape[0], RIDER_ROWS)
    assert k0 <= nj * ni
    if shift_caches:
        scratch += [pltpu.VMEM((2, RIDER_ROWS) + shift_caches[0].shape[1:], shift_caches[0].dtype),
                    pltpu.SemaphoreType.DMA((2,)), pltpu.SemaphoreType.DMA((2,)),
                    pltpu.SemaphoreType.DMA((1,))]
    silu_tiles = ((silu_cols[0] - col0) // tn, (silu_cols[1] - col0) // tn)
    return pl.pallas_call(
        functools.partial(_in_proj_kernel, head_major=head_major, silu_tiles=silu_tiles,
                          riders=tuple(riders), shift_k0s=tuple(shift_k0s)),
        grid=(nj // 2, ni, 2),
        in_specs=in_specs,
        out_specs=out_specs,
        out_shape=out_shape,
        scratch_shapes=scratch,
        compiler_params=_params(("arbitrary", "arbitrary", "arbitrary")),
        name="in_proj_hm" if head_major else "in_proj",
    )(u, us, w, *rider_args, *shift_caches)


def _attn_kernel(slope_ref, q0, q1, q2, k0, k1, k2, v0, v1, v2, za_ref, o_ref,
                 qbuf, kbuf, vbuf, onat, lnat, *, seq):
    q_refs, k_refs, v_refs = (q0, q1, q2), (k0, k1, k2), (v0, v1, v2)
    h = pl.program_id(1)
    t = pl.program_id(2)

    @pl.when(t == 0)
    def _():
        for g, dil in enumerate(DILATIONS):
            for r in range(dil):
                run = seq // dil + BAND
                kbuf[g, pl.ds(r * run, BAND), :] = jnp.zeros((BAND, A_HEAD_DIM), BF16)
                vbuf[g, r * run // BAND] = jnp.zeros((A_HEAD_DIM, BAND), BF16)

    kj = lax.broadcasted_iota(jnp.int32, (2 * BAND, BAND), 0)
    qi = lax.broadcasted_iota(jnp.int32, (2 * BAND, BAND), 1)
    dist = qi + BAND - kj
    valid = jnp.logical_and(dist >= 0, dist <= BAND)
    distf = dist.astype(F32)
    first_pen = jnp.where(jnp.logical_and(t == 0, kj < BAND), NEG, 0.0)
    scale = A_HEAD_DIM ** -0.5

    for g in range(N_GROUPS):
        dil = DILATIONS[g]
        n = ATT_TILE // dil
        nblk = n // BAND
        run = seq // dil + BAND
        for r in range(dil):
            rows = pl.ds(r, n, stride=dil) if dil > 1 else pl.ds(0, n)
            dst = pl.ds(pl.multiple_of(r * run + BAND + t * n, BAND), n)
            qbuf[g, r * n:(r + 1) * n, :] = (q_refs[g][rows, :] * scale).astype(BF16)
            kbuf[g, dst, :] = k_refs[g][rows, :].astype(BF16)
            v = v_refs[g][rows, :]
            for i in range(nblk):
                blk = (r * run + BAND + i * BAND) // BAND + t * nblk
                vbuf[g, blk] = v[i * BAND:(i + 1) * BAND, :].T.astype(BF16)

        bias = jnp.where(valid, -(slope_ref[g, h] * float(dil)) * distf, NEG)
        bias_first = bias + first_pen

        for r in range(dil):
            for i in range(nblk):
                u = r * nblk + i
                blk = (r * run + i * BAND) // BAND + t * nblk
                kv_rows = pl.ds(pl.multiple_of(blk * BAND, BAND), 2 * BAND)
                q = qbuf[g, u * BAND:(u + 1) * BAND, :]
                s = lax.dot_general(kbuf[g, kv_rows, :], q, (((1,), (1,)), ((), ())),
                                    preferred_element_type=F32)
                s = s + (bias_first if i == 0 else bias)
                m = jnp.max(s, axis=0, keepdims=True)
                e = jnp.exp(s - m)
                den = jnp.sum(e, axis=0, keepdims=True)
                v_t = jnp.concatenate([vbuf[g, blk], vbuf[g, blk + 1]], axis=1)
                acc_t = jnp.dot(v_t, e.astype(BF16), preferred_element_type=F32)
                out_rows = (pl.ds(r + i * (BAND * dil), BAND, stride=dil) if dil > 1
                            else pl.ds(u * BAND, BAND))
                onat[g, out_rows, :] = (acc_t / den).T
                lnat[g, out_rows, :] = jnp.broadcast_to(m + jnp.log(den), (A_HEAD_DIM, BAND)).T

    l0, l1, l2 = lnat[0], lnat[1], lnat[2]
    mx = jnp.maximum(jnp.maximum(l0, l1), l2)
    w0, w1, w2 = jnp.exp(l0 - mx), jnp.exp(l1 - mx), jnp.exp(l2 - mx)
    mixed = (w0 * onat[0] + w1 * onat[1] + w2 * onat[2]) / (w0 + w1 + w2)
    o_ref[...] = (mixed * za_ref[...]).astype(o_ref.dtype)


def _attention(zh, slopes, batch, seq):
    nt = seq // ATT_TILE
    m = batch * seq

    def head_spec(c0):
        return pl.BlockSpec((None, ATT_TILE, A_HEAD_DIM), lambda b, h, t, s: (c0 + h, b * nt + t, 0))

    in_specs = [head_spec(c + g * A_HEADS) for c in (CH_Q, CH_K, CH_V) for g in range(N_GROUPS)]
    in_specs.append(head_spec(CH_ZA))
    buf_rows = seq + BAND * max(DILATIONS)
    grid_spec = pltpu.PrefetchScalarGridSpec(
        num_scalar_prefetch=1,
        grid=(batch, A_HEADS, nt),
        in_specs=in_specs,
        out_specs=pl.BlockSpec((ATT_TILE, A_HEAD_DIM), lambda b, h, t, s: (b * nt + t, h)),
        scratch_shapes=[
            pltpu.VMEM((N_GROUPS, ATT_TILE, A_HEAD_DIM), BF16),
            pltpu.VMEM((N_GROUPS, buf_rows, A_HEAD_DIM), BF16),
            pltpu.VMEM((N_GROUPS, buf_rows // BAND, A_HEAD_DIM, BAND), BF16),
            pltpu.VMEM((N_GROUPS, ATT_TILE, A_HEAD_DIM), F32),
            pltpu.VMEM((N_GROUPS, ATT_TILE, A_HEAD_DIM), F32),
        ],
    )
    return pl.pallas_call(
        functools.partial(_attn_kernel, seq=seq), grid_spec=grid_spec,
        out_shape=jax.ShapeDtypeStruct((m, A_WIDTH), BF16),
        compiler_params=_params(("parallel", "parallel", "arbitrary")),
        name="dilated_attn",
    )(slopes, *([zh] * 10))


def _retention_kernel(lg_ref, q_ref, k_ref, v_ref, zr_ref, gret_ref, y_ref, rout_ref, r_scr):
    s = pl.program_id(2)
    heads = r_scr.shape[0]

    @pl.when(s == 0)
    def _():
        r_scr[...] = jnp.zeros_like(r_scr)

    L = R_CHUNK
    pi = lax.broadcasted_iota(jnp.int32, (L, L), 0)
    pj = lax.broadcasted_iota(jnp.int32, (L, L), 1)
    rel = (pi - pj).astype(F32)
    pos = lax.broadcasted_iota(jnp.int32, (L, 1), 0).astype(F32)
    k_scale = R_DIM ** -0.5
    decays = []
    for hp in range(heads):
        log_g = lg_ref[pl.program_id(1) * heads + hp]
        decays.append((
            jnp.where(rel >= 0, jnp.exp(jnp.maximum(rel, 0.0) * log_g), 0.0) * k_scale,
            jnp.exp((pos + 1.0) * log_g),
            jnp.exp((L - 1.0 - pos) * log_g) * k_scale,
            jnp.exp(jnp.full((1, 1), float(L), F32) * log_g)))

    def chunk(c, carry):
        rows = pl.ds(pl.multiple_of(c * L, L), L)
        for hp, (intra, q_decay, k_decay, chunk_decay) in enumerate(decays):
            cols = slice(hp * R_DIM, (hp + 1) * R_DIM)
            q = q_ref[rows, cols]
            k = k_ref[rows, cols]
            v = v_ref[rows, cols]
            r_prev = r_scr[hp]
            scores = lax.dot_general(q, k, (((1,), (1,)), ((), ())), preferred_element_type=F32) * intra
            inner = jnp.dot(scores.astype(BF16), v, preferred_element_type=F32)
            cross = jnp.dot(q, r_prev.astype(BF16), preferred_element_type=F32) * q_decay
            kd = (k.astype(F32) * k_decay).astype(BF16)
            r_scr[hp] = r_prev * chunk_decay + lax.dot_general(
                kd, v, (((0,), (0,)), ((), ())), preferred_element_type=F32)
            y = inner + cross
            y = y * lax.rsqrt(jnp.mean(y * y, axis=-1, keepdims=True) + EPS)
            y = (y * gret_ref[:, cols]) * zr_ref[rows, cols].astype(F32)
            y_ref[rows, cols] = y.astype(y_ref.dtype)
        return carry

    lax.fori_loop(0, q_ref.shape[0] // L, chunk, 0, unroll=True)

    @pl.when(s == pl.num_programs(2) - 1)
    def _():
        rout_ref[...] = r_scr[...]


def _retention(zr_all, log_g, g_ret, batch, seq, rb, heads):
    ns = seq // rb
    m = batch * seq
    width = heads * R_DIM

    def col_spec(col):
        c0 = (col - HEAD_COLS) // width
        return pl.BlockSpec((rb, width), lambda b, h, s, lg: (b * ns + s, c0 + h))

    grid_spec = pltpu.PrefetchScalarGridSpec(
        num_scalar_prefetch=1,
        grid=(batch, R_HEADS // heads, ns),
        in_specs=[col_spec(COL_QR), col_spec(COL_KR), col_spec(COL_VR), col_spec(COL_ZR),
                  pl.BlockSpec((1, width), lambda b, h, s, lg: (0, h))],
        out_specs=[pl.BlockSpec((rb, width), lambda b, h, s, lg: (b * ns + s, h)),
                   pl.BlockSpec((None, heads, R_DIM, R_DIM), lambda b, h, s, lg: (b, h, 0, 0))],
        scratch_shapes=[pltpu.VMEM((heads, R_DIM, R_DIM), F32)],
    )
    return pl.pallas_call(
        _retention_kernel, grid_spec=grid_spec,
        out_shape=[jax.ShapeDtypeStruct((m, R_WIDTH), BF16),
                   jax.ShapeDtypeStruct((batch, R_HEADS, R_DIM, R_DIM), F32)],
        compiler_params=_params(("parallel", "parallel", "arbitrary")),
        name="retention",
    )(log_g, zr_all, zr_all, zr_all, zr_all, g_ret.reshape(1, R_WIDTH))


def _out_kernel(ya_ref, yr_ref, ga_ref, gb_ref, x_ref, p_ref,
                wa_ref, wr_ref, wo_ref, gpost_ref, wg_ref, wp_ref, o_ref):
    a = jnp.dot(ya_ref[...].astype(BF16), wa_ref[...], preferred_element_type=F32)
    b = jnp.dot(yr_ref[...].astype(BF16), wr_ref[...], preferred_element_type=F32)
    merged = (jax.nn.sigmoid(ga_ref[...].astype(F32)) * a
              + jax.nn.sigmoid(gb_ref[...].astype(F32)) * b)
    y = jnp.dot(merged.astype(BF16), wo_ref[...], preferred_element_type=F32)
    y = y * lax.rsqrt(jnp.mean(y * y, axis=-1, keepdims=True) + EPS)
    hres = x_ref[...] + y * gpost_ref[...]
    gate = jax.nn.sigmoid(jnp.dot(hres.astype(BF16), wg_ref[...], preferred_element_type=F32))
    emb = jnp.dot(p_ref[...].astype(BF16), wp_ref[...], preferred_element_type=F32)
    o_ref[...] = hres + gate * emb


def _out_proj(ya, yr, gates, ga_blk, gb_blk, x, p, wa, wr, wo, g_post, wg, wp, tm):
    m = x.shape[0]

    def rows(width, cb=0):
        return pl.BlockSpec((tm, width), lambda i: (i, cb))

    def whole(arr):
        return pl.BlockSpec(arr.shape, lambda i: (0, 0), pipeline_mode=pl.Buffered(1))

    gp = g_post.reshape(1, D_MODEL)
    return pl.pallas_call(
        _out_kernel,
        grid=(m // tm,),
        in_specs=[rows(A_WIDTH), rows(R_WIDTH), rows(D_MODEL, ga_blk), rows(D_MODEL, gb_blk),
                  rows(D_MODEL), rows(PLE_DIM),
                  whole(wa), whole(wr), whole(wo), whole(gp), whole(wg), whole(wp)],
        out_specs=rows(D_MODEL),
        out_shape=jax.ShapeDtypeStruct((m, D_MODEL), F32),
        compiler_params=_params(("parallel",)),
        name="out_proj",
    )(ya, yr, gates, gates, x, p, wa, wr, wo, gp, wg, wp)


def _attn_step_kernel(slope_ref, z_ref, c0_ref, c1_ref, c2_ref, o_ref):
    c_refs = (c0_ref, c1_ref, c2_ref)
    scale = A_HEAD_DIM ** -0.5
    steps = float(BAND) - lax.broadcasted_iota(jnp.int32, (BAND, 1, 1), 0).astype(F32)
    outs, lses = [], []
    for g in range(N_GROUPS):
        q = z_ref[CH_Q + g * A_HEADS:CH_Q + (g + 1) * A_HEADS, :] * scale
        kn = z_ref[CH_K + g * A_HEADS:CH_K + (g + 1) * A_HEADS, :]
        vn = z_ref[CH_V + g * A_HEADS:CH_V + (g + 1) * A_HEADS, :]
        kc = c_refs[g][:, 0]
        vc = c_refs[g][:, 1]
        slope = slope_ref[g][None]
        s = jnp.sum(kc * q[None], axis=-1, keepdims=True) - (slope * float(DILATIONS[g])) * steps
        s_new = jnp.sum(kn * q, axis=-1, keepdims=True)
        m = jnp.maximum(jnp.max(s, axis=0), s_new)
        e = jnp.exp(s - m[None])
        e_new = jnp.exp(s_new - m)
        den = jnp.sum(e, axis=0) + e_new
        outs.append((jnp.sum(e * vc, axis=0) + e_new * vn) / den)
        lses.append(m + jnp.log(den))
    mx = jnp.maximum(jnp.maximum(lses[0], lses[1]), lses[2])
    ws = [jnp.exp(l - mx) for l in lses]
    mixed = (ws[0] * outs[0] + ws[1] * outs[1] + ws[2] * outs[2]) / (ws[0] + ws[1] + ws[2])
    o_ref[...] = mixed * _silu(z_ref[CH_ZA:CH_ZA + A_HEADS, :])


def _attention_step(zs_h, caches, slopes):
    b = zs_h.shape[0]
    z3 = zs_h.reshape(b, N_HEAD_CHUNKS, A_HEAD_DIM)
    views = [c.reshape(b, BAND, DILATIONS[g], 2, A_HEADS, A_HEAD_DIM) for g, c in enumerate(caches)]
    cache_spec = pl.BlockSpec((None, BAND, None, 2, A_HEADS, A_HEAD_DIM),
                              lambda i: (i, 0, 0, 0, 0, 0))
    out = pl.pallas_call(
        _attn_step_kernel,
        grid=(b,),
        in_specs=[pl.BlockSpec((N_GROUPS, A_HEADS, 1), lambda i: (0, 0, 0)),
                  pl.BlockSpec((None, N_HEAD_CHUNKS, A_HEAD_DIM), lambda i: (i, 0, 0)),
                  cache_spec, cache_spec, cache_spec],
        out_specs=pl.BlockSpec((None, A_HEADS, A_HEAD_DIM), lambda i: (i, 0, 0)),
        out_shape=jax.ShapeDtypeStruct((b, A_HEADS, A_HEAD_DIM), F32),
        compiler_params=_params(("parallel",)),
        name="dilated_attn_step",
    )(slopes.reshape(N_GROUPS, A_HEADS, 1), z3, *views)
    return out.reshape(b, A_WIDTH)


def _retention_step_kernel(lg_ref, q_ref, k_ref, v_ref, zr_ref, gret_ref, r_ref, y_ref, rout_ref):
    for hh in range(R_HEADS):
        cols = slice(hh * R_DIM, (hh + 1) * R_DIM)
        gamma = jnp.exp(jnp.full((1, 1), 1.0, F32) * lg_ref[hh])
        q = q_ref[:, cols]
        k = k_ref[:, cols] * (R_DIM ** -0.5)
        v = v_ref[:, cols]
        r_prev = r_ref[hh]
        qb = jnp.broadcast_to(q, (8, R_DIM)).astype(BF16)
        cross = jnp.dot(qb, r_prev.astype(BF16), preferred_element_type=F32)[0:1] * gamma
        inner = jnp.sum(q * k, axis=-1, keepdims=True) * v
        k_col = jnp.broadcast_to(k, (R_DIM, R_DIM)).T
        rout_ref[hh] = r_prev * gamma + k_col * v
        y = inner + cross
        y = y * lax.rsqrt(jnp.mean(y * y, axis=-1, keepdims=True) + EPS)
        y_ref[:, cols] = (y * gret_ref[:, cols]) * _silu(zr_ref[:, cols])


def _retention_step(zs_r, state, log_g, g_ret):
    b = zs_r.shape[0]
    z3 = zs_r.reshape(b, 1, REST_COLS)

    def col_spec(col):
        return pl.BlockSpec((None, 1, R_WIDTH), lambda i, lg: (i, 0, (col - HEAD_COLS) // R_WIDTH))

    state_spec = pl.BlockSpec((None, R_HEADS, R_DIM, R_DIM), lambda i, lg: (i, 0, 0, 0))
    grid_spec = pltpu.PrefetchScalarGridSpec(
        num_scalar_prefetch=1,
        grid=(b,),
        in_specs=[col_spec(COL_QR), col_spec(COL_KR), col_spec(COL_VR), col_spec(COL_ZR),
                  pl.BlockSpec((1, R_WIDTH), lambda i, lg: (0, 0)), state_spec],
        out_specs=[pl.BlockSpec((None, 1, R_WIDTH), lambda i, lg: (i, 0, 0)), state_spec],
    )
    y, new_state = pl.pallas_call(
        _retention_step_kernel, grid_spec=grid_spec,
        out_shape=[jax.ShapeDtypeStruct((b, 1, R_WIDTH), F32),
                   jax.ShapeDtypeStruct(state.shape, F32)],
        compiler_params=_params(("parallel",)),
        name="retention_step",
    )(log_g, z3, z3, z3, z3, g_ret.reshape(1, R_WIDTH), state)
    return y.reshape(b, R_WIDTH), new_state


def _window_new_row_kernel(z_ref, c0, c1, c2, o0, o1, o2):
    del c0, c1, c2
    for g, o_ref in enumerate((o0, o1, o2)):
        o_ref[0, 0] = z_ref[CH_K + g * A_HEADS:CH_K + (g + 1) * A_HEADS, :]
        o_ref[0, 1] = z_ref[CH_V + g * A_HEADS:CH_V + (g + 1) * A_HEADS, :]


def _window_new_row(zs_h, shifted):
    b = zs_h.shape[0]
    tile = (2, A_HEADS, A_HEAD_DIM)
    last_row = [pl.BlockSpec((None, 1) + tile, functools.partial(lambda i, w: (i, w - 1, 0, 0, 0), w=c.shape[1]))
                for c in shifted]
    return pl.pallas_call(
        _window_new_row_kernel,
        grid=(b,),
        in_specs=[pl.BlockSpec((None, N_HEAD_CHUNKS, A_HEAD_DIM), lambda i: (i, 0, 0))]
                 + [pl.BlockSpec(memory_space=pl.ANY)] * len(shifted),
        out_specs=last_row,
        out_shape=[jax.ShapeDtypeStruct(c.shape, c.dtype) for c in shifted],
        input_output_aliases={1: 0, 2: 1, 3: 2},
        compiler_params=_params(("parallel",)),
        name="window_new_row",
    )(zs_h.reshape(b, N_HEAD_CHUNKS, A_HEAD_DIM), *shifted)


def _alibi_slopes():
    n = N_GROUPS * A_HEADS
    return jnp.exp2(-8.0 * (jnp.arange(n, dtype=F32) + 1.0) / n).reshape(N_GROUPS, A_HEADS)


def _retention_log_decay():
    return jnp.log1p(-jnp.exp2(-5.0 - jnp.arange(R_HEADS, dtype=F32)))


def kernel(x_prompt, x_sample, cache_win0, cache_win1, cache_win2, state_ret, p_prompt, p_sample,
           g_pre, w_in, g_ret, w_a_out, w_r_out, w_o, g_post, w_ple_gate, w_ple_proj):
    batch, seq, _ = x_prompt.shape
    dec_batch = x_sample.shape[0]
    assert g_pre.shape[0] == 1 and x_sample.shape[1] == 1
    assert seq % ATT_TILE == 0
    m = batch * seq
    slopes = _alibi_slopes()
    log_g = _retention_log_decay()

    wa, wr, wo = w_a_out[0].astype(BF16), w_r_out[0].astype(BF16), w_o[0].astype(BF16)
    wg, wp = w_ple_gate[0].astype(BF16), w_ple_proj[0].astype(BF16)

    xp = x_prompt.reshape(m, D_MODEL)
    xs = x_sample.reshape(dec_batch, D_MODEL)
    u = _rmsnorm(xp, g_pre[0], TM_NORM)
    us = _rmsnorm(xs, g_pre[0], dec_batch)
    caches = (cache_win0[0], cache_win1[0], cache_win2[0])
    flat = [c.reshape((-1,) + c.shape[2:]) for c in caches]
    zh, zs_h, sh1, sh0 = _in_proj(u, us, w_in[0], 0, HEAD_COLS, F32, TM_IN, TN_IN, True,
                                  (COL_ZA, COL_QR), shift_caches=[flat[1], flat[0]])
    layout = [functools.partial(_window_prompt_rider, zh, g, batch, seq, RIDER_ROWS) for g in range(N_GROUPS)]
    np0 = batch * 2 * (min(WINDOWS[0], seq) // min(RIDER_ROWS, WINDOWS[0]))
    np1 = batch * 2 * (min(WINDOWS[1], seq) // min(RIDER_ROWS, WINDOWS[1]))
    zr, zs_r, wp2, wp1, wp0, sh2 = _in_proj(
        u, us, w_in[0], HEAD_COLS, REST_COLS, BF16, TM_IN, TN_IN, False, (COL_ZR, COL_GA),
        [(np0 + np1, layout[2]), (np0, layout[1]), (0, layout[0])], shift_caches=[flat[2]])
    win_p = [wp.reshape(1, batch, -1, 2, A_HEADS, A_HEAD_DIM) for wp in (wp0, wp1, wp2)]
    shifted = [s.reshape(c.shape) for s, c in zip((sh0, sh1, sh2), caches)]
    ga_blk = (COL_GA - HEAD_COLS) // D_MODEL

    ya = _attention(zh, slopes, batch, seq)
    yr, ret_prompt = _retention(zr, log_g, g_ret[0], batch, seq, RET_ROWS, RET_HEADS)
    y_prompt = _out_proj(ya, yr, zr, ga_blk, ga_blk + 1, xp, p_prompt[0].reshape(m, PLE_DIM),
                         wa, wr, wo, g_post[0], wg, wp, TM_OUT).reshape(batch, seq, D_MODEL)

    ya_s = _attention_step(zs_h, caches, slopes)
    yr_s, ret_sample = _retention_step(zs_r, state_ret[0], log_g, g_ret[0])
    y_sample = _out_proj(ya_s, yr_s, zs_r, ga_blk, ga_blk + 1, xs, p_sample[0].reshape(dec_batch, PLE_DIM),
                         wa, wr, wo, g_post[0], wg, wp, dec_batch).reshape(dec_batch, 1, D_MODEL)

    win_s = _window_new_row(zs_h, shifted)
    return (y_prompt, y_sample, win_p[0], win_p[1], win_p[2], ret_prompt[None],
            win_s[0][None], win_s[1][None], win_s[2][None], ret_sample[None])
```

```python
import functools

import jax
import jax.numpy as jnp
from jax import lax
from jax.experimental import pallas as pl
from jax.experimental.pallas import tpu as pltpu

F32 = jnp.float32
BF16 = jnp.bfloat16

D_MODEL = 2048
N_GROUPS = 3
DILATIONS = (1, 4, 16)
WINDOWS = (128, 512, 2048)
BAND = 128
A_HEADS = 8
A_HEAD_DIM = 128
A_QKV = N_GROUPS * A_HEADS * A_HEAD_DIM
A_WIDTH = A_HEADS * A_HEAD_DIM
R_HEADS = 8
R_DIM = 256
R_WIDTH = R_HEADS * R_DIM
R_CHUNK = 128
PLE_DIM = 256
EPS = 1e-6
N_IN = 3 * A_QKV + A_WIDTH + 4 * R_WIDTH + 2 * D_MODEL

COL_QA, COL_KA, COL_VA = 0, A_QKV, 2 * A_QKV
COL_ZA = 3 * A_QKV
COL_QR = COL_ZA + A_WIDTH
COL_KR = COL_QR + R_WIDTH
COL_VR = COL_KR + R_WIDTH
COL_ZR = COL_VR + R_WIDTH
COL_GA = COL_ZR + R_WIDTH
COL_GB = COL_GA + D_MODEL

HEAD_COLS = COL_QR
REST_COLS = N_IN - HEAD_COLS
CH_Q, CH_K, CH_V, CH_ZA = (c // A_HEAD_DIM for c in (COL_QA, COL_KA, COL_VA, COL_ZA))
N_HEAD_CHUNKS = HEAD_COLS // A_HEAD_DIM

NEG = -1e30
ATT_TILE = BAND * max(DILATIONS)
SPLIT_STRIDE = 4
assert all(d == 1 or d % SPLIT_STRIDE == 0 for d in DILATIONS)

assert R_DIM ** -0.5 == 2.0 ** -4

TM_NORM = 512
TM_IN, TN_IN = 1024, 1024
RIDER_ROWS = 256
RET_ROWS = 1024
RET_HEADS = 4
TM_OUT = 256
VMEM_LIMIT = 60 * 1024 * 1024


def _params(semantics, vmem=VMEM_LIMIT):
    return pltpu.CompilerParams(dimension_semantics=semantics, vmem_limit_bytes=vmem)


def _silu(x):
    return x * jax.nn.sigmoid(x)


def _rmsnorm_kernel(x_ref, g_ref, o_ref):
    x = x_ref[...]
    y = x * lax.rsqrt(jnp.mean(x * x, axis=-1, keepdims=True) + EPS)
    o_ref[...] = (y * g_ref[...]).astype(o_ref.dtype)


def _rmsnorm(x, g, tm):
    m, d = x.shape
    return pl.pallas_call(
        _rmsnorm_kernel,
        grid=(m // tm,),
        in_specs=[pl.BlockSpec((tm, d), lambda i: (i, 0)),
                  pl.BlockSpec((1, d), lambda i: (0, 0))],
        out_specs=pl.BlockSpec((tm, d), lambda i: (i, 0)),
        out_shape=jax.ShapeDtypeStruct((m, d), BF16),
        compiler_params=_params(("parallel",)),
        name="rmsnorm",
    )(x, g.reshape(1, d))


def _heads_to_rows_copy(in_ref, out_ref):
    for hh in range(A_HEADS):
        out_ref[:, hh, :] = in_ref[hh]


def _shift_dma_step(step, cache_ref, out_ref, buf, rsem, wsem, tsem, k0, rb):
    rows = cache_ref.shape[0]
    chunks = -(-(rows - 1) // rb)
    ci = step - k0

    def start_row(n):
        return jnp.minimum(n * rb, rows - 1 - rb)

    def read(n):
        return pltpu.make_async_copy(cache_ref.at[pl.ds(start_row(n) + 1, rb)], buf.at[n % 2], rsem.at[n % 2])

    def write(n):
        return pltpu.make_async_copy(buf.at[n % 2], out_ref.at[pl.ds(start_row(n), rb)], wsem.at[n % 2])

    tail = pltpu.make_async_copy(cache_ref.at[pl.ds(rows - 1, 1)], out_ref.at[pl.ds(rows - 1, 1)], tsem.at[0])

    @pl.when(ci == 0)
    def _():
        read(ci).start()
        tail.start()

    @pl.when(ci == chunks)
    def _():
        tail.wait()

    @pl.when(jnp.logical_and(ci >= 1, ci <= chunks))
    def _():
        write(ci - 1).wait()

    @pl.when(jnp.logical_and(ci >= 0, ci < chunks))
    def _():
        read(ci).wait()
        write(ci).start()

    @pl.when(jnp.logical_and(ci >= 0, ci + 1 < chunks))
    def _():
        read(ci + 1).start()


def _shift_dma_steps(rows, rb):
    return -(-(rows - 1) // rb) + 1


def _in_proj_kernel(*refs, head_major, silu_tiles, riders, shift_k0s):
    n_in = sum(r[1] for r in riders)
    ns = len(shift_k0s)
    u_ref, us_ref, w_ref = refs[:3]
    rider_in = refs[3:3 + n_in]
    shift_in = refs[3 + n_in:3 + n_in + ns]
    outs = refs[3 + n_in + ns:]
    o_ref, os_ref = outs[:2]
    rider_out = outs[2:2 + len(riders)]
    shift_out = outs[2 + len(riders):2 + len(riders) + ns]
    wb_ref = outs[2 + len(riders) + ns]
    c = pl.program_id(2)
    j = 2 * pl.program_id(0) + c

    @pl.when(pl.program_id(1) == 0)
    def _():
        wb_ref[c] = w_ref[...].astype(BF16)
        os_ref[...] = jnp.dot(us_ref[...], wb_ref[c], preferred_element_type=F32)

    def project(act):
        acc = jnp.dot(u_ref[...], wb_ref[c], preferred_element_type=F32)
        if act is not None:
            acc = act(acc)
        if head_major:
            for ch in range(o_ref.shape[0]):
                o_ref[ch] = acc[:, ch * A_HEAD_DIM:(ch + 1) * A_HEAD_DIM].astype(o_ref.dtype)
        else:
            o_ref[...] = acc.astype(o_ref.dtype)

    gated = jnp.logical_and(j >= silu_tiles[0], j < silu_tiles[1])
    pl.when(gated)(functools.partial(project, _silu))
    pl.when(jnp.logical_not(gated))(functools.partial(project, None))

    step = (pl.program_id(0) * pl.num_programs(1) + pl.program_id(1)) * 2 + c
    pos = 0
    for (copy, nin, k0, nsteps), out_ref in zip(riders, rider_out):
        active = jnp.logical_and(step >= k0, step < k0 + nsteps)
        pl.when(active)(functools.partial(copy, *rider_in[pos:pos + nin], out_ref))
        pos += nin

    if ns:
        buf, rsem, wsem, tsem = outs[3 + len(riders) + ns:7 + len(riders) + ns]
        for cache_ref, out_ref, k0 in zip(shift_in, shift_out, shift_k0s):
            _shift_dma_step(step, cache_ref, out_ref, buf, rsem, wsem, tsem, k0, buf.shape[1])


def _window_prompt_rider(zh, g, batch, seq, rb, k0, flat_step):
    keep = min(WINDOWS[g], seq)
    rb = min(rb, keep)
    nsb = keep // rb
    row0 = (seq - keep) // rb

    def loc(*idx):
        l = jnp.clip(flat_step(*idx) - k0, 0, batch * 2 * nsb - 1)
        return l // (2 * nsb), (l // nsb) % 2, l % nsb

    def in_map(*idx):
        b, kv, s = loc(*idx)
        return (CH_K + kv * (CH_V - CH_K)) // A_HEADS + g, b * (seq // rb) + row0 + s, 0

    def out_map(*idx):
        b, kv, s = loc(*idx)
        return b * nsb + s, kv, 0

    return dict(copy=_heads_to_rows_copy, args=[zh], steps=batch * 2 * nsb,
                in_specs=[pl.BlockSpec((A_HEADS, rb, A_HEAD_DIM), in_map)],
                out_spec=pl.BlockSpec((rb, A_HEADS, A_HEAD_DIM), out_map),
                out_shape=jax.ShapeDtypeStruct((batch * keep, 2 * A_HEADS, A_HEAD_DIM), zh.dtype))


def _in_proj(u, us, w, col0, ncols, out_dtype, tm, tn, head_major, silu_cols, make_riders=(),
             shift_caches=()):
    m, k = u.shape
    ms = us.shape[0]
    j0 = col0 // tn
    nj, ni = ncols // tn, m // tm
    assert nj % 2 == 0

    def flat_step(p, i, c):
        return (p * ni + i) * 2 + c

    def sticky_tile(p, i, c):
        return 2 * p + jnp.where(i == 0, c, 1)

    in_specs = [pl.BlockSpec((tm, k), lambda p, i, c: (i, 0)),
                pl.BlockSpec((ms, k), lambda p, i, c: (0, 0)),
                pl.BlockSpec((k, tn), lambda p, i, c: (0, j0 + sticky_tile(p, i, c)))]
    if head_major:
        o_spec = pl.BlockSpec((tn // A_HEAD_DIM, tm, A_HEAD_DIM), lambda p, i, c: (2 * p + c, i, 0))
        o_shape = jax.ShapeDtypeStruct((ncols // A_HEAD_DIM, m, A_HEAD_DIM), out_dtype)
    else:
        o_spec = pl.BlockSpec((tm, tn), lambda p, i, c: (i, 2 * p + c))
        o_shape = jax.ShapeDtypeStruct((m, ncols), out_dtype)
    out_specs = [o_spec, pl.BlockSpec((ms, tn), lambda p, i, c: (0, sticky_tile(p, i, c)))]
    out_shape = [o_shape, jax.ShapeDtypeStruct((ms, ncols), F32)]
    riders, rider_args = [], []
    for k0, make in make_riders:
        rider = make(k0, flat_step)
        in_specs += rider["in_specs"]
        out_specs.append(rider["out_spec"])
        out_shape.append(rider["out_shape"])
        rider_args += rider["args"]
        riders.append((rider["copy"], len(rider["in_specs"]), k0, rider["steps"]))
        assert k0 + rider["steps"] <= nj * ni
    any_spec = pl.BlockSpec(memory_space=pl.ANY)
    scratch = [pltpu.VMEM((2, k, tn), BF16)]
    shift_k0s, k0 = [], 0
    for cache in shift_caches:
        in_specs.append(any_spec)
        out_specs.append(any_spec)
        out_shape.append(jax.ShapeDtypeStruct(cache.shape, cache.dtype))
        shift_k0s.append(k0)
        k0 += _shift_dma_steps(cache.shape[0], RIDER_ROWS)
    assert k0 <= nj * ni
    if shift_caches:
        scratch += [pltpu.VMEM((2, RIDER_ROWS) + shift_caches[0].shape[1:], shift_caches[0].dtype),
                    pltpu.SemaphoreType.DMA((2,)), pltpu.SemaphoreType.DMA((2,)),
                    pltpu.SemaphoreType.DMA((1,))]
    silu_tiles = ((silu_cols[0] - col0) // tn, (silu_cols[1] - col0) // tn)
    return pl.pallas_call(
        functools.partial(_in_proj_kernel, head_major=head_major, silu_tiles=silu_tiles,
                          riders=tuple(riders), shift_k0s=tuple(shift_k0s)),
        grid=(nj // 2, ni, 2),
        in_specs=in_specs,
        out_specs=out_specs,
        out_shape=out_shape,
        scratch_shapes=scratch,
        compiler_params=_params(("arbitrary", "arbitrary", "arbitrary")),
        name="in_proj_hm" if head_major else "in_proj",
    )(u, us, w, *rider_args, *shift_caches)


def _attn_kernel(slope_ref, q0, q1, q2, k0, k1, k2, v0, v1, v2, za_ref, o_ref,
                 qbuf, kbuf, vbuf, onat, lnat, stage, *, seq):
    q_refs, k_refs, v_refs = (q0, q1, q2), (k0, k1, k2), (v0, v1, v2)
    h = pl.program_id(1)
    t = pl.program_id(2)

    @pl.when(t == 0)
    def _():
        for g, dil in enumerate(DILATIONS):
            for r in range(dil):
                run = seq // dil + BAND
                kbuf[g, pl.ds(r * run, BAND), :] = jnp.zeros((BAND, A_HEAD_DIM), BF16)
                vbuf[g, r * run // BAND] = jnp.zeros((A_HEAD_DIM, BAND), BF16)

    kj = lax.broadcasted_iota(jnp.int32, (2 * BAND, BAND), 0)
    qi = lax.broadcasted_iota(jnp.int32, (2 * BAND, BAND), 1)
    dist = qi + BAND - kj
    valid = jnp.logical_and(dist >= 0, dist <= BAND)
    distf = dist.astype(F32)
    first_pen = jnp.where(jnp.logical_and(t == 0, kj < BAND), NEG, 0.0)
    scale = A_HEAD_DIM ** -0.5

    for g in range(N_GROUPS):
        dil = DILATIONS[g]
        n = ATT_TILE // dil
        nblk = n // BAND
        run = seq // dil + BAND
        srcs = (q_refs[g], k_refs[g], v_refs[g])
        if dil > SPLIT_STRIDE:
            coarse = ATT_TILE // SPLIT_STRIDE
            for a, src in enumerate(srcs):
                for r4 in range(SPLIT_STRIDE):
                    stage[a, r4 * coarse:(r4 + 1) * coarse, :] = src[pl.ds(r4, coarse, stride=SPLIT_STRIDE), :]
            srcs = (stage.at[0], stage.at[1], stage.at[2])
        for r in range(dil):
            if dil > SPLIT_STRIDE:
                rows = pl.ds((r % SPLIT_STRIDE) * coarse + r // SPLIT_STRIDE, n, stride=dil // SPLIT_STRIDE)
            else:
                rows = pl.ds(r, n, stride=dil) if dil > 1 else pl.ds(0, n)
            dst = pl.ds(pl.multiple_of(r * run + BAND + t * n, BAND), n)
            qbuf[g, r * n:(r + 1) * n, :] = (srcs[0][rows, :] * scale).astype(BF16)
            kbuf[g, dst, :] = srcs[1][rows, :].astype(BF16)
            v = srcs[2][rows, :]
            for i in range(nblk):
                blk = (r * run + BAND + i * BAND) // BAND + t * nblk
                vbuf[g, blk] = v[i * BAND:(i + 1) * BAND, :].T.astype(BF16)

        bias = jnp.where(valid, -(slope_ref[g, h] * float(dil)) * distf, NEG)
        bias_first = bias + first_pen

        for r in range(dil):
            for i in range(nblk):
                u = r * nblk + i
                blk = (r * run + i * BAND) // BAND + t * nblk
                kv_rows = pl.ds(pl.multiple_of(blk * BAND, BAND), 2 * BAND)
                q = qbuf[g, u * BAND:(u + 1) * BAND, :]
                s = lax.dot_general(kbuf[g, kv_rows, :], q, (((1,), (1,)), ((), ())),
                                    preferred_element_type=F32)
                s = s + (bias_first if i == 0 else bias)
                m = jnp.max(s, axis=0, keepdims=True)
                e = jnp.exp(s - m)
                den = jnp.sum(e, axis=0, keepdims=True)
                v_t = jnp.concatenate([vbuf[g, blk], vbuf[g, blk + 1]], axis=1)
                acc_t = jnp.dot(v_t, e.astype(BF16), preferred_element_type=F32)
                if dil == 1:
                    out_rows = pl.ds(u * BAND, BAND)
                else:
                    fine = dil // SPLIT_STRIDE
                    start = (r % SPLIT_STRIDE) * (ATT_TILE // SPLIT_STRIDE) + r // SPLIT_STRIDE + i * BAND * fine
                    out_rows = pl.ds(start, BAND, stride=fine) if fine > 1 else pl.ds(start, BAND)
                onat[g, out_rows, :] = (acc_t / den).T
                lnat[g, out_rows, :] = jnp.broadcast_to(m + jnp.log(den), (A_HEAD_DIM, BAND)).T

    quarter = ATT_TILE // SPLIT_STRIDE
    for r4 in range(SPLIT_STRIDE):
        strided = pl.ds(r4, quarter, stride=SPLIT_STRIDE)
        grouped = pl.ds(r4 * quarter, quarter)
        ls = [lnat[g, strided if dil == 1 else grouped, :] for g, dil in enumerate(DILATIONS)]
        os_ = [onat[g, strided if dil == 1 else grouped, :] for g, dil in enumerate(DILATIONS)]
        mx = jnp.maximum(jnp.maximum(ls[0], ls[1]), ls[2])
        w0, w1, w2 = jnp.exp(ls[0] - mx), jnp.exp(ls[1] - mx), jnp.exp(ls[2] - mx)
        mixed = (w0 * os_[0] + w1 * os_[1] + w2 * os_[2]) / (w0 + w1 + w2)
        o_ref[strided, :] = mixed * za_ref[strided, :]


def _attention(zh, slopes, batch, seq):
    nt = seq // ATT_TILE
    m = batch * seq

    def head_spec(c0):
        return pl.BlockSpec((None, ATT_TILE, A_HEAD_DIM), lambda b, h, t, s: (c0 + h, b * nt + t, 0))

    in_specs = [head_spec(c + g * A_HEADS) for c in (CH_Q, CH_K, CH_V) for g in range(N_GROUPS)]
    in_specs.append(head_spec(CH_ZA))
    buf_rows = seq + BAND * max(DILATIONS)
    grid_spec = pltpu.PrefetchScalarGridSpec(
        num_scalar_prefetch=1,
        grid=(batch, A_HEADS, nt),
        in_specs=in_specs,
        out_specs=pl.BlockSpec((ATT_TILE, A_HEAD_DIM), lambda b, h, t, s: (b * nt + t, h)),
        scratch_shapes=[
            pltpu.VMEM((N_GROUPS, ATT_TILE, A_HEAD_DIM), BF16),
            pltpu.VMEM((N_GROUPS, buf_rows, A_HEAD_DIM), BF16),
            pltpu.VMEM((N_GROUPS, buf_rows // BAND, A_HEAD_DIM, BAND), BF16),
            pltpu.VMEM((N_GROUPS, ATT_TILE, A_HEAD_DIM), F32),
            pltpu.VMEM((N_GROUPS, ATT_TILE, A_HEAD_DIM), F32),
            pltpu.VMEM((3, ATT_TILE, A_HEAD_DIM), F32),
        ],
    )
    return pl.pallas_call(
        functools.partial(_attn_kernel, seq=seq), grid_spec=grid_spec,
        out_shape=jax.ShapeDtypeStruct((m, A_WIDTH), F32),
        compiler_params=_params(("parallel", "parallel", "arbitrary")),
        name="dilated_attn",
    )(slopes, *([zh] * 10))


def _retention_kernel(lg_ref, q_ref, k_ref, v_ref, zr_ref, gret_ref, y_ref, rout_ref, r_scr):
    s = pl.program_id(2)
    heads = r_scr.shape[0]

    @pl.when(s == 0)
    def _():
        r_scr[...] = jnp.zeros_like(r_scr)

    L = R_CHUNK
    pi = lax.broadcasted_iota(jnp.int32, (L, L), 0)
    pj = lax.broadcasted_iota(jnp.int32, (L, L), 1)
    rel = (pi - pj).astype(F32)
    pos = lax.broadcasted_iota(jnp.int32, (L, 1), 0).astype(F32)
    k_scale = R_DIM ** -0.5
    decays = []
    for hp in range(heads):
        log_g = lg_ref[pl.program_id(1) * heads + hp]
        decays.append((
            jnp.where(rel >= 0, jnp.exp(jnp.maximum(rel, 0.0) * log_g), 0.0) * k_scale,
            jnp.exp((pos + 1.0) * log_g),
            jnp.exp((L - 1.0 - pos) * log_g) * k_scale,
            jnp.exp(jnp.full((1, 1), float(L), F32) * log_g)))

    def chunk(c, carry):
        rows = pl.ds(pl.multiple_of(c * L, L), L)
        for hp, (intra, q_decay, k_decay, chunk_decay) in enumerate(decays):
            cols = slice(hp * R_DIM, (hp + 1) * R_DIM)
            q = q_ref[rows, cols]
            k = k_ref[rows, cols]
            v = v_ref[rows, cols]
            r_prev = r_scr[hp]
            scores = lax.dot_general(q, k, (((1,), (1,)), ((), ())), preferred_element_type=F32) * intra
            inner = jnp.dot(scores.astype(BF16), v, preferred_element_type=F32)
            cross = jnp.dot(q, r_prev.astype(BF16), preferred_element_type=F32) * q_decay
            kd = (k.astype(F32) * k_decay).astype(BF16)
            r_scr[hp] = r_prev * chunk_decay + lax.dot_general(
                kd, v, (((0,), (0,)), ((), ())), preferred_element_type=F32)
            y = inner + cross
            y = y * lax.rsqrt(jnp.mean(y * y, axis=-1, keepdims=True) + EPS)
            y = (y * gret_ref[:, cols]) * zr_ref[rows, cols].astype(F32)
            y_ref[rows, cols] = y.astype(y_ref.dtype)
        return carry

    lax.fori_loop(0, q_ref.shape[0] // L, chunk, 0, unroll=True)

    @pl.when(s == pl.num_programs(2) - 1)
    def _():
        rout_ref[...] = r_scr[...]


def _retention(zr_all, log_g, g_ret, batch, seq, rb, heads):
    ns = seq // rb
    m = batch * seq
    width = heads * R_DIM

    def col_spec(col):
        c0 = (col - HEAD_COLS) // width
        return pl.BlockSpec((rb, width), lambda b, h, s, lg: (b * ns + s, c0 + h))

    grid_spec = pltpu.PrefetchScalarGridSpec(
        num_scalar_prefetch=1,
        grid=(batch, R_HEADS // heads, ns),
        in_specs=[col_spec(COL_QR), col_spec(COL_KR), col_spec(COL_VR), col_spec(COL_ZR),
                  pl.BlockSpec((1, width), lambda b, h, s, lg: (0, h))],
        out_specs=[pl.BlockSpec((rb, width), lambda b, h, s, lg: (b * ns + s, h)),
                   pl.BlockSpec((None, heads, R_DIM, R_DIM), lambda b, h, s, lg: (b, h, 0, 0))],
        scratch_shapes=[pltpu.VMEM((heads, R_DIM, R_DIM), F32)],
    )
    return pl.pallas_call(
        _retention_kernel, grid_spec=grid_spec,
        out_shape=[jax.ShapeDtypeStruct((m, R_WIDTH), BF16),
                   jax.ShapeDtypeStruct((batch, R_HEADS, R_DIM, R_DIM), F32)],
        compiler_params=_params(("parallel", "parallel", "arbitrary")),
        name="retention",
    )(log_g, zr_all, zr_all, zr_all, zr_all, g_ret.reshape(1, R_WIDTH))


def _out_kernel(ya_ref, yr_ref, ga_ref, gb_ref, x_ref, p_ref,
                wa_ref, wr_ref, wo_ref, gpost_ref, wg_ref, wp_ref, o_ref):
    a = jnp.dot(ya_ref[...].astype(BF16), wa_ref[...], preferred_element_type=F32)
    b = jnp.dot(yr_ref[...].astype(BF16), wr_ref[...], preferred_element_type=F32)
    merged = (jax.nn.sigmoid(ga_ref[...].astype(F32)) * a
              + jax.nn.sigmoid(gb_ref[...].astype(F32)) * b)
    y = jnp.dot(merged.astype(BF16), wo_ref[...], preferred_element_type=F32)
    y = y * lax.rsqrt(jnp.mean(y * y, axis=-1, keepdims=True) + EPS)
    hres = x_ref[...] + y * gpost_ref[...]
    gate = jax.nn.sigmoid(jnp.dot(hres.astype(BF16), wg_ref[...], preferred_element_type=F32))
    emb = jnp.dot(p_ref[...].astype(BF16), wp_ref[...], preferred_element_type=F32)
    o_ref[...] = hres + gate * emb


def _out_proj(ya, yr, gates, ga_blk, gb_blk, x, p, wa, wr, wo, g_post, wg, wp, tm):
    m = x.shape[0]

    def rows(width, cb=0):
        return pl.BlockSpec((tm, width), lambda i: (i, cb))

    def whole(arr):
        return pl.BlockSpec(arr.shape, lambda i: (0, 0), pipeline_mode=pl.Buffered(1))

    gp = g_post.reshape(1, D_MODEL)
    return pl.pallas_call(
        _out_kernel,
        grid=(m // tm,),
        in_specs=[rows(A_WIDTH), rows(R_WIDTH), rows(D_MODEL, ga_blk), rows(D_MODEL, gb_blk),
                  rows(D_MODEL), rows(PLE_DIM),
                  whole(wa), whole(wr), whole(wo), whole(gp), whole(wg), whole(wp)],
        out_specs=rows(D_MODEL),
        out_shape=jax.ShapeDtypeStruct((m, D_MODEL), F32),
        compiler_params=_params(("parallel",)),
        name="out_proj",
    )(ya, yr, gates, gates, x, p, wa, wr, wo, gp, wg, wp)


def _attn_step_kernel(slope_ref, z_ref, c0_ref, c1_ref, c2_ref, o_ref):
    c_refs = (c0_ref, c1_ref, c2_ref)
    scale = A_HEAD_DIM ** -0.5
    steps = float(BAND) - lax.broadcasted_iota(jnp.int32, (BAND, 1, 1), 0).astype(F32)
    outs, lses = [], []
    for g in range(N_GROUPS):
        q = z_ref[CH_Q + g * A_HEADS:CH_Q + (g + 1) * A_HEADS, :] * scale
        kn = z_ref[CH_K + g * A_HEADS:CH_K + (g + 1) * A_HEADS, :]
        vn = z_ref[CH_V + g * A_HEADS:CH_V + (g + 1) * A_HEADS, :]
        kc = c_refs[g][:, 0]
        vc = c_refs[g][:, 1]
        slope = slope_ref[g][None]
        s = jnp.sum(kc * q[None], axis=-1, keepdims=True) - (slope * float(DILATIONS[g])) * steps
        s_new = jnp.sum(kn * q, axis=-1, keepdims=True)
        m = jnp.maximum(jnp.max(s, axis=0), s_new)
        e = jnp.exp(s - m[None])
        e_new = jnp.exp(s_new - m)
        den = jnp.sum(e, axis=0) + e_new
        outs.append((jnp.sum(e * vc, axis=0) + e_new * vn) / den)
        lses.append(m + jnp.log(den))
    mx = jnp.maximum(jnp.maximum(lses[0], lses[1]), lses[2])
    ws = [jnp.exp(l - mx) for l in lses]
    mixed = (ws[0] * outs[0] + ws[1] * outs[1] + ws[2] * outs[2]) / (ws[0] + ws[1] + ws[2])
    o_ref[...] = mixed * _silu(z_ref[CH_ZA:CH_ZA + A_HEADS, :])


def _attention_step(zs_h, caches, slopes):
    b = zs_h.shape[0]
    z3 = zs_h.reshape(b, N_HEAD_CHUNKS, A_HEAD_DIM)
    views = [c.reshape(b, BAND, DILATIONS[g], 2, A_HEADS, A_HEAD_DIM) for g, c in enumerate(caches)]
    cache_spec = pl.BlockSpec((None, BAND, None, 2, A_HEADS, A_HEAD_DIM),
                              lambda i: (i, 0, 0, 0, 0, 0))
    out = pl.pallas_call(
        _attn_step_kernel,
        grid=(b,),
        in_specs=[pl.BlockSpec((N_GROUPS, A_HEADS, 1), lambda i: (0, 0, 0)),
                  pl.BlockSpec((None, N_HEAD_CHUNKS, A_HEAD_DIM), lambda i: (i, 0, 0)),
                  cache_spec, cache_spec, cache_spec],
        out_specs=pl.BlockSpec((None, A_HEADS, A_HEAD_DIM), lambda i: (i, 0, 0)),
        out_shape=jax.ShapeDtypeStruct((b, A_HEADS, A_HEAD_DIM), F32),
        compiler_params=_params(("parallel",)),
        name="dilated_attn_step",
    )(slopes.reshape(N_GROUPS, A_HEADS, 1), z3, *views)
    return out.reshape(b, A_WIDTH)


def _retention_step_kernel(lg_ref, q_ref, k_ref, v_ref, zr_ref, gret_ref, r_ref, y_ref, rout_ref):
    for hh in range(R_HEADS):
        cols = slice(hh * R_DIM, (hh + 1) * R_DIM)
        gamma = jnp.exp(jnp.full((1, 1), 1.0, F32) * lg_ref[hh])
        q = q_ref[:, cols]
        k = k_ref[:, cols] * (R_DIM ** -0.5)
        v = v_ref[:, cols]
        r_prev = r_ref[hh]
        qb = jnp.broadcast_to(q, (8, R_DIM)).astype(BF16)
        cross = jnp.dot(qb, r_prev.astype(BF16), preferred_element_type=F32)[0:1] * gamma
        inner = jnp.sum(q * k, axis=-1, keepdims=True) * v
        k_col = jnp.broadcast_to(k, (R_DIM, R_DIM)).T
        rout_ref[hh] = r_prev * gamma + k_col * v
        y = inner + cross
        y = y * lax.rsqrt(jnp.mean(y * y, axis=-1, keepdims=True) + EPS)
        y_ref[:, cols] = (y * gret_ref[:, cols]) * _silu(zr_ref[:, cols])


def _retention_step(zs_r, state, log_g, g_ret):
    b = zs_r.shape[0]
    z3 = zs_r.reshape(b, 1, REST_COLS)

    def col_spec(col):
        return pl.BlockSpec((None, 1, R_WIDTH), lambda i, lg: (i, 0, (col - HEAD_COLS) // R_WIDTH))

    state_spec = pl.BlockSpec((None, R_HEADS, R_DIM, R_DIM), lambda i, lg: (i, 0, 0, 0))
    grid_spec = pltpu.PrefetchScalarGridSpec(
        num_scalar_prefetch=1,
        grid=(b,),
        in_specs=[col_spec(COL_QR), col_spec(COL_KR), col_spec(COL_VR), col_spec(COL_ZR),
                  pl.BlockSpec((1, R_WIDTH), lambda i, lg: (0, 0)), state_spec],
        out_specs=[pl.BlockSpec((None, 1, R_WIDTH), lambda i, lg: (i, 0, 0)), state_spec],
    )
    y, new_state = pl.pallas_call(
        _retention_step_kernel, grid_spec=grid_spec,
        out_shape=[jax.ShapeDtypeStruct((b, 1, R_WIDTH), F32),
                   jax.ShapeDtypeStruct(state.shape, F32)],
        compiler_params=_params(("parallel",)),
        name="retention_step",
    )(log_g, z3, z3, z3, z3, g_ret.reshape(1, R_WIDTH), state)
    return y.reshape(b, R_WIDTH), new_state


def _window_new_row_kernel(z_ref, c0, c1, c2, o0, o1, o2):
    del c0, c1, c2
    for g, o_ref in enumerate((o0, o1, o2)):
        o_ref[0, 0] = z_ref[CH_K + g * A_HEADS:CH_K + (g + 1) * A_HEADS, :]
        o_ref[0, 1] = z_ref[CH_V + g * A_HEADS:CH_V + (g + 1) * A_HEADS, :]


def _window_new_row(zs_h, shifted):
    b = zs_h.shape[0]
    tile = (2, A_HEADS, A_HEAD_DIM)
    last_row = [pl.BlockSpec((None, 1) + tile, functools.partial(lambda i, w: (i, w - 1, 0, 0, 0), w=c.shape[1]))
                for c in shifted]
    return pl.pallas_call(
        _window_new_row_kernel,
        grid=(b,),
        in_specs=[pl.BlockSpec((None, N_HEAD_CHUNKS, A_HEAD_DIM), lambda i: (i, 0, 0))]
                 + [pl.BlockSpec(memory_space=pl.ANY)] * len(shifted),
        out_specs=last_row,
        out_shape=[jax.ShapeDtypeStruct(c.shape, c.dtype) for c in shifted],
        input_output_aliases={1: 0, 2: 1, 3: 2},
        compiler_params=_params(("parallel",)),
        name="window_new_row",
    )(zs_h.reshape(b, N_HEAD_CHUNKS, A_HEAD_DIM), *shifted)


def _alibi_slopes():
    n = N_GROUPS * A_HEADS
    return jnp.exp2(-8.0 * (jnp.arange(n, dtype=F32) + 1.0) / n).reshape(N_GROUPS, A_HEADS)


def _retention_log_decay():
    return jnp.log1p(-jnp.exp2(-5.0 - jnp.arange(R_HEADS, dtype=F32)))


def kernel(x_prompt, x_sample, cache_win0, cache_win1, cache_win2, state_ret, p_prompt, p_sample,
           g_pre, w_in, g_ret, w_a_out, w_r_out, w_o, g_post, w_ple_gate, w_ple_proj):
    batch, seq, _ = x_prompt.shape
    dec_batch = x_sample.shape[0]
    assert g_pre.shape[0] == 1 and x_sample.shape[1] == 1
    assert seq % ATT_TILE == 0
    m = batch * seq
    slopes = _alibi_slopes()
    log_g = _retention_log_decay()

    wa, wr, wo = w_a_out[0].astype(BF16), w_r_out[0].astype(BF16), w_o[0].astype(BF16)
    wg, wp = w_ple_gate[0].astype(BF16), w_ple_proj[0].astype(BF16)

    xp = x_prompt.reshape(m, D_MODEL)
    xs = x_sample.reshape(dec_batch, D_MODEL)
    u = _rmsnorm(xp, g_pre[0], TM_NORM)
    us = _rmsnorm(xs, g_pre[0], dec_batch)
    caches = (cache_win0[0], cache_win1[0], cache_win2[0])
    flat = [c.reshape((-1,) + c.shape[2:]) for c in caches]
    zh, zs_h, sh1, sh0 = _in_proj(u, us, w_in[0], 0, HEAD_COLS, F32, TM_IN, TN_IN, True,
                                  (COL_ZA, COL_QR), shift_caches=[flat[1], flat[0]])
    layout = [functools.partial(_window_prompt_rider, zh, g, batch, seq, RIDER_ROWS) for g in range(N_GROUPS)]
    np0 = batch * 2 * (min(WINDOWS[0], seq) // min(RIDER_ROWS, WINDOWS[0]))
    np1 = batch * 2 * (min(WINDOWS[1], seq) // min(RIDER_ROWS, WINDOWS[1]))
    zr, zs_r, wp2, wp1, wp0, sh2 = _in_proj(
        u, us, w_in[0], HEAD_COLS, REST_COLS, BF16, TM_IN, TN_IN, False, (COL_ZR, COL_GA),
        [(np0 + np1, layout[2]), (np0, layout[1]), (0, layout[0])], shift_caches=[flat[2]])
    win_p = [wp.reshape(1, batch, -1, 2, A_HEADS, A_HEAD_DIM) for wp in (wp0, wp1, wp2)]
    shifted = [s.reshape(c.shape) for s, c in zip((sh0, sh1, sh2), caches)]
    ga_blk = (COL_GA - HEAD_COLS) // D_MODEL

    ya = _attention(zh, slopes, batch, seq)
    yr, ret_prompt = _retention(zr, log_g, g_ret[0], batch, seq, RET_ROWS, RET_HEADS)
    y_prompt = _out_proj(ya, yr, zr, ga_blk, ga_blk + 1, xp, p_prompt[0].reshape(m, PLE_DIM),
                         wa, wr, wo, g_post[0], wg, wp, TM_OUT).reshape(batch, seq, D_MODEL)

    ya_s = _attention_step(zs_h, caches, slopes)
    yr_s, ret_sample = _retention_step(zs_r, state_ret[0], log_g, g_ret[0])
    y_sample = _out_proj(ya_s, yr_s, zs_r, ga_blk, ga_blk + 1, xs, p_sample[0].reshape(dec_batch, PLE_DIM),
                         wa, wr, wo, g_post[0], wg, wp, dec_batch).reshape(dec_batch, 1, D_MODEL)

    win_s = _window_new_row(zs_h, shifted)
    return (y_prompt, y_sample, win_p[0], win_p[1], win_p[2], ret_prompt[None],
            win_s[0][None], win_s[1][None], win_s[2][None], ret_sample[None])
```

```python
import functools

import jax
import jax.numpy as jnp
from jax import lax
from jax.experimental import pallas as pl
from jax.experimental.pallas import tpu as pltpu

F32 = jnp.float32
BF16 = jnp.bfloat16

D_MODEL = 2048
N_GROUPS = 3
DILATIONS = (1, 4, 16)
WINDOWS = (128, 512, 2048)
BAND = 128
A_HEADS = 8
A_HEAD_DIM = 128
A_QKV = N_GROUPS * A_HEADS * A_HEAD_DIM
A_WIDTH = A_HEADS * A_HEAD_DIM
R_HEADS = 8
R_DIM = 256
R_WIDTH = R_HEADS * R_DIM
R_CHUNK = 128
PLE_DIM = 256
EPS = 1e-6
N_IN = 3 * A_QKV + A_WIDTH + 4 * R_WIDTH + 2 * D_MODEL

COL_QA, COL_KA, COL_VA = 0, A_QKV, 2 * A_QKV
COL_ZA = 3 * A_QKV
COL_QR = COL_ZA + A_WIDTH
COL_KR = COL_QR + R_WIDTH
COL_VR = COL_KR + R_WIDTH
COL_ZR = COL_VR + R_WIDTH
COL_GA = COL_ZR + R_WIDTH
COL_GB = COL_GA + D_MODEL

HEAD_COLS = COL_QR
REST_COLS = N_IN - HEAD_COLS
CH_Q, CH_K, CH_V, CH_ZA = (c // A_HEAD_DIM for c in (COL_QA, COL_KA, COL_VA, COL_ZA))
N_HEAD_CHUNKS = HEAD_COLS // A_HEAD_DIM

NEG = -1e30
ATT_TILE = BAND * max(DILATIONS)
SPLIT_STRIDE = 4
assert all(d == 1 or d % SPLIT_STRIDE == 0 for d in DILATIONS)

assert R_DIM ** -0.5 == 2.0 ** -4

TM_NORM = 512
TM_IN, TN_IN = 1024, 1024
RIDER_ROWS = 256
CAST_ROWS = 64
RET_ROWS = 1024
RET_HEADS = 4
TM_OUT = 256
VMEM_LIMIT = 60 * 1024 * 1024


def _params(semantics, vmem=VMEM_LIMIT):
    return pltpu.CompilerParams(dimension_semantics=semantics, vmem_limit_bytes=vmem)


def _silu(x):
    return x * jax.nn.sigmoid(x)


def _rmsnorm_kernel(x_ref, g_ref, o_ref):
    x = x_ref[...]
    y = x * lax.rsqrt(jnp.mean(x * x, axis=-1, keepdims=True) + EPS)
    o_ref[...] = (y * g_ref[...]).astype(o_ref.dtype)


def _rmsnorm(x, g, tm):
    m, d = x.shape
    return pl.pallas_call(
        _rmsnorm_kernel,
        grid=(m // tm,),
        in_specs=[pl.BlockSpec((tm, d), lambda i: (i, 0)),
                  pl.BlockSpec((1, d), lambda i: (0, 0))],
        out_specs=pl.BlockSpec((tm, d), lambda i: (i, 0)),
        out_shape=jax.ShapeDtypeStruct((m, d), BF16),
        compiler_params=_params(("parallel",)),
        name="rmsnorm",
    )(x, g.reshape(1, d))


def _heads_to_rows_copy(in_ref, out_ref):
    for hh in range(A_HEADS):
        out_ref[:, hh, :] = in_ref[hh]


def _shift_dma_step(step, cache_ref, out_ref, buf, rsem, wsem, tsem, k0, rb):
    rows = cache_ref.shape[0]
    chunks = -(-(rows - 1) // rb)
    ci = step - k0

    def start_row(n):
        return jnp.minimum(n * rb, rows - 1 - rb)

    def read(n):
        return pltpu.make_async_copy(cache_ref.at[pl.ds(start_row(n) + 1, rb)], buf.at[n % 2], rsem.at[n % 2])

    def write(n):
        return pltpu.make_async_copy(buf.at[n % 2], out_ref.at[pl.ds(start_row(n), rb)], wsem.at[n % 2])

    tail = pltpu.make_async_copy(cache_ref.at[pl.ds(rows - 1, 1)], out_ref.at[pl.ds(rows - 1, 1)], tsem.at[0])

    @pl.when(ci == 0)
    def _():
        read(ci).start()
        tail.start()

    @pl.when(ci == chunks)
    def _():
        tail.wait()

    @pl.when(jnp.logical_and(ci >= 1, ci <= chunks))
    def _():
        write(ci - 1).wait()

    @pl.when(jnp.logical_and(ci >= 0, ci < chunks))
    def _():
        read(ci).wait()
        write(ci).start()

    @pl.when(jnp.logical_and(ci >= 0, ci + 1 < chunks))
    def _():
        read(ci + 1).start()


def _shift_dma_steps(rows, rb):
    return -(-(rows - 1) // rb) + 1


def _in_proj_kernel(*refs, head_major, silu_tiles, riders, shift_k0s):
    n_in = sum(r[1] for r in riders)
    ns = len(shift_k0s)
    u_ref, us_ref, w_ref = refs[:3]
    rider_in = refs[3:3 + n_in]
    shift_in = refs[3 + n_in:3 + n_in + ns]
    outs = refs[3 + n_in + ns:]
    o_ref, os_ref = outs[:2]
    rider_out = outs[2:2 + len(riders)]
    shift_out = outs[2 + len(riders):2 + len(riders) + ns]
    wb_ref = outs[2 + len(riders) + ns]
    c = pl.program_id(2)
    j = 2 * pl.program_id(0) + c

    @pl.when(pl.program_id(1) == 0)
    def _():
        wb_ref[c] = w_ref[...].astype(BF16)
        os_ref[...] = jnp.dot(us_ref[...], wb_ref[c], preferred_element_type=F32)

    def project(act):
        acc = jnp.dot(u_ref[...], wb_ref[c], preferred_element_type=F32)
        if act is not None:
            acc = act(acc)
        if head_major:
            for ch in range(o_ref.shape[0]):
                o_ref[ch] = acc[:, ch * A_HEAD_DIM:(ch + 1) * A_HEAD_DIM].astype(o_ref.dtype)
        else:
            o_ref[...] = acc.astype(o_ref.dtype)

    gated = jnp.logical_and(j >= silu_tiles[0], j < silu_tiles[1])
    pl.when(gated)(functools.partial(project, _silu))
    pl.when(jnp.logical_not(gated))(functools.partial(project, None))

    step = (pl.program_id(0) * pl.num_programs(1) + pl.program_id(1)) * 2 + c
    pos = 0
    for (copy, nin, k0, nsteps), out_ref in zip(riders, rider_out):
        active = jnp.logical_and(step >= k0, step < k0 + nsteps)
        pl.when(active)(functools.partial(copy, *rider_in[pos:pos + nin], out_ref))
        pos += nin

    if ns:
        buf, rsem, wsem, tsem = outs[3 + len(riders) + ns:7 + len(riders) + ns]
        for cache_ref, out_ref, k0 in zip(shift_in, shift_out, shift_k0s):
            _shift_dma_step(step, cache_ref, out_ref, buf, rsem, wsem, tsem, k0, buf.shape[1])


def _cast_copy(in_ref, out_ref):
    out_ref[...] = in_ref[...].astype(out_ref.dtype)


def _cast_rider(w, rb, k0, flat_step):
    rows, cols = w.shape
    nb = rows // rb

    def row_map(*idx):
        return jnp.clip(flat_step(*idx) - k0, 0, nb - 1), 0

    return dict(copy=_cast_copy, args=[w], steps=nb,
                in_specs=[pl.BlockSpec((rb, cols), row_map)],
                out_spec=pl.BlockSpec((rb, cols), row_map),
                out_shape=jax.ShapeDtypeStruct(w.shape, BF16))


def _window_prompt_rider(zh, g, batch, seq, rb, k0, flat_step):
    keep = min(WINDOWS[g], seq)
    rb = min(rb, keep)
    nsb = keep // rb
    row0 = (seq - keep) // rb

    def loc(*idx):
        l = jnp.clip(flat_step(*idx) - k0, 0, batch * 2 * nsb - 1)
        return l // (2 * nsb), (l // nsb) % 2, l % nsb

    def in_map(*idx):
        b, kv, s = loc(*idx)
        return (CH_K + kv * (CH_V - CH_K)) // A_HEADS + g, b * (seq // rb) + row0 + s, 0

    def out_map(*idx):
        b, kv, s = loc(*idx)
        return b * nsb + s, kv, 0

    return dict(copy=_heads_to_rows_copy, args=[zh], steps=batch * 2 * nsb,
                in_specs=[pl.BlockSpec((A_HEADS, rb, A_HEAD_DIM), in_map)],
                out_spec=pl.BlockSpec((rb, A_HEADS, A_HEAD_DIM), out_map),
                out_shape=jax.ShapeDtypeStruct((batch * keep, 2 * A_HEADS, A_HEAD_DIM), zh.dtype))


def _in_proj(u, us, w, col0, ncols, out_dtype, tm, tn, head_major, silu_cols, make_riders=(),
             shift_caches=()):
    m, k = u.shape
    ms = us.shape[0]
    j0 = col0 // tn
    nj, ni = ncols // tn, m // tm
    assert nj % 2 == 0

    def flat_step(p, i, c):
        return (p * ni + i) * 2 + c

    def sticky_tile(p, i, c):
        return 2 * p + jnp.where(i == 0, c, 1)

    in_specs = [pl.BlockSpec((tm, k), lambda p, i, c: (i, 0)),
                pl.BlockSpec((ms, k), lambda p, i, c: (0, 0)),
                pl.BlockSpec((k, tn), lambda p, i, c: (0, j0 + sticky_tile(p, i, c)))]
    if head_major:
        o_spec = pl.BlockSpec((tn // A_HEAD_DIM, tm, A_HEAD_DIM), lambda p, i, c: (2 * p + c, i, 0))
        o_shape = jax.ShapeDtypeStruct((ncols // A_HEAD_DIM, m, A_HEAD_DIM), out_dtype)
    else:
        o_spec = pl.BlockSpec((tm, tn), lambda p, i, c: (i, 2 * p + c))
        o_shape = jax.ShapeDtypeStruct((m, ncols), out_dtype)
    out_specs = [o_spec, pl.BlockSpec((ms, tn), lambda p, i, c: (0, sticky_tile(p, i, c)))]
    out_shape = [o_shape, jax.ShapeDtypeStruct((ms, ncols), F32)]
    riders, rider_args = [], []
    for k0, make in make_riders:
        rider = make(k0, flat_step)
        in_specs += rider["in_specs"]
        out_specs.append(rider["out_spec"])
        out_shape.append(rider["out_shape"])
        rider_args += rider["args"]
        riders.append((rider["copy"], len(rider["in_specs"]), k0, rider["steps"]))
        assert k0 + rider["steps"] <= nj * ni
    any_spec = pl.BlockSpec(memory_space=pl.ANY)
    scratch = [pltpu.VMEM((2, k, tn), BF16)]
    shift_k0s, k0 = [], 0
    for cache in shift_caches:
        in_specs.append(any_spec)
        out_specs.append(any_spec)
        out_shape.append(jax.ShapeDtypeStruct(cache.shape, cache.dtype))
        shift_k0s.append(k0)
        k0 += _shift_dma_steps(cache.shape[0], RIDER_ROWS)
    assert k0 <= nj * ni
    if shift_caches:
        scratch += [pltpu.VMEM((2, RIDER_ROWS) + shift_caches[0].shape[1:], shift_caches[0].dtype),
                    pltpu.SemaphoreType.DMA((2,)), pltpu.SemaphoreType.DMA((2,)),
                    pltpu.SemaphoreType.DMA((1,))]
    silu_tiles = ((silu_cols[0] - col0) // tn, (silu_cols[1] - col0) // tn)
    return pl.pallas_call(
        functools.partial(_in_proj_kernel, head_major=head_major, silu_tiles=silu_tiles,
                          riders=tuple(riders), shift_k0s=tuple(shift_k0s)),
        grid=(nj // 2, ni, 2),
        in_specs=in_specs,
        out_specs=out_specs,
        out_shape=out_shape,
        scratch_shapes=scratch,
        compiler_params=_params(("arbitrary", "arbitrary", "arbitrary")),
        name="in_proj_hm" if head_major else "in_proj",
    )(u, us, w, *rider_args, *shift_caches)


def _attn_kernel(slope_ref, q0, q1, q2, k0, k1, k2, v0, v1, v2, za_ref, o_ref,
                 qbuf, kbuf, vbuf, onat, lnat, stage, *, seq):
    q_refs, k_refs, v_refs = (q0, q1, q2), (k0, k1, k2), (v0, v1, v2)
    h = pl.program_id(1)
    t = pl.program_id(2)

    @pl.when(t == 0)
    def _():
        for g, dil in enumerate(DILATIONS):
            for r in range(dil):
                run = seq // dil + BAND
                kbuf[g, pl.ds(r * run, BAND), :] = jnp.zeros((BAND, A_HEAD_DIM), BF16)
                vbuf[g, r * run // BAND] = jnp.zeros((A_HEAD_DIM, BAND), BF16)

    kj = lax.broadcasted_iota(jnp.int32, (2 * BAND, BAND), 0)
    qi = lax.broadcasted_iota(jnp.int32, (2 * BAND, BAND), 1)
    dist = qi + BAND - kj
    valid = jnp.logical_and(dist >= 0, dist <= BAND)
    distf = dist.astype(F32)
    first_pen = jnp.where(jnp.logical_and(t == 0, kj < BAND), NEG, 0.0)
    scale = A_HEAD_DIM ** -0.5

    for g in range(N_GROUPS):
        dil = DILATIONS[g]
        n = ATT_TILE // dil
        nblk = n // BAND
        run = seq // dil + BAND
        srcs = (q_refs[g], k_refs[g], v_refs[g])
        if dil > SPLIT_STRIDE:
            coarse = ATT_TILE // SPLIT_STRIDE
            for a, src in enumerate(srcs):
                for r4 in range(SPLIT_STRIDE):
                    stage[a, r4 * coarse:(r4 + 1) * coarse, :] = src[pl.ds(r4, coarse, stride=SPLIT_STRIDE), :]
            srcs = (stage.at[0], stage.at[1], stage.at[2])
        for r in range(dil):
            if dil > SPLIT_STRIDE:
                rows = pl.ds((r % SPLIT_STRIDE) * coarse + r // SPLIT_STRIDE, n, stride=dil // SPLIT_STRIDE)
            else:
                rows = pl.ds(r, n, stride=dil) if dil > 1 else pl.ds(0, n)
            dst = pl.ds(pl.multiple_of(r * run + BAND + t * n, BAND), n)
            qbuf[g, r * n:(r + 1) * n, :] = (srcs[0][rows, :] * scale).astype(BF16)
            kbuf[g, dst, :] = srcs[1][rows, :].astype(BF16)
            v = srcs[2][rows, :]
            for i in range(nblk):
                blk = (r * run + BAND + i * BAND) // BAND + t * nblk
                vbuf[g, blk] = v[i * BAND:(i + 1) * BAND, :].T.astype(BF16)

        bias = jnp.where(valid, -(slope_ref[g, h] * float(dil)) * distf, NEG)
        bias_first = bias + first_pen

        for r in range(dil):
            for i in range(nblk):
                u = r * nblk + i
                blk = (r * run + i * BAND) // BAND + t * nblk
                kv_rows = pl.ds(pl.multiple_of(blk * BAND, BAND), 2 * BAND)
                q = qbuf[g, u * BAND:(u + 1) * BAND, :]
                s = lax.dot_general(kbuf[g, kv_rows, :], q, (((1,), (1,)), ((), ())),
                                    preferred_element_type=F32)
                s = s + (bias_first if i == 0 else bias)
                m = jnp.max(s, axis=0, keepdims=True)
                e = jnp.exp(s - m)
                den = jnp.sum(e, axis=0, keepdims=True)
                v_t = jnp.concatenate([vbuf[g, blk], vbuf[g, blk + 1]], axis=1)
                acc_t = jnp.dot(v_t, e.astype(BF16), preferred_element_type=F32)
                if dil == 1:
                    out_rows = pl.ds(u * BAND, BAND)
                else:
                    fine = dil // SPLIT_STRIDE
                    start = (r % SPLIT_STRIDE) * (ATT_TILE // SPLIT_STRIDE) + r // SPLIT_STRIDE + i * BAND * fine
                    out_rows = pl.ds(start, BAND, stride=fine) if fine > 1 else pl.ds(start, BAND)
                onat[g, out_rows, :] = (acc_t / den).T
                lnat[g, out_rows, :] = jnp.broadcast_to(m + jnp.log(den), (A_HEAD_DIM, BAND)).T

    quarter = ATT_TILE // SPLIT_STRIDE
    for r4 in range(SPLIT_STRIDE):
        strided = pl.ds(r4, quarter, stride=SPLIT_STRIDE)
        grouped = pl.ds(r4 * quarter, quarter)
        ls = [lnat[g, strided if dil == 1 else grouped, :] for g, dil in enumerate(DILATIONS)]
        os_ = [onat[g, strided if dil == 1 else grouped, :] for g, dil in enumerate(DILATIONS)]
        mx = jnp.maximum(jnp.maximum(ls[0], ls[1]), ls[2])
        w0, w1, w2 = jnp.exp(ls[0] - mx), jnp.exp(ls[1] - mx), jnp.exp(ls[2] - mx)
        mixed = (w0 * os_[0] + w1 * os_[1] + w2 * os_[2]) / (w0 + w1 + w2)
        o_ref[strided, :] = mixed * za_ref[strided, :]


def _attention(zh, slopes, batch, seq):
    nt = seq // ATT_TILE
    m = batch * seq

    def head_spec(c0):
        return pl.BlockSpec((None, ATT_TILE, A_HEAD_DIM), lambda b, h, t, s: (c0 + h, b * nt + t, 0))

    in_specs = [head_spec(c + g * A_HEADS) for c in (CH_Q, CH_K, CH_V) for g in range(N_GROUPS)]
    in_specs.append(head_spec(CH_ZA))
    buf_rows = seq + BAND * max(DILATIONS)
    grid_spec = pltpu.PrefetchScalarGridSpec(
        num_scalar_prefetch=1,
        grid=(batch, A_HEADS, nt),
        in_specs=in_specs,
        out_specs=pl.BlockSpec((ATT_TILE, A_HEAD_DIM), lambda b, h, t, s: (b * nt + t, h)),
        scratch_shapes=[
            pltpu.VMEM((N_GROUPS, ATT_TILE, A_HEAD_DIM), BF16),
            pltpu.VMEM((N_GROUPS, buf_rows, A_HEAD_DIM), BF16),
            pltpu.VMEM((N_GROUPS, buf_rows // BAND, A_HEAD_DIM, BAND), BF16),
            pltpu.VMEM((N_GROUPS, ATT_TILE, A_HEAD_DIM), F32),
            pltpu.VMEM((N_GROUPS, ATT_TILE, A_HEAD_DIM), F32),
            pltpu.VMEM((3, ATT_TILE, A_HEAD_DIM), F32),
        ],
    )
    return pl.pallas_call(
        functools.partial(_attn_kernel, seq=seq), grid_spec=grid_spec,
        out_shape=jax.ShapeDtypeStruct((m, A_WIDTH), F32),
        compiler_params=_params(("parallel", "parallel", "arbitrary")),
        name="dilated_attn",
    )(slopes, *([zh] * 10))


def _retention_kernel(lg_ref, q_ref, k_ref, v_ref, zr_ref, gret_ref, y_ref, rout_ref, r_scr):
    s = pl.program_id(2)
    heads = r_scr.shape[0]

    @pl.when(s == 0)
    def _():
        r_scr[...] = jnp.zeros_like(r_scr)

    L = R_CHUNK
    pi = lax.broadcasted_iota(jnp.int32, (L, L), 0)
    pj = lax.broadcasted_iota(jnp.int32, (L, L), 1)
    rel = (pi - pj).astype(F32)
    pos = lax.broadcasted_iota(jnp.int32, (L, 1), 0).astype(F32)
    k_scale = R_DIM ** -0.5
    decays = []
    for hp in range(heads):
        log_g = lg_ref[pl.program_id(1) * heads + hp]
        decays.append((
            jnp.where(rel >= 0, jnp.exp(jnp.maximum(rel, 0.0) * log_g), 0.0) * k_scale,
            jnp.exp((pos + 1.0) * log_g),
            jnp.exp((L - 1.0 - pos) * log_g) * k_scale,
            jnp.exp(jnp.full((1, 1), float(L), F32) * log_g)))

    def chunk(c, carry):
        rows = pl.ds(pl.multiple_of(c * L, L), L)
        for hp, (intra, q_decay, k_decay, chunk_decay) in enumerate(decays):
            cols = slice(hp * R_DIM, (hp + 1) * R_DIM)
            q = q_ref[rows, cols]
            k = k_ref[rows, cols]
            v = v_ref[rows, cols]
            r_prev = r_scr[hp]
            scores = lax.dot_general(q, k, (((1,), (1,)), ((), ())), preferred_element_type=F32) * intra
            inner = jnp.dot(scores.astype(BF16), v, preferred_element_type=F32)
            cross = jnp.dot(q, r_prev.astype(BF16), preferred_element_type=F32) * q_decay
            kd = (k.astype(F32) * k_decay).astype(BF16)
            r_scr[hp] = r_prev * chunk_decay + lax.dot_general(
                kd, v, (((0,), (0,)), ((), ())), preferred_element_type=F32)
            y = inner + cross
            y = y * lax.rsqrt(jnp.mean(y * y, axis=-1, keepdims=True) + EPS)
            y = (y * gret_ref[:, cols]) * zr_ref[rows, cols].astype(F32)
            y_ref[rows, cols] = y.astype(y_ref.dtype)
        return carry

    lax.fori_loop(0, q_ref.shape[0] // L, chunk, 0, unroll=True)

    @pl.when(s == pl.num_programs(2) - 1)
    def _():
        rout_ref[...] = r_scr[...]


def _retention(zr_all, log_g, g_ret, batch, seq, rb, heads):
    ns = seq // rb
    m = batch * seq
    width = heads * R_DIM

    def col_spec(col):
        c0 = (col - HEAD_COLS) // width
        return pl.BlockSpec((rb, width), lambda b, h, s, lg: (b * ns + s, c0 + h))

    grid_spec = pltpu.PrefetchScalarGridSpec(
        num_scalar_prefetch=1,
        grid=(batch, R_HEADS // heads, ns),
        in_specs=[col_spec(COL_QR), col_spec(COL_KR), col_spec(COL_VR), col_spec(COL_ZR),
                  pl.BlockSpec((1, width), lambda b, h, s, lg: (0, h))],
        out_specs=[pl.BlockSpec((rb, width), lambda b, h, s, lg: (b * ns + s, h)),
                   pl.BlockSpec((None, heads, R_DIM, R_DIM), lambda b, h, s, lg: (b, h, 0, 0))],
        scratch_shapes=[pltpu.VMEM((heads, R_DIM, R_DIM), F32)],
    )
    return pl.pallas_call(
        _retention_kernel, grid_spec=grid_spec,
        out_shape=[jax.ShapeDtypeStruct((m, R_WIDTH), BF16),
                   jax.ShapeDtypeStruct((batch, R_HEADS, R_DIM, R_DIM), F32)],
        compiler_params=_params(("parallel", "parallel", "arbitrary")),
        name="retention",
    )(log_g, zr_all, zr_all, zr_all, zr_all, g_ret.reshape(1, R_WIDTH))


def _out_kernel(ya_ref, yr_ref, ga_ref, gb_ref, x_ref, p_ref,
                wa_ref, wr_ref, wo_ref, gpost_ref, wg_ref, wp_ref, o_ref):
    a = jnp.dot(ya_ref[...].astype(BF16), wa_ref[...], preferred_element_type=F32)
    b = jnp.dot(yr_ref[...].astype(BF16), wr_ref[...], preferred_element_type=F32)
    merged = (jax.nn.sigmoid(ga_ref[...].astype(F32)) * a
              + jax.nn.sigmoid(gb_ref[...].astype(F32)) * b)
    y = jnp.dot(merged.astype(BF16), wo_ref[...], preferred_element_type=F32)
    y = y * lax.rsqrt(jnp.mean(y * y, axis=-1, keepdims=True) + EPS)
    hres = x_ref[...] + y * gpost_ref[...]
    gate = jax.nn.sigmoid(jnp.dot(hres.astype(BF16), wg_ref[...], preferred_element_type=F32))
    emb = jnp.dot(p_ref[...].astype(BF16), wp_ref[...], preferred_element_type=F32)
    o_ref[...] = hres + gate * emb


def _out_proj(ya, yr, gates, ga_blk, gb_blk, x, p, wa, wr, wo, g_post, wg, wp, tm):
    m = x.shape[0]

    def rows(width, cb=0):
        return pl.BlockSpec((tm, width), lambda i: (i, cb))

    def whole(arr):
        return pl.BlockSpec(arr.shape, lambda i: (0, 0), pipeline_mode=pl.Buffered(1))

    gp = g_post.reshape(1, D_MODEL)
    return pl.pallas_call(
        _out_kernel,
        grid=(m // tm,),
        in_specs=[rows(A_WIDTH), rows(R_WIDTH), rows(D_MODEL, ga_blk), rows(D_MODEL, gb_blk),
                  rows(D_MODEL), rows(PLE_DIM),
                  whole(wa), whole(wr), whole(wo), whole(gp), whole(wg), whole(wp)],
        out_specs=rows(D_MODEL),
        out_shape=jax.ShapeDtypeStruct((m, D_MODEL), F32),
        compiler_params=_params(("parallel",)),
        name="out_proj",
    )(ya, yr, gates, gates, x, p, wa, wr, wo, gp, wg, wp)


def _attn_step_kernel(slope_ref, z_ref, c0_ref, c1_ref, c2_ref, o_ref):
    c_refs = (c0_ref, c1_ref, c2_ref)
    scale = A_HEAD_DIM ** -0.5
    steps = float(BAND) - lax.broadcasted_iota(jnp.int32, (BAND, 1, 1), 0).astype(F32)
    outs, lses = [], []
    for g in range(N_GROUPS):
        q = z_ref[CH_Q + g * A_HEADS:CH_Q + (g + 1) * A_HEADS, :] * scale
        kn = z_ref[CH_K + g * A_HEADS:CH_K + (g + 1) * A_HEADS, :]
        vn = z_ref[CH_V + g * A_HEADS:CH_V + (g + 1) * A_HEADS, :]
        kc = c_refs[g][:, 0]
        vc = c_refs[g][:, 1]
        slope = slope_ref[g][None]
        s = jnp.sum(kc * q[None], axis=-1, keepdims=True) - (slope * float(DILATIONS[g])) * steps
        s_new = jnp.sum(kn * q, axis=-1, keepdims=True)
        m = jnp.maximum(jnp.max(s, axis=0), s_new)
        e = jnp.exp(s - m[None])
        e_new = jnp.exp(s_new - m)
        den = jnp.sum(e, axis=0) + e_new
        outs.append((jnp.sum(e * vc, axis=0) + e_new * vn) / den)
        lses.append(m + jnp.log(den))
    mx = jnp.maximum(jnp.maximum(lses[0], lses[1]), lses[2])
    ws = [jnp.exp(l - mx) for l in lses]
    mixed = (ws[0] * outs[0] + ws[1] * outs[1] + ws[2] * outs[2]) / (ws[0] + ws[1] + ws[2])
    o_ref[...] = mixed * _silu(z_ref[CH_ZA:CH_ZA + A_HEADS, :])


def _attention_step(zs_h, caches, slopes):
    b = zs_h.shape[0]
    z3 = zs_h.reshape(b, N_HEAD_CHUNKS, A_HEAD_DIM)
    views = [c.reshape(b, BAND, DILATIONS[g], 2, A_HEADS, A_HEAD_DIM) for g, c in enumerate(caches)]
    cache_spec = pl.BlockSpec((None, BAND, None, 2, A_HEADS, A_HEAD_DIM),
                              lambda i: (i, 0, 0, 0, 0, 0))
    out = pl.pallas_call(
        _attn_step_kernel,
        grid=(b,),
        in_specs=[pl.BlockSpec((N_GROUPS, A_HEADS, 1), lambda i: (0, 0, 0)),
                  pl.BlockSpec((None, N_HEAD_CHUNKS, A_HEAD_DIM), lambda i: (i, 0, 0)),
                  cache_spec, cache_spec, cache_spec],
        out_specs=pl.BlockSpec((None, A_HEADS, A_HEAD_DIM), lambda i: (i, 0, 0)),
        out_shape=jax.ShapeDtypeStruct((b, A_HEADS, A_HEAD_DIM), F32),
        compiler_params=_params(("parallel",)),
        name="dilated_attn_step",
    )(slopes.reshape(N_GROUPS, A_HEADS, 1), z3, *views)
    return out.reshape(b, A_WIDTH)


def _retention_step_kernel(lg_ref, q_ref, k_ref, v_ref, zr_ref, gret_ref, r_ref, y_ref, rout_ref):
    for hh in range(R_HEADS):
        cols = slice(hh * R_DIM, (hh + 1) * R_DIM)
        gamma = jnp.exp(jnp.full((1, 1), 1.0, F32) * lg_ref[hh])
        q = q_ref[:, cols]
        k = k_ref[:, cols] * (R_DIM ** -0.5)
        v = v_ref[:, cols]
        r_prev = r_ref[hh]
        qb = jnp.broadcast_to(q, (8, R_DIM)).astype(BF16)
        cross = jnp.dot(qb, r_prev.astype(BF16), preferred_element_type=F32)[0:1] * gamma
        inner = jnp.sum(q * k, axis=-1, keepdims=True) * v
        k_col = jnp.broadcast_to(k, (R_DIM, R_DIM)).T
        rout_ref[hh] = r_prev * gamma + k_col * v
        y = inner + cross
        y = y * lax.rsqrt(jnp.mean(y * y, axis=-1, keepdims=True) + EPS)
        y_ref[:, cols] = (y * gret_ref[:, cols]) * _silu(zr_ref[:, cols])


def _retention_step(zs_r, state, log_g, g_ret):
    b = zs_r.shape[0]
    z3 = zs_r.reshape(b, 1, REST_COLS)

    def col_spec(col):
        return pl.BlockSpec((None, 1, R_WIDTH), lambda i, lg: (i, 0, (col - HEAD_COLS) // R_WIDTH))

    state_spec = pl.BlockSpec((None, R_HEADS, R_DIM, R_DIM), lambda i, lg: (i, 0, 0, 0))
    grid_spec = pltpu.PrefetchScalarGridSpec(
        num_scalar_prefetch=1,
        grid=(b,),
        in_specs=[col_spec(COL_QR), col_spec(COL_KR), col_spec(COL_VR), col_spec(COL_ZR),
                  pl.BlockSpec((1, R_WIDTH), lambda i, lg: (0, 0)), state_spec],
        out_specs=[pl.BlockSpec((None, 1, R_WIDTH), lambda i, lg: (i, 0, 0)), state_spec],
    )
    y, new_state = pl.pallas_call(
        _retention_step_kernel, grid_spec=grid_spec,
        out_shape=[jax.ShapeDtypeStruct((b, 1, R_WIDTH), F32),
                   jax.ShapeDtypeStruct(state.shape, F32)],
        compiler_params=_params(("parallel",)),
        name="retention_step",
    )(log_g, z3, z3, z3, z3, g_ret.reshape(1, R_WIDTH), state)
    return y.reshape(b, R_WIDTH), new_state


def _window_new_row_kernel(z_ref, c0, c1, c2, o0, o1, o2):
    del c0, c1, c2
    for g, o_ref in enumerate((o0, o1, o2)):
        o_ref[0, 0] = z_ref[CH_K + g * A_HEADS:CH_K + (g + 1) * A_HEADS, :]
        o_ref[0, 1] = z_ref[CH_V + g * A_HEADS:CH_V + (g + 1) * A_HEADS, :]


def _window_new_row(zs_h, shifted):
    b = zs_h.shape[0]
    tile = (2, A_HEADS, A_HEAD_DIM)
    last_row = [pl.BlockSpec((None, 1) + tile, functools.partial(lambda i, w: (i, w - 1, 0, 0, 0), w=c.shape[1]))
                for c in shifted]
    return pl.pallas_call(
        _window_new_row_kernel,
        grid=(b,),
        in_specs=[pl.BlockSpec((None, N_HEAD_CHUNKS, A_HEAD_DIM), lambda i: (i, 0, 0))]
                 + [pl.BlockSpec(memory_space=pl.ANY)] * len(shifted),
        out_specs=last_row,
        out_shape=[jax.ShapeDtypeStruct(c.shape, c.dtype) for c in shifted],
        input_output_aliases={1: 0, 2: 1, 3: 2},
        compiler_params=_params(("parallel",)),
        name="window_new_row",
    )(zs_h.reshape(b, N_HEAD_CHUNKS, A_HEAD_DIM), *shifted)


def _alibi_slopes():
    n = N_GROUPS * A_HEADS
    return jnp.exp2(-8.0 * (jnp.arange(n, dtype=F32) + 1.0) / n).reshape(N_GROUPS, A_HEADS)


def _retention_log_decay():
    return jnp.log1p(-jnp.exp2(-5.0 - jnp.arange(R_HEADS, dtype=F32)))


def kernel(x_prompt, x_sample, cache_win0, cache_win1, cache_win2, state_ret, p_prompt, p_sample,
           g_pre, w_in, g_ret, w_a_out, w_r_out, w_o, g_post, w_ple_gate, w_ple_proj):
    batch, seq, _ = x_prompt.shape
    dec_batch = x_sample.shape[0]
    assert g_pre.shape[0] == 1 and x_sample.shape[1] == 1
    assert seq % ATT_TILE == 0
    m = batch * seq
    slopes = _alibi_slopes()
    log_g = _retention_log_decay()

    xp = x_prompt.reshape(m, D_MODEL)
    xs = x_sample.reshape(dec_batch, D_MODEL)
    u = _rmsnorm(xp, g_pre[0], TM_NORM)
    us = _rmsnorm(xs, g_pre[0], dec_batch)
    caches = (cache_win0[0], cache_win1[0], cache_win2[0])
    flat = [c.reshape((-1,) + c.shape[2:]) for c in caches]
    casts = [(0, functools.partial(_cast_rider, wt[0], CAST_ROWS))
             for wt in (w_a_out, w_r_out, w_o, w_ple_gate, w_ple_proj)]
    zh, zs_h, wa, wr, wo, wg, wp, sh1, sh0 = _in_proj(
        u, us, w_in[0], 0, HEAD_COLS, F32, TM_IN, TN_IN, True, (COL_ZA, COL_QR), casts,
        shift_caches=[flat[1], flat[0]])
    layout = [functools.partial(_window_prompt_rider, zh, g, batch, seq, RIDER_ROWS) for g in range(N_GROUPS)]
    np0 = batch * 2 * (min(WINDOWS[0], seq) // min(RIDER_ROWS, WINDOWS[0]))
    np1 = batch * 2 * (min(WINDOWS[1], seq) // min(RIDER_ROWS, WINDOWS[1]))
    zr, zs_r, wp2, wp1, wp0, sh2 = _in_proj(
        u, us, w_in[0], HEAD_COLS, REST_COLS, BF16, TM_IN, TN_IN, False, (COL_ZR, COL_GA),
        [(np0 + np1, layout[2]), (np0, layout[1]), (0, layout[0])], shift_caches=[flat[2]])
    win_p = [wp.reshape(1, batch, -1, 2, A_HEADS, A_HEAD_DIM) for wp in (wp0, wp1, wp2)]
    shifted = [s.reshape(c.shape) for s, c in zip((sh0, sh1, sh2), caches)]
    ga_blk = (COL_GA - HEAD_COLS) // D_MODEL

    ya = _attention(zh, slopes, batch, seq)
    yr, ret_prompt = _retention(zr, log_g, g_ret[0], batch, seq, RET_ROWS, RET_HEADS)
    y_prompt = _out_proj(ya, yr, zr, ga_blk, ga_blk + 1, xp, p_prompt[0].reshape(m, PLE_DIM),
                         wa, wr, wo, g_post[0], wg, wp, TM_OUT).reshape(batch, seq, D_MODEL)

    ya_s = _attention_step(zs_h, caches, slopes)
    yr_s, ret_sample = _retention_step(zs_r, state_ret[0], log_g, g_ret[0])
    y_sample = _out_proj(ya_s, yr_s, zs_r, ga_blk, ga_blk + 1, xs, p_sample[0].reshape(dec_batch, PLE_DIM),
                         wa, wr, wo, g_post[0], wg, wp, dec_batch).reshape(dec_batch, 1, D_MODEL)

    win_s = _window_new_row(zs_h, shifted)
    return (y_prompt, y_sample, win_p[0], win_p[1], win_p[2], ret_prompt[None],
            win_s[0][None], win_s[1][None], win_s[2][None], ret_sample[None])
```

```python
import functools

import jax
import jax.numpy as jnp
from jax import lax
from jax.experimental import pallas as pl
from jax.experimental.pallas import tpu as pltpu

F32 = jnp.float32
BF16 = jnp.bfloat16

D_MODEL = 2048
N_GROUPS = 3
DILATIONS = (1, 4, 16)
WINDOWS = (128, 512, 2048)
BAND = 128
A_HEADS = 8
A_HEAD_DIM = 128
A_QKV = N_GROUPS * A_HEADS * A_HEAD_DIM
A_WIDTH = A_HEADS * A_HEAD_DIM
R_HEADS = 8
R_DIM = 256
R_WIDTH = R_HEADS * R_DIM
R_CHUNK = 128
PLE_DIM = 256
EPS = 1e-6
N_IN = 3 * A_QKV + A_WIDTH + 4 * R_WIDTH + 2 * D_MODEL

COL_QA, COL_KA, COL_VA = 0, A_QKV, 2 * A_QKV
COL_ZA = 3 * A_QKV
COL_QR = COL_ZA + A_WIDTH
COL_KR = COL_QR + R_WIDTH
COL_VR = COL_KR + R_WIDTH
COL_ZR = COL_VR + R_WIDTH
COL_GA = COL_ZR + R_WIDTH
COL_GB = COL_GA + D_MODEL

HEAD_COLS = COL_QR
REST_COLS = N_IN - HEAD_COLS
CH_Q, CH_K, CH_V, CH_ZA = (c // A_HEAD_DIM for c in (COL_QA, COL_KA, COL_VA, COL_ZA))
N_HEAD_CHUNKS = HEAD_COLS // A_HEAD_DIM

NEG = -1e30
ATT_TILE = BAND * max(DILATIONS)
SPLIT_STRIDE = 4
assert all(d == 1 or d % SPLIT_STRIDE == 0 for d in DILATIONS)

assert R_DIM ** -0.5 == 2.0 ** -4

TM_NORM = 512
TM_IN, TN_IN = 1024, 1024
RIDER_ROWS = 256
CAST_ROWS = 64
RET_ROWS = 1024
RET_HEADS = 4
TM_OUT = 256
VMEM_LIMIT = 60 * 1024 * 1024


def _params(semantics, vmem=VMEM_LIMIT):
    return pltpu.CompilerParams(dimension_semantics=semantics, vmem_limit_bytes=vmem)


def _silu(x):
    return x * jax.nn.sigmoid(x)


def _rmsnorm_kernel(x_ref, g_ref, o_ref):
    x = x_ref[...]
    y = x * lax.rsqrt(jnp.mean(x * x, axis=-1, keepdims=True) + EPS)
    o_ref[...] = (y * g_ref[...]).astype(o_ref.dtype)


def _rmsnorm(x, g, tm):
    m, d = x.shape
    return pl.pallas_call(
        _rmsnorm_kernel,
        grid=(m // tm,),
        in_specs=[pl.BlockSpec((tm, d), lambda i: (i, 0)),
                  pl.BlockSpec((1, d), lambda i: (0, 0))],
        out_specs=pl.BlockSpec((tm, d), lambda i: (i, 0)),
        out_shape=jax.ShapeDtypeStruct((m, d), BF16),
        compiler_params=_params(("parallel",)),
        name="rmsnorm",
    )(x, g.reshape(1, d))


def _heads_to_rows_copy(in_ref, out_ref):
    for hh in range(A_HEADS):
        out_ref[:, hh, :] = in_ref[hh]


def _shift_dma_step(step, cache_ref, out_ref, buf, rsem, wsem, tsem, k0, rb):
    rows = cache_ref.shape[0]
    chunks = -(-(rows - 1) // rb)
    ci = step - k0

    def start_row(n):
        return jnp.minimum(n * rb, rows - 1 - rb)

    def read(n):
        return pltpu.make_async_copy(cache_ref.at[pl.ds(start_row(n) + 1, rb)], buf.at[n % 2], rsem.at[n % 2])

    def write(n):
        return pltpu.make_async_copy(buf.at[n % 2], out_ref.at[pl.ds(start_row(n), rb)], wsem.at[n % 2])

    tail = pltpu.make_async_copy(cache_ref.at[pl.ds(rows - 1, 1)], out_ref.at[pl.ds(rows - 1, 1)], tsem.at[0])

    @pl.when(ci == 0)
    def _():
        read(ci).start()
        tail.start()

    @pl.when(ci == chunks)
    def _():
        tail.wait()

    @pl.when(jnp.logical_and(ci >= 1, ci <= chunks))
    def _():
        write(ci - 1).wait()

    @pl.when(jnp.logical_and(ci >= 0, ci < chunks))
    def _():
        read(ci).wait()
        write(ci).start()

    @pl.when(jnp.logical_and(ci >= 0, ci + 1 < chunks))
    def _():
        read(ci + 1).start()


def _shift_dma_steps(rows, rb):
    return -(-(rows - 1) // rb) + 1


def _in_proj_kernel(*refs, head_major, silu_tiles, riders, shift_k0s):
    n_in = sum(r[1] for r in riders)
    ns = len(shift_k0s)
    u_ref, us_ref, w_ref = refs[:3]
    rider_in = refs[3:3 + n_in]
    shift_in = refs[3 + n_in:3 + n_in + ns]
    outs = refs[3 + n_in + ns:]
    o_ref, os_ref = outs[:2]
    rider_out = outs[2:2 + len(riders)]
    shift_out = outs[2 + len(riders):2 + len(riders) + ns]
    wb_ref = outs[2 + len(riders) + ns]
    c = pl.program_id(2)
    j = 2 * pl.program_id(0) + c

    @pl.when(pl.program_id(1) == 0)
    def _():
        wb_ref[c] = w_ref[...].astype(BF16)
        os_ref[...] = jnp.dot(us_ref[...], wb_ref[c], preferred_element_type=F32)

    def project(act):
        acc = jnp.dot(u_ref[...], wb_ref[c], preferred_element_type=F32)
        if act is not None:
            acc = act(acc)
        if head_major:
            for ch in range(o_ref.shape[0]):
                o_ref[ch] = acc[:, ch * A_HEAD_DIM:(ch + 1) * A_HEAD_DIM].astype(o_ref.dtype)
        else:
            o_ref[...] = acc.astype(o_ref.dtype)

    gated = jnp.logical_and(j >= silu_tiles[0], j < silu_tiles[1])
    pl.when(gated)(functools.partial(project, _silu))
    pl.when(jnp.logical_not(gated))(functools.partial(project, None))

    step = (pl.program_id(0) * pl.num_programs(1) + pl.program_id(1)) * 2 + c
    pos = 0
    for (copy, nin, k0, nsteps), out_ref in zip(riders, rider_out):
        active = jnp.logical_and(step >= k0, step < k0 + nsteps)
        pl.when(active)(functools.partial(copy, *rider_in[pos:pos + nin], out_ref))
        pos += nin

    if ns:
        buf, rsem, wsem, tsem = outs[3 + len(riders) + ns:7 + len(riders) + ns]
        for cache_ref, out_ref, k0 in zip(shift_in, shift_out, shift_k0s):
            _shift_dma_step(step, cache_ref, out_ref, buf, rsem, wsem, tsem, k0, buf.shape[1])


def _cast_copy(in_ref, out_ref):
    out_ref[...] = in_ref[...].astype(out_ref.dtype)


def _cast_rider(w, rb, k0, flat_step):
    rows, cols = w.shape
    nb = rows // rb

    def row_map(*idx):
        return jnp.clip(flat_step(*idx) - k0, 0, nb - 1), 0

    return dict(copy=_cast_copy, args=[w], steps=nb,
                in_specs=[pl.BlockSpec((rb, cols), row_map)],
                out_spec=pl.BlockSpec((rb, cols), row_map),
                out_shape=jax.ShapeDtypeStruct(w.shape, BF16))


def _window_prompt_rider(zh, g, batch, seq, rb, k0, flat_step):
    keep = min(WINDOWS[g], seq)
    rb = min(rb, keep)
    nsb = keep // rb
    row0 = (seq - keep) // rb

    def loc(*idx):
        l = jnp.clip(flat_step(*idx) - k0, 0, batch * 2 * nsb - 1)
        return l // (2 * nsb), (l // nsb) % 2, l % nsb

    def in_map(*idx):
        b, kv, s = loc(*idx)
        return (CH_K + kv * (CH_V - CH_K)) // A_HEADS + g, b * (seq // rb) + row0 + s, 0

    def out_map(*idx):
        b, kv, s = loc(*idx)
        return b * nsb + s, kv, 0

    return dict(copy=_heads_to_rows_copy, args=[zh], steps=batch * 2 * nsb,
                in_specs=[pl.BlockSpec((A_HEADS, rb, A_HEAD_DIM), in_map)],
                out_spec=pl.BlockSpec((rb, A_HEADS, A_HEAD_DIM), out_map),
                out_shape=jax.ShapeDtypeStruct((batch * keep, 2 * A_HEADS, A_HEAD_DIM), zh.dtype))


def _in_proj(u, us, w, col0, ncols, out_dtype, tm, tn, head_major, silu_cols, make_riders=(),
             shift_caches=()):
    m, k = u.shape
    ms = us.shape[0]
    j0 = col0 // tn
    nj, ni = ncols // tn, m // tm
    assert nj % 2 == 0

    def flat_step(p, i, c):
        return (p * ni + i) * 2 + c

    def sticky_tile(p, i, c):
        return 2 * p + jnp.where(i == 0, c, 1)

    in_specs = [pl.BlockSpec((tm, k), lambda p, i, c: (i, 0)),
                pl.BlockSpec((ms, k), lambda p, i, c: (0, 0)),
                pl.BlockSpec((k, tn), lambda p, i, c: (0, j0 + sticky_tile(p, i, c)))]
    if head_major:
        o_spec = pl.BlockSpec((tn // A_HEAD_DIM, tm, A_HEAD_DIM), lambda p, i, c: (2 * p + c, i, 0))
        o_shape = jax.ShapeDtypeStruct((ncols // A_HEAD_DIM, m, A_HEAD_DIM), out_dtype)
    else:
        o_spec = pl.BlockSpec((tm, tn), lambda p, i, c: (i, 2 * p + c))
        o_shape = jax.ShapeDtypeStruct((m, ncols), out_dtype)
    out_specs = [o_spec, pl.BlockSpec((ms, tn), lambda p, i, c: (0, sticky_tile(p, i, c)))]
    out_shape = [o_shape, jax.ShapeDtypeStruct((ms, ncols), F32)]
    riders, rider_args = [], []
    for k0, make in make_riders:
        rider = make(k0, flat_step)
        in_specs += rider["in_specs"]
        out_specs.append(rider["out_spec"])
        out_shape.append(rider["out_shape"])
        rider_args += rider["args"]
        riders.append((rider["copy"], len(rider["in_specs"]), k0, rider["steps"]))
        assert k0 + rider["steps"] <= nj * ni
    any_spec = pl.BlockSpec(memory_space=pl.ANY)
    scratch = [pltpu.VMEM((2, k, tn), BF16)]
    shift_k0s, k0 = [], 0
    for cache in shift_caches:
        in_specs.append(any_spec)
        out_specs.append(any_spec)
        out_shape.append(jax.ShapeDtypeStruct(cache.shape, cache.dtype))
        shift_k0s.append(k0)
        k0 += _shift_dma_steps(cache.shape[0], RIDER_ROWS)
    assert k0 <= nj * ni
    if shift_caches:
        scratch += [pltpu.VMEM((2, RIDER_ROWS) + shift_caches[0].shape[1:], shift_caches[0].dtype),
                    pltpu.SemaphoreType.DMA((2,)), pltpu.SemaphoreType.DMA((2,)),
                    pltpu.SemaphoreType.DMA((1,))]
    silu_tiles = ((silu_cols[0] - col0) // tn, (silu_cols[1] - col0) // tn)
    return pl.pallas_call(
        functools.partial(_in_proj_kernel, head_major=head_major, silu_tiles=silu_tiles,
                          riders=tuple(riders), shift_k0s=tuple(shift_k0s)),
        grid=(nj // 2, ni, 2),
        in_specs=in_specs,
        out_specs=out_specs,
        out_shape=out_shape,
        scratch_shapes=scratch,
        compiler_params=_params(("arbitrary", "arbitrary", "arbitrary")),
        name="in_proj_hm" if head_major else "in_proj",
    )(u, us, w, *rider_args, *shift_caches)


def _attn_kernel(slope_ref, q0, q1, q2, k0, k1, k2, v0, v1, v2, za_ref, o_ref,
                 qbuf, kbuf, vbuf, onat, lnat, stage, *, seq):
    q_refs, k_refs, v_refs = (q0, q1, q2), (k0, k1, k2), (v0, v1, v2)
    h = pl.program_id(1)
    t = pl.program_id(2)

    @pl.when(t == 0)
    def _():
        for g, dil in enumerate(DILATIONS):
            for r in range(dil):
                run = seq // dil + BAND
                kbuf[g, pl.ds(r * run, BAND), :] = jnp.zeros((BAND, A_HEAD_DIM), BF16)
                vbuf[g, r * run // BAND] = jnp.zeros((A_HEAD_DIM, BAND), BF16)

    kj = lax.broadcasted_iota(jnp.int32, (2 * BAND, BAND), 0)
    qi = lax.broadcasted_iota(jnp.int32, (2 * BAND, BAND), 1)
    dist = qi + BAND - kj
    valid = jnp.logical_and(dist >= 0, dist <= BAND)
    distf = dist.astype(F32)
    first_pen = jnp.where(jnp.logical_and(t == 0, kj < BAND), NEG, 0.0)
    scale = A_HEAD_DIM ** -0.5

    for g in range(N_GROUPS):
        dil = DILATIONS[g]
        n = ATT_TILE // dil
        nblk = n // BAND
        run = seq // dil + BAND
        srcs = (q_refs[g], k_refs[g], v_refs[g])
        if dil > SPLIT_STRIDE:
            coarse = ATT_TILE // SPLIT_STRIDE
            for a, src in enumerate(srcs):
                for r4 in range(SPLIT_STRIDE):
                    stage[a, r4 * coarse:(r4 + 1) * coarse, :] = src[pl.ds(r4, coarse, stride=SPLIT_STRIDE), :]
            srcs = (stage.at[0], stage.at[1], stage.at[2])
        for r in range(dil):
            if dil > SPLIT_STRIDE:
                rows = pl.ds((r % SPLIT_STRIDE) * coarse + r // SPLIT_STRIDE, n, stride=dil // SPLIT_STRIDE)
            else:
                rows = pl.ds(r, n, stride=dil) if dil > 1 else pl.ds(0, n)
            dst = pl.ds(pl.multiple_of(r * run + BAND + t * n, BAND), n)
            qbuf[g, r * n:(r + 1) * n, :] = (srcs[0][rows, :] * scale).astype(BF16)
            kbuf[g, dst, :] = srcs[1][rows, :].astype(BF16)
            v = srcs[2][rows, :]
            for i in range(nblk):
                blk = (r * run + BAND + i * BAND) // BAND + t * nblk
                vbuf[g, blk] = v[i * BAND:(i + 1) * BAND, :].T.astype(BF16)

        bias = jnp.where(valid, -(slope_ref[g, h] * float(dil)) * distf, NEG)
        bias_first = bias + first_pen

        for r in range(dil):
            for i in range(nblk):
                u = r * nblk + i
                blk = (r * run + i * BAND) // BAND + t * nblk
                kv_rows = pl.ds(pl.multiple_of(blk * BAND, BAND), 2 * BAND)
                q = qbuf[g, u * BAND:(u + 1) * BAND, :]
                s = lax.dot_general(kbuf[g, kv_rows, :], q, (((1,), (1,)), ((), ())),
                                    preferred_element_type=F32)
                s = s + (bias_first if i == 0 else bias)
                m = jnp.max(s, axis=0, keepdims=True)
                e = jnp.exp(s - m)
                den = jnp.sum(e, axis=0, keepdims=True)
                v_t = jnp.concatenate([vbuf[g, blk], vbuf[g, blk + 1]], axis=1)
                acc_t = jnp.dot(v_t, e.astype(BF16), preferred_element_type=F32)
                if dil == 1:
                    out_rows = pl.ds(u * BAND, BAND)
                else:
                    fine = dil // SPLIT_STRIDE
                    start = (r % SPLIT_STRIDE) * (ATT_TILE // SPLIT_STRIDE) + r // SPLIT_STRIDE + i * BAND * fine
                    out_rows = pl.ds(start, BAND, stride=fine) if fine > 1 else pl.ds(start, BAND)
                onat[g, out_rows, :] = (acc_t / den).T
                lnat[g, out_rows, :] = jnp.broadcast_to(m + jnp.log(den), (A_HEAD_DIM, BAND)).T

    quarter = ATT_TILE // SPLIT_STRIDE
    for r4 in range(SPLIT_STRIDE):
        strided = pl.ds(r4, quarter, stride=SPLIT_STRIDE)
        grouped = pl.ds(r4 * quarter, quarter)
        ls = [lnat[g, strided if dil == 1 else grouped, :] for g, dil in enumerate(DILATIONS)]
        os_ = [onat[g, strided if dil == 1 else grouped, :] for g, dil in enumerate(DILATIONS)]
        mx = jnp.maximum(jnp.maximum(ls[0], ls[1]), ls[2])
        w0, w1, w2 = jnp.exp(ls[0] - mx), jnp.exp(ls[1] - mx), jnp.exp(ls[2] - mx)
        mixed = (w0 * os_[0] + w1 * os_[1] + w2 * os_[2]) / (w0 + w1 + w2)
        o_ref[strided, :] = mixed * za_ref[strided, :]


def _attention(zh, slopes, batch, seq):
    nt = seq // ATT_TILE
    m = batch * seq

    def head_spec(c0):
        return pl.BlockSpec((None, ATT_TILE, A_HEAD_DIM), lambda b, h, t, s: (c0 + h, b * nt + t, 0))

    in_specs = [head_spec(c + g * A_HEADS) for c in (CH_Q, CH_K, CH_V) for g in range(N_GROUPS)]
    in_specs.append(head_spec(CH_ZA))
    buf_rows = seq + BAND * max(DILATIONS)
    grid_spec = pltpu.PrefetchScalarGridSpec(
        num_scalar_prefetch=1,
        grid=(batch, A_HEADS, nt),
        in_specs=in_specs,
        out_specs=pl.BlockSpec((ATT_TILE, A_HEAD_DIM), lambda b, h, t, s: (b * nt + t, h)),
        scratch_shapes=[
            pltpu.VMEM((N_GROUPS, ATT_TILE, A_HEAD_DIM), BF16),
            pltpu.VMEM((N_GROUPS, buf_rows, A_HEAD_DIM), BF16),
            pltpu.VMEM((N_GROUPS, buf_rows // BAND, A_HEAD_DIM, BAND), BF16),
            pltpu.VMEM((N_GROUPS, ATT_TILE, A_HEAD_DIM), F32),
            pltpu.VMEM((N_GROUPS, ATT_TILE, A_HEAD_DIM), F32),
            pltpu.VMEM((3, ATT_TILE, A_HEAD_DIM), F32),
        ],
    )
    return pl.pallas_call(
        functools.partial(_attn_kernel, seq=seq), grid_spec=grid_spec,
        out_shape=jax.ShapeDtypeStruct((m, A_WIDTH), F32),
        compiler_params=_params(("parallel", "parallel", "arbitrary")),
        name="dilated_attn",
    )(slopes, *([zh] * 10))


def _retention_kernel(lg_ref, q_ref, k_ref, v_ref, zr_ref, gret_ref, y_ref, rout_ref, r_scr):
    s = pl.program_id(2)
    heads = r_scr.shape[0]

    @pl.when(s == 0)
    def _():
        r_scr[...] = jnp.zeros_like(r_scr)

    L = R_CHUNK
    pi = lax.broadcasted_iota(jnp.int32, (L, L), 0)
    pj = lax.broadcasted_iota(jnp.int32, (L, L), 1)
    rel = (pi - pj).astype(F32)
    pos = lax.broadcasted_iota(jnp.int32, (L, 1), 0).astype(F32)
    k_scale = R_DIM ** -0.5
    decays = []
    for hp in range(heads):
        log_g = lg_ref[pl.program_id(1) * heads + hp]
        decays.append((
            jnp.where(rel >= 0, jnp.exp(jnp.maximum(rel, 0.0) * log_g), 0.0) * k_scale,
            jnp.exp((pos + 1.0) * log_g),
            jnp.exp((L - 1.0 - pos) * log_g) * k_scale,
            jnp.exp(jnp.full((1, 1), float(L), F32) * log_g)))

    col_slices = [slice(hp * R_DIM, (hp + 1) * R_DIM) for hp in range(heads)]

    def front(c):
        rows = pl.ds(c * L, L)
        qs = [q_ref[rows, cols] for cols in col_slices]
        ks = [k_ref[rows, cols] for cols in col_slices]
        vs = [v_ref[rows, cols] for cols in col_slices]
        scores = [(lax.dot_general(q, k, (((1,), (1,)), ((), ())), preferred_element_type=F32)
                   * d[0]).astype(BF16) for q, k, d in zip(qs, ks, decays)]
        crosses = [jnp.dot(q, r_scr[hp].astype(BF16), preferred_element_type=F32) * d[1]
                   for hp, (q, d) in enumerate(zip(qs, decays))]
        for hp, (k, v, d) in enumerate(zip(ks, vs, decays)):
            kd = (k.astype(F32) * d[2]).astype(BF16)
            r_scr[hp] = r_scr[hp] * d[3] + lax.dot_general(
                kd, v, (((0,), (0,)), ((), ())), preferred_element_type=F32)
        return rows, scores, crosses, vs

    def back(rows, scores, crosses, vs):
        for hp, cols in enumerate(col_slices):
            y = jnp.dot(scores[hp], vs[hp], preferred_element_type=F32) + crosses[hp]
            y = y * lax.rsqrt(jnp.mean(y * y, axis=-1, keepdims=True) + EPS)
            y = (y * gret_ref[:, cols]) * zr_ref[rows, cols].astype(F32)
            y_ref[rows, cols] = y.astype(y_ref.dtype)

    pending = None
    for c in range(q_ref.shape[0] // L):
        current = front(c)
        if pending is not None:
            back(*pending)
        pending = current
    back(*pending)

    @pl.when(s == pl.num_programs(2) - 1)
    def _():
        rout_ref[...] = r_scr[...]


def _retention(zr_all, log_g, g_ret, batch, seq, rb, heads):
    ns = seq // rb
    m = batch * seq
    width = heads * R_DIM

    def col_spec(col):
        c0 = (col - HEAD_COLS) // width
        return pl.BlockSpec((rb, width), lambda b, h, s, lg: (b * ns + s, c0 + h))

    grid_spec = pltpu.PrefetchScalarGridSpec(
        num_scalar_prefetch=1,
        grid=(batch, R_HEADS // heads, ns),
        in_specs=[col_spec(COL_QR), col_spec(COL_KR), col_spec(COL_VR), col_spec(COL_ZR),
                  pl.BlockSpec((1, width), lambda b, h, s, lg: (0, h))],
        out_specs=[pl.BlockSpec((rb, width), lambda b, h, s, lg: (b * ns + s, h)),
                   pl.BlockSpec((None, heads, R_DIM, R_DIM), lambda b, h, s, lg: (b, h, 0, 0))],
        scratch_shapes=[pltpu.VMEM((heads, R_DIM, R_DIM), F32)],
    )
    return pl.pallas_call(
        _retention_kernel, grid_spec=grid_spec,
        out_shape=[jax.ShapeDtypeStruct((m, R_WIDTH), BF16),
                   jax.ShapeDtypeStruct((batch, R_HEADS, R_DIM, R_DIM), F32)],
        compiler_params=_params(("parallel", "parallel", "arbitrary")),
        name="retention",
    )(log_g, zr_all, zr_all, zr_all, zr_all, g_ret.reshape(1, R_WIDTH))


def _out_kernel(ya_ref, yr_ref, ga_ref, gb_ref, x_ref, p_ref,
                wa_ref, wr_ref, wo_ref, gpost_ref, wg_ref, wp_ref, o_ref):
    a = jnp.dot(ya_ref[...].astype(BF16), wa_ref[...], preferred_element_type=F32)
    b = jnp.dot(yr_ref[...].astype(BF16), wr_ref[...], preferred_element_type=F32)
    merged = (jax.nn.sigmoid(ga_ref[...].astype(F32)) * a
              + jax.nn.sigmoid(gb_ref[...].astype(F32)) * b)
    y = jnp.dot(merged.astype(BF16), wo_ref[...], preferred_element_type=F32)
    y = y * lax.rsqrt(jnp.mean(y * y, axis=-1, keepdims=True) + EPS)
    hres = x_ref[...] + y * gpost_ref[...]
    gate = jax.nn.sigmoid(jnp.dot(hres.astype(BF16), wg_ref[...], preferred_element_type=F32))
    emb = jnp.dot(p_ref[...].astype(BF16), wp_ref[...], preferred_element_type=F32)
    o_ref[...] = hres + gate * emb


def _out_proj(ya, yr, gates, ga_blk, gb_blk, x, p, wa, wr, wo, g_post, wg, wp, tm):
    m = x.shape[0]

    def rows(width, cb=0):
        return pl.BlockSpec((tm, width), lambda i: (i, cb))

    def whole(arr):
        return pl.BlockSpec(arr.shape, lambda i: (0, 0), pipeline_mode=pl.Buffered(1))

    gp = g_post.reshape(1, D_MODEL)
    return pl.pallas_call(
        _out_kernel,
        grid=(m // tm,),
        in_specs=[rows(A_WIDTH), rows(R_WIDTH), rows(D_MODEL, ga_blk), rows(D_MODEL, gb_blk),
                  rows(D_MODEL), rows(PLE_DIM),
                  whole(wa), whole(wr), whole(wo), whole(gp), whole(wg), whole(wp)],
        out_specs=rows(D_MODEL),
        out_shape=jax.ShapeDtypeStruct((m, D_MODEL), F32),
        compiler_params=_params(("parallel",)),
        name="out_proj",
    )(ya, yr, gates, gates, x, p, wa, wr, wo, gp, wg, wp)


def _attn_step_kernel(slope_ref, z_ref, c0_ref, c1_ref, c2_ref, o_ref):
    c_refs = (c0_ref, c1_ref, c2_ref)
    scale = A_HEAD_DIM ** -0.5
    steps = float(BAND) - lax.broadcasted_iota(jnp.int32, (BAND, 1, 1), 0).astype(F32)
    outs, lses = [], []
    for g in range(N_GROUPS):
        q = z_ref[CH_Q + g * A_HEADS:CH_Q + (g + 1) * A_HEADS, :] * scale
        kn = z_ref[CH_K + g * A_HEADS:CH_K + (g + 1) * A_HEADS, :]
        vn = z_ref[CH_V + g * A_HEADS:CH_V + (g + 1) * A_HEADS, :]
        kc = c_refs[g][:, 0]
        vc = c_refs[g][:, 1]
        slope = slope_ref[g][None]
        s = jnp.sum(kc * q[None], axis=-1, keepdims=True) - (slope * float(DILATIONS[g])) * steps
        s_new = jnp.sum(kn * q, axis=-1, keepdims=True)
        m = jnp.maximum(jnp.max(s, axis=0), s_new)
        e = jnp.exp(s - m[None])
        e_new = jnp.exp(s_new - m)
        den = jnp.sum(e, axis=0) + e_new
        outs.append((jnp.sum(e * vc, axis=0) + e_new * vn) / den)
        lses.append(m + jnp.log(den))
    mx = jnp.maximum(jnp.maximum(lses[0], lses[1]), lses[2])
    ws = [jnp.exp(l - mx) for l in lses]
    mixed = (ws[0] * outs[0] + ws[1] * outs[1] + ws[2] * outs[2]) / (ws[0] + ws[1] + ws[2])
    o_ref[...] = mixed * _silu(z_ref[CH_ZA:CH_ZA + A_HEADS, :])


def _attention_step(zs_h, caches, slopes):
    b = zs_h.shape[0]
    z3 = zs_h.reshape(b, N_HEAD_CHUNKS, A_HEAD_DIM)
    views = [c.reshape(b, BAND, DILATIONS[g], 2, A_HEADS, A_HEAD_DIM) for g, c in enumerate(caches)]
    cache_spec = pl.BlockSpec((None, BAND, None, 2, A_HEADS, A_HEAD_DIM),
                              lambda i: (i, 0, 0, 0, 0, 0))
    out = pl.pallas_call(
        _attn_step_kernel,
        grid=(b,),
        in_specs=[pl.BlockSpec((N_GROUPS, A_HEADS, 1), lambda i: (0, 0, 0)),
                  pl.BlockSpec((None, N_HEAD_CHUNKS, A_HEAD_DIM), lambda i: (i, 0, 0)),
                  cache_spec, cache_spec, cache_spec],
        out_specs=pl.BlockSpec((None, A_HEADS, A_HEAD_DIM), lambda i: (i, 0, 0)),
        out_shape=jax.ShapeDtypeStruct((b, A_HEADS, A_HEAD_DIM), F32),
        compiler_params=_params(("parallel",)),
        name="dilated_attn_step",
    )(slopes.reshape(N_GROUPS, A_HEADS, 1), z3, *views)
    return out.reshape(b, A_WIDTH)


def _retention_step_kernel(lg_ref, q_ref, k_ref, v_ref, zr_ref, gret_ref, r_ref, y_ref, rout_ref):
    for hh in range(R_HEADS):
        cols = slice(hh * R_DIM, (hh + 1) * R_DIM)
        gamma = jnp.exp(jnp.full((1, 1), 1.0, F32) * lg_ref[hh])
        q = q_ref[:, cols]
        k = k_ref[:, cols] * (R_DIM ** -0.5)
        v = v_ref[:, cols]
        r_prev = r_ref[hh]
        qb = jnp.broadcast_to(q, (8, R_DIM)).astype(BF16)
        cross = jnp.dot(qb, r_prev.astype(BF16), preferred_element_type=F32)[0:1] * gamma
        inner = jnp.sum(q * k, axis=-1, keepdims=True) * v
        k_col = jnp.broadcast_to(k, (R_DIM, R_DIM)).T
        rout_ref[hh] = r_prev * gamma + k_col * v
        y = inner + cross
        y = y * lax.rsqrt(jnp.mean(y * y, axis=-1, keepdims=True) + EPS)
        y_ref[:, cols] = (y * gret_ref[:, cols]) * _silu(zr_ref[:, cols])


def _retention_step(zs_r, state, log_g, g_ret):
    b = zs_r.shape[0]
    z3 = zs_r.reshape(b, 1, REST_COLS)

    def col_spec(col):
        return pl.BlockSpec((None, 1, R_WIDTH), lambda i, lg: (i, 0, (col - HEAD_COLS) // R_WIDTH))

    state_spec = pl.BlockSpec((None, R_HEADS, R_DIM, R_DIM), lambda i, lg: (i, 0, 0, 0))
    grid_spec = pltpu.PrefetchScalarGridSpec(
        num_scalar_prefetch=1,
        grid=(b,),
        in_specs=[col_spec(COL_QR), col_spec(COL_KR), col_spec(COL_VR), col_spec(COL_ZR),
                  pl.BlockSpec((1, R_WIDTH), lambda i, lg: (0, 0)), state_spec],
        out_specs=[pl.BlockSpec((None, 1, R_WIDTH), lambda i, lg: (i, 0, 0)), state_spec],
    )
    y, new_state = pl.pallas_call(
        _retention_step_kernel, grid_spec=grid_spec,
        out_shape=[jax.ShapeDtypeStruct((b, 1, R_WIDTH), F32),
                   jax.ShapeDtypeStruct(state.shape, F32)],
        compiler_params=_params(("parallel",)),
        name="retention_step",
    )(log_g, z3, z3, z3, z3, g_ret.reshape(1, R_WIDTH), state)
    return y.reshape(b, R_WIDTH), new_state


def _window_new_row_kernel(z_ref, c0, c1, c2, o0, o1, o2):
    del c0, c1, c2
    for g, o_ref in enumerate((o0, o1, o2)):
        o_ref[0, 0] = z_ref[CH_K + g * A_HEADS:CH_K + (g + 1) * A_HEADS, :]
        o_ref[0, 1] = z_ref[CH_V + g * A_HEADS:CH_V + (g + 1) * A_HEADS, :]


def _window_new_row(zs_h, shifted):
    b = zs_h.shape[0]
    tile = (2, A_HEADS, A_HEAD_DIM)
    last_row = [pl.BlockSpec((None, 1) + tile, functools.partial(lambda i, w: (i, w - 1, 0, 0, 0), w=c.shape[1]))
                for c in shifted]
    return pl.pallas_call(
        _window_new_row_kernel,
        grid=(b,),
        in_specs=[pl.BlockSpec((None, N_HEAD_CHUNKS, A_HEAD_DIM), lambda i: (i, 0, 0))]
                 + [pl.BlockSpec(memory_space=pl.ANY)] * len(shifted),
        out_specs=last_row,
        out_shape=[jax.ShapeDtypeStruct(c.shape, c.dtype) for c in shifted],
        input_output_aliases={1: 0, 2: 1, 3: 2},
        compiler_params=_params(("parallel",)),
        name="window_new_row",
    )(zs_h.reshape(b, N_HEAD_CHUNKS, A_HEAD_DIM), *shifted)


def _alibi_slopes():
    n = N_GROUPS * A_HEADS
    return jnp.exp2(-8.0 * (jnp.arange(n, dtype=F32) + 1.0) / n).reshape(N_GROUPS, A_HEADS)


def _retention_log_decay():
    return jnp.log1p(-jnp.exp2(-5.0 - jnp.arange(R_HEADS, dtype=F32)))


def kernel(x_prompt, x_sample, cache_win0, cache_win1, cache_win2, state_ret, p_prompt, p_sample,
           g_pre, w_in, g_ret, w_a_out, w_r_out, w_o, g_post, w_ple_gate, w_ple_proj):
    batch, seq, _ = x_prompt.shape
    dec_batch = x_sample.shape[0]
    assert g_pre.shape[0] == 1 and x_sample.shape[1] == 1
    assert seq % ATT_TILE == 0
    m = batch * seq
    slopes = _alibi_slopes()
    log_g = _retention_log_decay()

    xp = x_prompt.reshape(m, D_MODEL)
    xs = x_sample.reshape(dec_batch, D_MODEL)
    u = _rmsnorm(xp, g_pre[0], TM_NORM)
    us = _rmsnorm(xs, g_pre[0], dec_batch)
    caches = (cache_win0[0], cache_win1[0], cache_win2[0])
    flat = [c.reshape((-1,) + c.shape[2:]) for c in caches]
    casts = [(0, functools.partial(_cast_rider, wt[0], CAST_ROWS))
             for wt in (w_a_out, w_r_out, w_o, w_ple_gate, w_ple_proj)]
    zh, zs_h, wa, wr, wo, wg, wp, sh1, sh0 = _in_proj(
        u, us, w_in[0], 0, HEAD_COLS, F32, TM_IN, TN_IN, True, (COL_ZA, COL_QR), casts,
        shift_caches=[flat[1], flat[0]])
    layout = [functools.partial(_window_prompt_rider, zh, g, batch, seq, RIDER_ROWS) for g in range(N_GROUPS)]
    np0 = batch * 2 * (min(WINDOWS[0], seq) // min(RIDER_ROWS, WINDOWS[0]))
    np1 = batch * 2 * (min(WINDOWS[1], seq) // min(RIDER_ROWS, WINDOWS[1]))
    zr, zs_r, wp2, wp1, wp0, sh2 = _in_proj(
        u, us, w_in[0], HEAD_COLS, REST_COLS, BF16, TM_IN, TN_IN, False, (COL_ZR, COL_GA),
        [(np0 + np1, layout[2]), (np0, layout[1]), (0, layout[0])], shift_caches=[flat[2]])
    win_p = [wp.reshape(1, batch, -1, 2, A_HEADS, A_HEAD_DIM) for wp in (wp0, wp1, wp2)]
    shifted = [s.reshape(c.shape) for s, c in zip((sh0, sh1, sh2), caches)]
    ga_blk = (COL_GA - HEAD_COLS) // D_MODEL

    ya = _attention(zh, slopes, batch, seq)
    yr, ret_prompt = _retention(zr, log_g, g_ret[0], batch, seq, RET_ROWS, RET_HEADS)
    y_prompt = _out_proj(ya, yr, zr, ga_blk, ga_blk + 1, xp, p_prompt[0].reshape(m, PLE_DIM),
                         wa, wr, wo, g_post[0], wg, wp, TM_OUT).reshape(batch, seq, D_MODEL)

    ya_s = _attention_step(zs_h, caches, slopes)
    yr_s, ret_sample = _retention_step(zs_r, state_ret[0], log_g, g_ret[0])
    y_sample = _out_proj(ya_s, yr_s, zs_r, ga_blk, ga_blk + 1, xs, p_sample[0].reshape(dec_batch, PLE_DIM),
                         wa, wr, wo, g_post[0], wg, wp, dec_batch).reshape(dec_batch, 1, D_MODEL)

    win_s = _window_new_row(zs_h, shifted)
    return (y_prompt, y_sample, win_p[0], win_p[1], win_p[2], ret_prompt[None],
            win_s[0][None], win_s[1][None], win_s[2][None], ret_sample[None])
```

```python
import functools

import jax
import jax.numpy as jnp
from jax import lax
from jax.experimental import pallas as pl
from jax.experimental.pallas import tpu as pltpu

F32 = jnp.float32
BF16 = jnp.bfloat16

D_MODEL = 2048
N_GROUPS = 3
DILATIONS = (1, 4, 16)
WINDOWS = (128, 512, 2048)
BAND = 128
A_HEADS = 8
A_HEAD_DIM = 128
A_QKV = N_GROUPS * A_HEADS * A_HEAD_DIM
A_WIDTH = A_HEADS * A_HEAD_DIM
R_HEADS = 8
R_DIM = 256
R_WIDTH = R_HEADS * R_DIM
R_CHUNK = 128
PLE_DIM = 256
EPS = 1e-6
N_IN = 3 * A_QKV + A_WIDTH + 4 * R_WIDTH + 2 * D_MODEL

COL_QA, COL_KA, COL_VA = 0, A_QKV, 2 * A_QKV
COL_ZA = 3 * A_QKV
COL_QR = COL_ZA + A_WIDTH
COL_KR = COL_QR + R_WIDTH
COL_VR = COL_KR + R_WIDTH
COL_ZR = COL_VR + R_WIDTH
COL_GA = COL_ZR + R_WIDTH
COL_GB = COL_GA + D_MODEL

HEAD_COLS = COL_QR
REST_COLS = N_IN - HEAD_COLS
CH_Q, CH_K, CH_V, CH_ZA = (c // A_HEAD_DIM for c in (COL_QA, COL_KA, COL_VA, COL_ZA))
N_HEAD_CHUNKS = HEAD_COLS // A_HEAD_DIM

PACKED_ROWS = 16
NEG = -1e30
ATT_TILE = BAND * max(DILATIONS)
SPLIT_STRIDE = 4
assert all(d == 1 or d % SPLIT_STRIDE == 0 for d in DILATIONS)
ATT_SKEW = 2

assert R_DIM ** -0.5 == 2.0 ** -4

TM_NORM = 512
TM_IN, TN_IN = 1024, 1024
RIDER_ROWS = 256
CAST_ROWS = 64
RET_ROWS = 1024
RET_HEADS = 4
TM_OUT = 256
VMEM_LIMIT = 60 * 1024 * 1024


def _params(semantics, vmem=VMEM_LIMIT):
    return pltpu.CompilerParams(dimension_semantics=semantics, vmem_limit_bytes=vmem)


def _silu(x):
    return x * jax.nn.sigmoid(x)


def _rmsnorm_kernel(x_ref, g_ref, o_ref):
    x = x_ref[...]
    y = x * lax.rsqrt(jnp.mean(x * x, axis=-1, keepdims=True) + EPS)
    o_ref[...] = (y * g_ref[...]).astype(o_ref.dtype)


def _rmsnorm(x, g, tm):
    m, d = x.shape
    return pl.pallas_call(
        _rmsnorm_kernel,
        grid=(m // tm,),
        in_specs=[pl.BlockSpec((tm, d), lambda i: (i, 0)),
                  pl.BlockSpec((1, d), lambda i: (0, 0))],
        out_specs=pl.BlockSpec((tm, d), lambda i: (i, 0)),
        out_shape=jax.ShapeDtypeStruct((m, d), BF16),
        compiler_params=_params(("parallel",)),
        name="rmsnorm",
    )(x, g.reshape(1, d))


def _heads_to_rows_copy(in_ref, out_ref):
    for hh in range(A_HEADS):
        out_ref[:, hh, :] = in_ref[hh]


def _shift_dma_step(step, cache_ref, out_ref, buf, rsem, wsem, tsem, k0, rb):
    rows = cache_ref.shape[0]
    chunks = -(-(rows - 1) // rb)
    ci = step - k0

    def start_row(n):
        return jnp.minimum(n * rb, rows - 1 - rb)

    def read(n):
        return pltpu.make_async_copy(cache_ref.at[pl.ds(start_row(n) + 1, rb)], buf.at[n % 2], rsem.at[n % 2])

    def write(n):
        return pltpu.make_async_copy(buf.at[n % 2], out_ref.at[pl.ds(start_row(n), rb)], wsem.at[n % 2])

    tail = pltpu.make_async_copy(cache_ref.at[pl.ds(rows - 1, 1)], out_ref.at[pl.ds(rows - 1, 1)], tsem.at[0])

    @pl.when(ci == 0)
    def _():
        read(ci).start()
        tail.start()

    @pl.when(ci == chunks)
    def _():
        tail.wait()

    @pl.when(jnp.logical_and(ci >= 1, ci <= chunks))
    def _():
        write(ci - 1).wait()

    @pl.when(jnp.logical_and(ci >= 0, ci < chunks))
    def _():
        read(ci).wait()
        write(ci).start()

    @pl.when(jnp.logical_and(ci >= 0, ci + 1 < chunks))
    def _():
        read(ci + 1).start()


def _shift_dma_steps(rows, rb):
    return -(-(rows - 1) // rb) + 1


def _in_proj_kernel(*refs, head_major, silu_tiles, riders, shift_k0s):
    n_in = sum(r[1] for r in riders)
    ns = len(shift_k0s)
    u_ref, us_ref, w_ref = refs[:3]
    rider_in = refs[3:3 + n_in]
    shift_in = refs[3 + n_in:3 + n_in + ns]
    outs = refs[3 + n_in + ns:]
    o_ref, os_ref = outs[:2]
    rider_out = outs[2:2 + len(riders)]
    shift_out = outs[2 + len(riders):2 + len(riders) + ns]
    wb_ref = outs[2 + len(riders) + ns]
    c = pl.program_id(2)
    j = 2 * pl.program_id(0) + c

    @pl.when(pl.program_id(1) == 0)
    def _():
        wb_ref[c] = w_ref[...].astype(BF16)
        os_ref[...] = jnp.dot(us_ref[...], wb_ref[c], preferred_element_type=F32)

    def project(act):
        acc = jnp.dot(u_ref[...], wb_ref[c], preferred_element_type=F32)
        if act is not None:
            acc = act(acc)
        if head_major:
            for ch in range(o_ref.shape[0]):
                o_ref[ch] = acc[:, ch * A_HEAD_DIM:(ch + 1) * A_HEAD_DIM].astype(o_ref.dtype)
        else:
            o_ref[...] = acc.astype(o_ref.dtype)

    gated = jnp.logical_and(j >= silu_tiles[0], j < silu_tiles[1])
    pl.when(gated)(functools.partial(project, _silu))
    pl.when(jnp.logical_not(gated))(functools.partial(project, None))

    step = (pl.program_id(0) * pl.num_programs(1) + pl.program_id(1)) * 2 + c
    pos = 0
    for (copy, nin, k0, nsteps), out_ref in zip(riders, rider_out):
        active = jnp.logical_and(step >= k0, step < k0 + nsteps)
        pl.when(active)(functools.partial(copy, *rider_in[pos:pos + nin], out_ref))
        pos += nin

    if ns:
        buf, rsem, wsem, tsem = outs[3 + len(riders) + ns:7 + len(riders) + ns]
        for cache_ref, out_ref, k0 in zip(shift_in, shift_out, shift_k0s):
            _shift_dma_step(step, cache_ref, out_ref, buf, rsem, wsem, tsem, k0, buf.shape[1])


def _cast_copy(in_ref, out_ref):
    out_ref[...] = in_ref[...].astype(out_ref.dtype)


def _cast_rider(w, rb, k0, flat_step):
    rows, cols = w.shape
    nb = rows // rb

    def row_map(*idx):
        return jnp.clip(flat_step(*idx) - k0, 0, nb - 1), 0

    return dict(copy=_cast_copy, args=[w], steps=nb,
                in_specs=[pl.BlockSpec((rb, cols), row_map)],
                out_spec=pl.BlockSpec((rb, cols), row_map),
                out_shape=jax.ShapeDtypeStruct(w.shape, BF16))


def _window_prompt_rider(zh, g, batch, seq, rb, k0, flat_step):
    keep = min(WINDOWS[g], seq)
    rb = min(rb, keep)
    nsb = keep // rb
    row0 = (seq - keep) // rb

    def loc(*idx):
        l = jnp.clip(flat_step(*idx) - k0, 0, batch * 2 * nsb - 1)
        return l // (2 * nsb), (l // nsb) % 2, l % nsb

    def in_map(*idx):
        b, kv, s = loc(*idx)
        return (CH_K + kv * (CH_V - CH_K)) // A_HEADS + g, b * (seq // rb) + row0 + s, 0

    def out_map(*idx):
        b, kv, s = loc(*idx)
        return b * nsb + s, kv, 0

    return dict(copy=_heads_to_rows_copy, args=[zh], steps=batch * 2 * nsb,
                in_specs=[pl.BlockSpec((A_HEADS, rb, A_HEAD_DIM), in_map)],
                out_spec=pl.BlockSpec((rb, A_HEADS, A_HEAD_DIM), out_map),
                out_shape=jax.ShapeDtypeStruct((batch * keep, 2 * A_HEADS, A_HEAD_DIM), zh.dtype))


def _in_proj(u, us, w, col0, ncols, out_dtype, tm, tn, head_major, silu_cols, make_riders=(),
             shift_caches=()):
    m, k = u.shape
    ms = us.shape[0]
    j0 = col0 // tn
    nj, ni = ncols // tn, m // tm
    assert nj % 2 == 0

    def flat_step(p, i, c):
        return (p * ni + i) * 2 + c

    def sticky_tile(p, i, c):
        return 2 * p + jnp.where(i == 0, c, 1)

    in_specs = [pl.BlockSpec((tm, k), lambda p, i, c: (i, 0)),
                pl.BlockSpec((ms, k), lambda p, i, c: (0, 0)),
                pl.BlockSpec((k, tn), lambda p, i, c: (0, j0 + sticky_tile(p, i, c)))]
    if head_major:
        o_spec = pl.BlockSpec((tn // A_HEAD_DIM, tm, A_HEAD_DIM), lambda p, i, c: (2 * p + c, i, 0))
        o_shape = jax.ShapeDtypeStruct((ncols // A_HEAD_DIM, m, A_HEAD_DIM), out_dtype)
    else:
        o_spec = pl.BlockSpec((tm, tn), lambda p, i, c: (i, 2 * p + c))
        o_shape = jax.ShapeDtypeStruct((m, ncols), out_dtype)
    out_specs = [o_spec, pl.BlockSpec((ms, tn), lambda p, i, c: (0, sticky_tile(p, i, c)))]
    out_shape = [o_shape, jax.ShapeDtypeStruct((ms, ncols), F32)]
    riders, rider_args = [], []
    for k0, make in make_riders:
        rider = make(k0, flat_step)
        in_specs += rider["in_specs"]
        out_specs.append(rider["out_spec"])
        out_shape.append(rider["out_shape"])
        rider_args += rider["args"]
        riders.append((rider["copy"], len(rider["in_specs"]), k0, rider["steps"]))
        assert k0 + rider["steps"] <= nj * ni
    any_spec = pl.BlockSpec(memory_space=pl.ANY)
    scratch = [pltpu.VMEM((2, k, tn), BF16)]
    shift_k0s, k0 = [], 0
    for cache in shift_caches:
        in_specs.append(any_spec)
        out_specs.append(any_spec)
        out_shape.append(jax.ShapeDtypeStruct(cache.shape, cache.dtype))
        shift_k0s.append(k0)
        k0 += _shift_dma_steps(cache.shape[0], RIDER_ROWS)
    assert k0 <= nj * ni
    if shift_caches:
        scratch += [pltpu.VMEM((2, RIDER_ROWS) + shift_caches[0].shape[1:], shift_caches[0].dtype),
                    pltpu.SemaphoreType.DMA((2,)), pltpu.SemaphoreType.DMA((2,)),
                    pltpu.SemaphoreType.DMA((1,))]
    silu_tiles = ((silu_cols[0] - col0) // tn, (silu_cols[1] - col0) // tn)
    return pl.pallas_call(
        functools.partial(_in_proj_kernel, head_major=head_major, silu_tiles=silu_tiles,
                          riders=tuple(riders), shift_k0s=tuple(shift_k0s)),
        grid=(nj // 2, ni, 2),
        in_specs=in_specs,
        out_specs=out_specs,
        out_shape=out_shape,
        scratch_shapes=scratch,
        compiler_params=_params(("arbitrary", "arbitrary", "arbitrary")),
        name="in_proj_hm" if head_major else "in_proj",
    )(u, us, w, *rider_args, *shift_caches)


def _attn_kernel(slope_ref, q0, q1, q2, k0, k1, k2, v0, v1, v2, za_ref, o_ref,
                 qbuf, kbuf, vbuf, onat, lnat, stage, *, seq):
    q_refs, k_refs, v_refs = (q0, q1, q2), (k0, k1, k2), (v0, v1, v2)
    h = pl.program_id(1)
    t = pl.program_id(2)

    @pl.when(t == 0)
    def _():
        for g, dil in enumerate(DILATIONS):
            for r in range(dil):
                run = seq // dil + BAND
                kbuf[g, pl.ds(r * run, BAND), :] = jnp.zeros((BAND, A_HEAD_DIM), BF16)
                vbuf[g, r * run // BAND] = jnp.zeros((A_HEAD_DIM, BAND), BF16)

    kj = lax.broadcasted_iota(jnp.int32, (2 * BAND, BAND), 0)
    qi = lax.broadcasted_iota(jnp.int32, (2 * BAND, BAND), 1)
    dist = qi + BAND - kj
    valid = jnp.logical_and(dist >= 0, dist <= BAND)
    distf = dist.astype(F32)
    first_pen = jnp.where(jnp.logical_and(t == 0, kj < BAND), NEG, 0.0)
    scale = A_HEAD_DIM ** -0.5

    for g in range(N_GROUPS):
        dil = DILATIONS[g]
        n = ATT_TILE // dil
        nblk = n // BAND
        run = seq // dil + BAND
        srcs = (q_refs[g], k_refs[g], v_refs[g])
        if dil > SPLIT_STRIDE:
            coarse = ATT_TILE // SPLIT_STRIDE
            for a, src in enumerate(srcs):
                for r4 in range(SPLIT_STRIDE):
                    stage[a, r4 * coarse:(r4 + 1) * coarse, :] = src[pl.ds(r4, coarse, stride=SPLIT_STRIDE), :]
            srcs = (stage.at[0], stage.at[1], stage.at[2])
        for r in range(dil):
            if dil > SPLIT_STRIDE:
                rows = pl.ds((r % SPLIT_STRIDE) * coarse + r // SPLIT_STRIDE, n, stride=dil // SPLIT_STRIDE)
            else:
                rows = pl.ds(r, n, stride=dil) if dil > 1 else pl.ds(0, n)
            dst = pl.ds(pl.multiple_of(r * run + BAND + t * n, BAND), n)
            qbuf[g, r * n:(r + 1) * n, :] = (srcs[0][rows, :] * scale).astype(BF16)
            kbuf[g, dst, :] = srcs[1][rows, :].astype(BF16)
            v = srcs[2][rows, :]
            for i in range(nblk):
                blk = (r * run + BAND + i * BAND) // BAND + t * nblk
                vbuf[g, blk] = v[i * BAND:(i + 1) * BAND, :].T.astype(BF16)

        bias = jnp.where(valid, -(slope_ref[g, h] * float(dil)) * distf, NEG)
        bias_first = bias + first_pen

        def scores_of(r, i, g=g, nblk=nblk, run=run, bias=bias, bias_first=bias_first):
            blk = (r * run + i * BAND) // BAND + t * nblk
            kv_rows = pl.ds(pl.multiple_of(blk * BAND, BAND), 2 * BAND)
            q = qbuf[g, (r * nblk + i) * BAND:(r * nblk + i + 1) * BAND, :]
            s = lax.dot_general(kbuf[g, kv_rows, :], q, (((1,), (1,)), ((), ())),
                                preferred_element_type=F32)
            return r, i, blk, s + (bias_first if i == 0 else bias)

        def finish(r, i, blk, s, g=g, dil=dil, nblk=nblk):
            m = jnp.max(s, axis=0, keepdims=True)
            e = jnp.exp(s - m)
            den = jnp.sum(e, axis=0, keepdims=True)
            v_t = jnp.concatenate([vbuf[g, blk], vbuf[g, blk + 1]], axis=1)
            acc_t = jnp.dot(v_t, e.astype(BF16), preferred_element_type=F32)
            if dil == 1:
                out_rows = pl.ds((r * nblk + i) * BAND, BAND)
            else:
                fine = dil // SPLIT_STRIDE
                start = (r % SPLIT_STRIDE) * (ATT_TILE // SPLIT_STRIDE) + r // SPLIT_STRIDE + i * BAND * fine
                out_rows = pl.ds(start, BAND, stride=fine) if fine > 1 else pl.ds(start, BAND)
            onat[g, out_rows, :] = (acc_t / den).T
            lnat[g, out_rows, :] = jnp.broadcast_to(m + jnp.log(den), (A_HEAD_DIM, BAND)).T

        in_flight = []
        for r in range(dil):
            for i in range(nblk):
                in_flight.append(scores_of(r, i))
                if len(in_flight) > ATT_SKEW:
                    finish(*in_flight.pop(0))
        for unit in in_flight:
            finish(*unit)

    quarter = ATT_TILE // SPLIT_STRIDE
    for r4 in range(SPLIT_STRIDE):
        strided = pl.ds(r4, quarter, stride=SPLIT_STRIDE)
        grouped = pl.ds(r4 * quarter, quarter)
        ls = [lnat[g, strided if dil == 1 else grouped, :] for g, dil in enumerate(DILATIONS)]
        os_ = [onat[g, strided if dil == 1 else grouped, :] for g, dil in enumerate(DILATIONS)]
        mx = jnp.maximum(jnp.maximum(ls[0], ls[1]), ls[2])
        w0, w1, w2 = jnp.exp(ls[0] - mx), jnp.exp(ls[1] - mx), jnp.exp(ls[2] - mx)
        mixed = (w0 * os_[0] + w1 * os_[1] + w2 * os_[2]) / (w0 + w1 + w2)
        o_ref[strided, :] = mixed * za_ref[strided, :]


def _attention(zh, slopes, batch, seq):
    nt = seq // ATT_TILE
    m = batch * seq

    def head_spec(c0):
        return pl.BlockSpec((None, ATT_TILE, A_HEAD_DIM), lambda b, h, t, s: (c0 + h, b * nt + t, 0))

    in_specs = [head_spec(c + g * A_HEADS) for c in (CH_Q, CH_K, CH_V) for g in range(N_GROUPS)]
    in_specs.append(head_spec(CH_ZA))
    buf_rows = seq + BAND * max(DILATIONS)
    grid_spec = pltpu.PrefetchScalarGridSpec(
        num_scalar_prefetch=1,
        grid=(batch, A_HEADS, nt),
        in_specs=in_specs,
        out_specs=pl.BlockSpec((ATT_TILE, A_HEAD_DIM), lambda b, h, t, s: (b * nt + t, h)),
        scratch_shapes=[
            pltpu.VMEM((N_GROUPS, ATT_TILE, A_HEAD_DIM), BF16),
            pltpu.VMEM((N_GROUPS, buf_rows, A_HEAD_DIM), BF16),
            pltpu.VMEM((N_GROUPS, buf_rows // BAND, A_HEAD_DIM, BAND), BF16),
            pltpu.VMEM((N_GROUPS, ATT_TILE, A_HEAD_DIM), F32),
            pltpu.VMEM((N_GROUPS, ATT_TILE, A_HEAD_DIM), F32),
            pltpu.VMEM((3, ATT_TILE, A_HEAD_DIM), F32),
        ],
    )
    return pl.pallas_call(
        functools.partial(_attn_kernel, seq=seq), grid_spec=grid_spec,
        out_shape=jax.ShapeDtypeStruct((m, A_WIDTH), F32),
        compiler_params=_params(("parallel", "parallel", "arbitrary")),
        name="dilated_attn",
    )(slopes, *([zh] * 10))


def _retention_kernel(lg_ref, q_ref, k_ref, v_ref, zr_ref, gret_ref, y_ref, rout_ref, r_scr):
    s = pl.program_id(2)
    heads = r_scr.shape[0]

    @pl.when(s == 0)
    def _():
        r_scr[...] = jnp.zeros_like(r_scr)

    L = R_CHUNK
    pi = lax.broadcasted_iota(jnp.int32, (L, L), 0)
    pj = lax.broadcasted_iota(jnp.int32, (L, L), 1)
    rel = (pi - pj).astype(F32)
    pos = lax.broadcasted_iota(jnp.int32, (L, 1), 0).astype(F32)
    k_scale = R_DIM ** -0.5
    decays = []
    for hp in range(heads):
        log_g = lg_ref[pl.program_id(1) * heads + hp]
        decays.append((
            jnp.where(rel >= 0, jnp.exp(jnp.maximum(rel, 0.0) * log_g), 0.0) * k_scale,
            jnp.exp((pos + 1.0) * log_g),
            jnp.exp((L - 1.0 - pos) * log_g) * k_scale,
            jnp.exp(jnp.full((1, 1), float(L), F32) * log_g)))

    col_slices = [slice(hp * R_DIM, (hp + 1) * R_DIM) for hp in range(heads)]

    def front(c):
        rows = pl.ds(c * L, L)
        qs = [q_ref[rows, cols] for cols in col_slices]
        ks = [k_ref[rows, cols] for cols in col_slices]
        vs = [v_ref[rows, cols] for cols in col_slices]
        scores = [(lax.dot_general(q, k, (((1,), (1,)), ((), ())), preferred_element_type=F32)
                   * d[0]).astype(BF16) for q, k, d in zip(qs, ks, decays)]
        crosses = [jnp.dot(q, r_scr[hp].astype(BF16), preferred_element_type=F32) * d[1]
                   for hp, (q, d) in enumerate(zip(qs, decays))]
        for hp, (k, v, d) in enumerate(zip(ks, vs, decays)):
            kd = (k.astype(F32) * d[2]).astype(BF16)
            r_scr[hp] = r_scr[hp] * d[3] + lax.dot_general(
                kd, v, (((0,), (0,)), ((), ())), preferred_element_type=F32)
        return rows, scores, crosses, vs

    def back(rows, scores, crosses, vs):
        for hp, cols in enumerate(col_slices):
            y = jnp.dot(scores[hp], vs[hp], preferred_element_type=F32) + crosses[hp]
            y = y * lax.rsqrt(jnp.mean(y * y, axis=-1, keepdims=True) + EPS)
            y = (y * gret_ref[:, cols]) * zr_ref[rows, cols].astype(F32)
            y_ref[rows, cols] = y.astype(y_ref.dtype)

    pending = None
    for c in range(q_ref.shape[0] // L):
        current = front(c)
        if pending is not None:
            back(*pending)
        pending = current
    back(*pending)

    @pl.when(s == pl.num_programs(2) - 1)
    def _():
        rout_ref[...] = r_scr[...]


def _retention(zr_all, log_g, g_ret, batch, seq, rb, heads):
    ns = seq // rb
    m = batch * seq
    width = heads * R_DIM

    def col_spec(col):
        c0 = (col - HEAD_COLS) // width
        return pl.BlockSpec((rb, width), lambda b, h, s, lg: (b * ns + s, c0 + h))

    grid_spec = pltpu.PrefetchScalarGridSpec(
        num_scalar_prefetch=1,
        grid=(batch, R_HEADS // heads, ns),
        in_specs=[col_spec(COL_QR), col_spec(COL_KR), col_spec(COL_VR), col_spec(COL_ZR),
                  pl.BlockSpec((1, width), lambda b, h, s, lg: (0, h))],
        out_specs=[pl.BlockSpec((rb, width), lambda b, h, s, lg: (b * ns + s, h)),
                   pl.BlockSpec((None, heads, R_DIM, R_DIM), lambda b, h, s, lg: (b, h, 0, 0))],
        scratch_shapes=[pltpu.VMEM((heads, R_DIM, R_DIM), F32)],
    )
    return pl.pallas_call(
        _retention_kernel, grid_spec=grid_spec,
        out_shape=[jax.ShapeDtypeStruct((m, R_WIDTH), BF16),
                   jax.ShapeDtypeStruct((batch, R_HEADS, R_DIM, R_DIM), F32)],
        compiler_params=_params(("parallel", "parallel", "arbitrary")),
        name="retention",
    )(log_g, zr_all, zr_all, zr_all, zr_all, g_ret.reshape(1, R_WIDTH))


def _out_kernel(ya_ref, yr_ref, ga_ref, gb_ref, x_ref, p_ref,
                ya_s, yr_s, ga_s, gb_s, x_s, p_s,
                wa_ref, wr_ref, wo_ref, gpost_ref, wg_ref, wp_ref, o_ref, os_ref):
    def chain(ya, yr, ga, gb, x, p):
        a = jnp.dot(ya.astype(BF16), wa_ref[...], preferred_element_type=F32)
        b = jnp.dot(yr.astype(BF16), wr_ref[...], preferred_element_type=F32)
        merged = jax.nn.sigmoid(ga.astype(F32)) * a + jax.nn.sigmoid(gb.astype(F32)) * b
        y = jnp.dot(merged.astype(BF16), wo_ref[...], preferred_element_type=F32)
        y = y * lax.rsqrt(jnp.mean(y * y, axis=-1, keepdims=True) + EPS)
        hres = x + y * gpost_ref[...]
        gate = jax.nn.sigmoid(jnp.dot(hres.astype(BF16), wg_ref[...], preferred_element_type=F32))
        emb = jnp.dot(p.astype(BF16), wp_ref[...], preferred_element_type=F32)
        return hres + gate * emb

    tile = (ya_ref, yr_ref, ga_ref, gb_ref, x_ref, p_ref)
    extra = (ya_s, yr_s, ga_s, gb_s, x_s, p_s)
    tm, ms = o_ref.shape[0], os_ref.shape[0]

    @pl.when(pl.program_id(0) == 0)
    def _():
        pad = -ms % PACKED_ROWS
        both = [jnp.concatenate([t[...].astype(F32), e[...].astype(F32),
                                 jnp.zeros((pad, t.shape[1]), F32)], axis=0)
                for t, e in zip(tile, extra)]
        out = chain(*both)
        o_ref[...] = out[:tm]
        os_ref[...] = out[tm:tm + ms]

    @pl.when(pl.program_id(0) != 0)
    def _():
        o_ref[...] = chain(*(t[...] for t in tile))


def _out_proj(ya, yr, gates, ya_s, yr_s, gates_s, ga_blk, gb_blk, x, p, x_s, p_s,
              wa, wr, wo, g_post, wg, wp, tm):
    m, ms = x.shape[0], x_s.shape[0]

    def rows(width, cb=0):
        return pl.BlockSpec((tm, width), lambda i: (i, cb))

    def few(width, cb=0):
        return pl.BlockSpec((ms, width), lambda i: (0, cb))

    def whole(arr):
        return pl.BlockSpec(arr.shape, lambda i: (0, 0), pipeline_mode=pl.Buffered(1))

    gp = g_post.reshape(1, D_MODEL)
    return pl.pallas_call(
        _out_kernel,
        grid=(m // tm,),
        in_specs=[rows(A_WIDTH), rows(R_WIDTH), rows(D_MODEL, ga_blk), rows(D_MODEL, gb_blk),
                  rows(D_MODEL), rows(PLE_DIM),
                  few(A_WIDTH), few(R_WIDTH), few(D_MODEL, ga_blk), few(D_MODEL, gb_blk),
                  few(D_MODEL), few(PLE_DIM),
                  whole(wa), whole(wr), whole(wo), whole(gp), whole(wg), whole(wp)],
        out_specs=[rows(D_MODEL), few(D_MODEL)],
        out_shape=[jax.ShapeDtypeStruct((m, D_MODEL), F32), jax.ShapeDtypeStruct((ms, D_MODEL), F32)],
        compiler_params=_params(("arbitrary",)),
        name="out_proj",
    )(ya, yr, gates, gates, x, p, ya_s, yr_s, gates_s, gates_s, x_s, p_s, wa, wr, wo, gp, wg, wp)


def _attn_step_kernel(slope_ref, z_ref, c0_ref, c1_ref, c2_ref, s0, s1, s2, o_ref, w0, w1, w2):
    del s0, s1, s2
    c_refs = (c0_ref, c1_ref, c2_ref)
    scale = A_HEAD_DIM ** -0.5
    steps = float(BAND) - lax.broadcasted_iota(jnp.int32, (BAND, 1, 1), 0).astype(F32)
    outs, lses = [], []
    for g in range(N_GROUPS):
        q = z_ref[CH_Q + g * A_HEADS:CH_Q + (g + 1) * A_HEADS, :] * scale
        kn = z_ref[CH_K + g * A_HEADS:CH_K + (g + 1) * A_HEADS, :]
        vn = z_ref[CH_V + g * A_HEADS:CH_V + (g + 1) * A_HEADS, :]
        (w0, w1, w2)[g][0, 0] = kn
        (w0, w1, w2)[g][0, 1] = vn
        kc = c_refs[g][:, 0]
        vc = c_refs[g][:, 1]
        slope = slope_ref[g][None]
        s = jnp.sum(kc * q[None], axis=-1, keepdims=True) - (slope * float(DILATIONS[g])) * steps
        s_new = jnp.sum(kn * q, axis=-1, keepdims=True)
        m = jnp.maximum(jnp.max(s, axis=0), s_new)
        e = jnp.exp(s - m[None])
        e_new = jnp.exp(s_new - m)
        den = jnp.sum(e, axis=0) + e_new
        outs.append((jnp.sum(e * vc, axis=0) + e_new * vn) / den)
        lses.append(m + jnp.log(den))
    mx = jnp.maximum(jnp.maximum(lses[0], lses[1]), lses[2])
    ws = [jnp.exp(l - mx) for l in lses]
    mixed = (ws[0] * outs[0] + ws[1] * outs[1] + ws[2] * outs[2]) / (ws[0] + ws[1] + ws[2])
    o_ref[...] = mixed * _silu(z_ref[CH_ZA:CH_ZA + A_HEADS, :])


def _attention_step(zs_h, caches, shifted, slopes):
    b = zs_h.shape[0]
    z3 = zs_h.reshape(b, N_HEAD_CHUNKS, A_HEAD_DIM)
    views = [c.reshape(b, BAND, DILATIONS[g], 2, A_HEADS, A_HEAD_DIM) for g, c in enumerate(caches)]
    cache_spec = pl.BlockSpec((None, BAND, None, 2, A_HEADS, A_HEAD_DIM),
                              lambda i: (i, 0, 0, 0, 0, 0))
    last_row = [pl.BlockSpec((None, 1, 2, A_HEADS, A_HEAD_DIM),
                             functools.partial(lambda i, w: (i, w - 1, 0, 0, 0), w=c.shape[1]))
                for c in shifted]
    out = pl.pallas_call(
        _attn_step_kernel,
        grid=(b,),
        in_specs=[pl.BlockSpec((N_GROUPS, A_HEADS, 1), lambda i: (0, 0, 0)),
                  pl.BlockSpec((None, N_HEAD_CHUNKS, A_HEAD_DIM), lambda i: (i, 0, 0)),
                  cache_spec, cache_spec, cache_spec] + [pl.BlockSpec(memory_space=pl.ANY)] * len(shifted),
        out_specs=[pl.BlockSpec((None, A_HEADS, A_HEAD_DIM), lambda i: (i, 0, 0))] + last_row,
        out_shape=[jax.ShapeDtypeStruct((b, A_HEADS, A_HEAD_DIM), F32)]
                  + [jax.ShapeDtypeStruct(c.shape, c.dtype) for c in shifted],
        input_output_aliases={5: 1, 6: 2, 7: 3},
        compiler_params=_params(("parallel",)),
        name="dilated_attn_step",
    )(slopes.reshape(N_GROUPS, A_HEADS, 1), z3, *views, *shifted)
    return out[0].reshape(b, A_WIDTH), out[1:]


def _retention_step_kernel(lg_ref, q_ref, k_ref, v_ref, zr_ref, gret_ref, r_ref, y_ref, rout_ref):
    for hh in range(R_HEADS):
        cols = slice(hh * R_DIM, (hh + 1) * R_DIM)
        gamma = jnp.exp(jnp.full((1, 1), 1.0, F32) * lg_ref[hh])
        q = q_ref[:, cols]
        k = k_ref[:, cols] * (R_DIM ** -0.5)
        v = v_ref[:, cols]
        r_prev = r_ref[hh]
        qb = jnp.broadcast_to(q, (8, R_DIM)).astype(BF16)
        cross = jnp.dot(qb, r_prev.astype(BF16), preferred_element_type=F32)[0:1] * gamma
        inner = jnp.sum(q * k, axis=-1, keepdims=True) * v
        k_col = jnp.broadcast_to(k, (R_DIM, R_DIM)).T
        rout_ref[hh] = r_prev * gamma + k_col * v
        y = inner + cross
        y = y * lax.rsqrt(jnp.mean(y * y, axis=-1, keepdims=True) + EPS)
        y_ref[:, cols] = (y * gret_ref[:, cols]) * _silu(zr_ref[:, cols])


def _retention_step(zs_r, state, log_g, g_ret):
    b = zs_r.shape[0]
    z3 = zs_r.reshape(b, 1, REST_COLS)

    def col_spec(col):
        return pl.BlockSpec((None, 1, R_WIDTH), lambda i, lg: (i, 0, (col - HEAD_COLS) // R_WIDTH))

    state_spec = pl.BlockSpec((None, R_HEADS, R_DIM, R_DIM), lambda i, lg: (i, 0, 0, 0))
    grid_spec = pltpu.PrefetchScalarGridSpec(
        num_scalar_prefetch=1,
        grid=(b,),
        in_specs=[col_spec(COL_QR), col_spec(COL_KR), col_spec(COL_VR), col_spec(COL_ZR),
                  pl.BlockSpec((1, R_WIDTH), lambda i, lg: (0, 0)), state_spec],
        out_specs=[pl.BlockSpec((None, 1, R_WIDTH), lambda i, lg: (i, 0, 0)), state_spec],
    )
    y, new_state = pl.pallas_call(
        _retention_step_kernel, grid_spec=grid_spec,
        out_shape=[jax.ShapeDtypeStruct((b, 1, R_WIDTH), F32),
                   jax.ShapeDtypeStruct(state.shape, F32)],
        compiler_params=_params(("parallel",)),
        name="retention_step",
    )(log_g, z3, z3, z3, z3, g_ret.reshape(1, R_WIDTH), state)
    return y.reshape(b, R_WIDTH), new_state


def _alibi_slopes():
    n = N_GROUPS * A_HEADS
    return jnp.exp2(-8.0 * (jnp.arange(n, dtype=F32) + 1.0) / n).reshape(N_GROUPS, A_HEADS)


def _retention_log_decay():
    return jnp.log1p(-jnp.exp2(-5.0 - jnp.arange(R_HEADS, dtype=F32)))


def kernel(x_prompt, x_sample, cache_win0, cache_win1, cache_win2, state_ret, p_prompt, p_sample,
           g_pre, w_in, g_ret, w_a_out, w_r_out, w_o, g_post, w_ple_gate, w_ple_proj):
    batch, seq, _ = x_prompt.shape
    dec_batch = x_sample.shape[0]
    assert g_pre.shape[0] == 1 and x_sample.shape[1] == 1
    assert seq % ATT_TILE == 0
    m = batch * seq
    slopes = _alibi_slopes()
    log_g = _retention_log_decay()

    xp = x_prompt.reshape(m, D_MODEL)
    xs = x_sample.reshape(dec_batch, D_MODEL)
    u = _rmsnorm(xp, g_pre[0], TM_NORM)
    us = _rmsnorm(xs, g_pre[0], dec_batch)
    caches = (cache_win0[0], cache_win1[0], cache_win2[0])
    flat = [c.reshape((-1,) + c.shape[2:]) for c in caches]
    casts = [(0, functools.partial(_cast_rider, wt[0], CAST_ROWS))
             for wt in (w_a_out, w_r_out, w_o, w_ple_gate, w_ple_proj)]
    zh, zs_h, wa, wr, wo, wg, wp, sh1, sh0 = _in_proj(
        u, us, w_in[0], 0, HEAD_COLS, F32, TM_IN, TN_IN, True, (COL_ZA, COL_QR), casts,
        shift_caches=[flat[1], flat[0]])
    layout = [functools.partial(_window_prompt_rider, zh, g, batch, seq, RIDER_ROWS) for g in range(N_GROUPS)]
    np0 = batch * 2 * (min(WINDOWS[0], seq) // min(RIDER_ROWS, WINDOWS[0]))
    np1 = batch * 2 * (min(WINDOWS[1], seq) // min(RIDER_ROWS, WINDOWS[1]))
    zr, zs_r, wp2, wp1, wp0, sh2 = _in_proj(
        u, us, w_in[0], HEAD_COLS, REST_COLS, BF16, TM_IN, TN_IN, False, (COL_ZR, COL_GA),
        [(np0 + np1, layout[2]), (np0, layout[1]), (0, layout[0])], shift_caches=[flat[2]])
    win_p = [wp.reshape(1, batch, -1, 2, A_HEADS, A_HEAD_DIM) for wp in (wp0, wp1, wp2)]
    shifted = [s.reshape(c.shape) for s, c in zip((sh0, sh1, sh2), caches)]
    ga_blk = (COL_GA - HEAD_COLS) // D_MODEL

    ya = _attention(zh, slopes, batch, seq)
    yr, ret_prompt = _retention(zr, log_g, g_ret[0], batch, seq, RET_ROWS, RET_HEADS)
    ya_s, win_s = _attention_step(zs_h, caches, shifted, slopes)
    yr_s, ret_sample = _retention_step(zs_r, state_ret[0], log_g, g_ret[0])
    y_prompt, y_sample = _out_proj(
        ya, yr, zr, ya_s, yr_s, zs_r, ga_blk, ga_blk + 1,
        xp, p_prompt[0].reshape(m, PLE_DIM), xs, p_sample[0].reshape(dec_batch, PLE_DIM),
        wa, wr, wo, g_post[0], wg, wp, TM_OUT)
    y_prompt = y_prompt.reshape(batch, seq, D_MODEL)
    y_sample = y_sample.reshape(dec_batch, 1, D_MODEL)

    return (y_prompt, y_sample, win_p[0], win_p[1], win_p[2], ret_prompt[None],
            win_s[0][None], win_s[1][None], win_s[2][None], ret_sample[None])
```
